```python
import math, functools
import jax, jax.numpy as jnp
from jax import lax
import numpy as np

D_MODEL = 1024
BATCH = 4
SEQ = 4096
DEPTH = 1
DEC_BATCH = 128
DEC_SEQ = 8
PAST_LEN = 2048
PAGE_SIZE = 128

ATTN_GROUPS = ((128, 1), (512, 4), (2048, 16))
N_ATTN_GROUPS = len(ATTN_GROUPS)
HEADS_PER_GROUP = 8
HEAD_DIM = 64
ATTN_GROUP_W = HEADS_PER_GROUP * HEAD_DIM
QKV_W = N_ATTN_GROUPS * 3 * ATTN_GROUP_W
ATTN_OUT_W = ATTN_GROUP_W
BAND = 128
ATTN_SCALE = HEAD_DIM ** -0.5
D_INNER = 2 * D_MODEL
SSM_HEAD_DIM = 64
SSM_HEADS = D_INNER // SSM_HEAD_DIM
SSM_GROUPS = 4
HEADS_PER_SSM_GROUP = SSM_HEADS // SSM_GROUPS
D_STATE = 128
CONV_W = 4
SSD_CHUNK = 128
D_XBC = D_INNER + 2 * SSM_GROUPS * D_STATE
DT_MIN = 0.001
DT_MAX = 0.1
A_MIN = 1.0
A_MAX = 16.0
D_IN = QKV_W + D_INNER + D_XBC + SSM_HEADS + 2 * D_MODEL
N_EXPERTS = 32
TOP_K = 4
D_FF = D_MODEL
SWIGLU_ALPHA = 1.702
SWIGLU_LIMIT = 7.0
MOE_BLOCK = 128
LN_EPS = 1e-5
RMS_EPS = 1e-5
DN_ALPHA = (2 * DEPTH) ** 0.25
DN_BETA = (8 * DEPTH) ** -0.25

kernel_name = "dilated_attn_ssd_gated_moe_decoder_step"


def layer_norm(x, g, b):
    xf = x.astype(jnp.float32)
    mu = jnp.mean(xf, axis=-1, keepdims=True)
    var = jnp.mean(jnp.square(xf - mu), axis=-1, keepdims=True)
    return ((xf - mu) * lax.rsqrt(var + LN_EPS) * g + b).astype(x.dtype)


def grouped_rms_norm(y, g):
    yf = y.astype(jnp.float32).reshape(*y.shape[:-1], SSM_GROUPS, D_INNER // SSM_GROUPS)
    yf = yf * lax.rsqrt(jnp.mean(jnp.square(yf), axis=-1, keepdims=True) + RMS_EPS)
    return (yf.reshape(y.shape) * g).astype(y.dtype)


def attend(scores, mask, v, spec):
    s = jnp.where(mask, scores, -jnp.inf)
    m = jnp.max(s, axis=-1, keepdims=True)
    p = jnp.exp(s - m)
    denom = jnp.sum(p, axis=-1, keepdims=True)
    o = jnp.einsum(spec, (p / denom).astype(v.dtype), v)
    return o, (m + jnp.log(denom))[..., 0]


def dilated_attn_prompt(q, k, v, window, dilation):
    bsz, seq = q.shape[:2]
    n_sub = seq // dilation
    n_blk = -(-n_sub // BAND)
    sub_pad = n_blk * BAND

    def strided(a):
        a = a.reshape(bsz, n_sub, dilation, HEADS_PER_GROUP, HEAD_DIM)
        a = jnp.pad(a, ((0, 0), (0, sub_pad - n_sub), (0, 0), (0, 0), (0, 0)))
        return a.reshape(bsz, n_blk, BAND, dilation, HEADS_PER_GROUP, HEAD_DIM)

    def with_prev(a):
        prev = jnp.pad(a[:, :-1], ((0, 0), (1, 0), (0, 0), (0, 0), (0, 0), (0, 0)))
        return jnp.concatenate([prev, a], axis=2)

    qb = strided(q)
    kc = with_prev(strided(k))
    vc = with_prev(strided(v))
    scores = jnp.einsum("bnqrhd,bnkrhd->bnrhqk", qb, kc).astype(jnp.float32) * ATTN_SCALE
    qi = jnp.arange(BAND)[:, None]
    kj = jnp.arange(2 * BAND)[None, :]
    dist = qi + BAND - kj
    key_pos = jnp.arange(n_blk)[:, None, None] * BAND + kj[None] - BAND
    mask = (dist >= 0) & (dist <= window // dilation) & (key_pos >= 0)
    o, lse = attend(scores, mask[None, :, None, None], vc, "bnrhqk,bnkrhd->bnqrhd")
    o = o.reshape(bsz, sub_pad, dilation, HEADS_PER_GROUP, HEAD_DIM)[:, :n_sub]
    lse = jnp.transpose(lse, (0, 1, 4, 2, 3)).reshape(bsz, sub_pad, dilation, HEADS_PER_GROUP)[:, :n_sub]
    return (o.reshape(bsz, seq, HEADS_PER_GROUP, HEAD_DIM),
            lse.reshape(bsz, seq, HEADS_PER_GROUP))


def dilated_attn_sample(q, k_new, v_new, kv_buf, window, dilation):
    buf_len = kv_buf.shape[1]
    n_new = q.shape[1]
    k_all = jnp.concatenate([kv_buf[:, :, 0], k_new], axis=1)
    v_all = jnp.concatenate([kv_buf[:, :, 1], v_new], axis=1)
    steps = jnp.arange(window // dilation + 1) * dilation
    idx = buf_len + jnp.arange(n_new)[:, None] - steps[None, :]
    valid = idx >= 0
    idx = jnp.maximum(idx, 0)
    kg = k_all[:, idx]
    vg = v_all[:, idx]
    scores = jnp.einsum("bshd,bsjhd->bhsj", q, kg).astype(jnp.float32) * ATTN_SCALE
    o, lse = attend(scores, valid[None, None], vg, "bhsj,bsjhd->bshd")
    return o, jnp.transpose(lse, (0, 2, 1))


def combine_groups(outs, lses):
    w = jax.nn.softmax(jnp.stack(lses, axis=0), axis=0)
    o = jnp.einsum("gbth,gbthd->bthd", w.astype(outs[0].dtype), jnp.stack(outs, axis=0))
    return o.reshape(*o.shape[:2], ATTN_OUT_W)


def attn_prompt(qkv):
    outs, lses, rows = [], [], []
    for g, (window, dilation) in enumerate(ATTN_GROUPS):
        q, k, v = qkv[:, :, g, 0], qkv[:, :, g, 1], qkv[:, :, g, 2]
        o, lse = dilated_attn_prompt(q, k, v, window, dilation)
        outs.append(o)
        lses.append(lse)
        keep = min(window, k.shape[1])
        rows.append(jnp.stack([k[:, -keep:], v[:, -keep:]], axis=2))
    return combine_groups(outs, lses), rows


def attn_sample(qkv, kv_bufs):
    outs, lses, rows = [], [], []
    for g, (window, dilation) in enumerate(ATTN_GROUPS):
        q, k, v = qkv[:, :, g, 0], qkv[:, :, g, 1], qkv[:, :, g, 2]
        o, lse = dilated_attn_sample(q, k, v, kv_bufs[g], window, dilation)
        outs.append(o)
        lses.append(lse)
        rows.append(jnp.stack([k, v], axis=2))
    return combine_groups(outs, lses), rows


def causal_conv(xbc, conv_state, w, b):
    n = xbc.shape[1]
    xp = jnp.concatenate([conv_state, xbc], axis=1)
    y = sum(xp[:, j:j + n] * w[j] for j in range(CONV_W)) + b
    return jax.nn.silu(y), xp[:, -(CONV_W - 1):]


def ssd_scan(x, dt, a, bm, cm, h0):
    bsz, seq = x.shape[:2]
    chunk = min(SSD_CHUNK, seq)
    n_chunks = -(-seq // chunk)
    pad = n_chunks * chunk - seq

    def chunked(t):
        t = jnp.pad(t, [(0, 0), (0, pad)] + [(0, 0)] * (t.ndim - 2))
        return jnp.moveaxis(t.reshape(bsz, n_chunks, chunk, *t.shape[2:]), 1, 0)

    f32 = jnp.float32
    xc, dtc, bc, cc = (chunked(x.astype(f32)), chunked(dt), chunked(bm.astype(f32)), chunked(cm.astype(f32)))
    causal = jnp.tril(jnp.ones((chunk, chunk), bool))[None, :, :, None, None]

    def step(h, inp):
        xq, dq, bq, cq = inp
        cum = jnp.cumsum(dq * a, axis=1)
        seg = cum[:, :, None] - cum[:, None]
        decay = jnp.where(causal, jnp.exp(jnp.where(causal, seg, 0.0)), 0.0)
        cb = jnp.einsum("bign,bjgn->bijg", cq, bq)
        w = cb[..., None] * decay * dq[:, None]
        y = jnp.einsum("bijge,bjgep->bigep", w, xq)
        y = y + jnp.exp(cum)[..., None] * jnp.einsum("bign,bgepn->bigep", cq, h)
        to_end = jnp.exp(cum[:, -1:] - cum) * dq
        h = (jnp.exp(cum[:, -1])[..., None, None] * h
             + jnp.einsum("bjge,bjgep,bjgn->bgepn", to_end, xq, bq))
        return h, y

    h_last, ys = lax.scan(step, h0.astype(f32), (xc, dtc, bc, cc))
    y = jnp.moveaxis(ys, 0, 1).reshape(bsz, n_chunks * chunk, *x.shape[2:])[:, :seq]
    return y.astype(x.dtype), h_last.astype(x.dtype)


def ssd_branch(z, xbc, dt_raw, conv_state, ssm_state, lp):
    bsz, n = xbc.shape[:2]
    xbc, new_conv = causal_conv(xbc, conv_state, lp["conv_w"], lp["conv_b"])
    gn = SSM_GROUPS * D_STATE
    xs = xbc[..., :D_INNER].reshape(bsz, n, SSM_GROUPS, HEADS_PER_SSM_GROUP, SSM_HEAD_DIM)
    bm = xbc[..., D_INNER:D_INNER + gn].reshape(bsz, n, SSM_GROUPS, D_STATE)
    cm = xbc[..., D_INNER + gn:].reshape(bsz, n, SSM_GROUPS, D_STATE)
    dt = jax.nn.softplus(dt_raw.astype(jnp.float32) + lp["dt_bias"]).reshape(bsz, n, SSM_GROUPS, HEADS_PER_SSM_GROUP)
    a = -jnp.exp(lp["a_log"].astype(jnp.float32)).reshape(SSM_GROUPS, HEADS_PER_SSM_GROUP)
    h0 = ssm_state.reshape(bsz, SSM_GROUPS, HEADS_PER_SSM_GROUP, SSM_HEAD_DIM, D_STATE)
    y, h_last = ssd_scan(xs, dt, a, bm, cm, h0)
    y = y + lp["d_skip"].reshape(SSM_GROUPS, HEADS_PER_SSM_GROUP)[:, :, None] * xs
    y = y.reshape(bsz, n, D_INNER) * jax.nn.silu(z)
    y = grouped_rms_norm(y, lp["ssm_norm_g"])
    return y, new_conv, h_last.reshape(bsz, SSM_HEADS, SSM_HEAD_DIM, D_STATE)


def clamped_swiglu(gate, up):
    gate = jnp.minimum(gate, SWIGLU_LIMIT)
    up = jnp.clip(up, -SWIGLU_LIMIT, SWIGLU_LIMIT)
    return gate * jax.nn.sigmoid(SWIGLU_ALPHA * gate) * (up + 1.0)


def moe_ffn(u, lp):
    lead = u.shape[:-1]
    xt = u.reshape(-1, D_MODEL)
    n_tok = xt.shape[0]
    logits = (xt @ lp["w_router"] + lp["b_router"]).astype(jnp.float32)
    top_logit, top_idx = lax.top_k(logits, TOP_K)
    top_gate = jax.nn.softmax(top_logit, axis=-1)
    n_assign = n_tok * TOP_K
    e_flat = top_idx.reshape(-1)
    tok_flat = jnp.repeat(jnp.arange(n_tok, dtype=jnp.int32), TOP_K)
    g_flat = top_gate.reshape(-1)
    order = jnp.argsort(e_flat)
    e_sorted, tok_sorted, g_sorted = e_flat[order], tok_flat[order], g_flat[order]
    counts = jax.ops.segment_sum(jnp.ones((n_assign,), jnp.int32), e_flat, num_segments=N_EXPERTS)
    starts = jnp.cumsum(counts) - counts
    padded = (counts + MOE_BLOCK - 1) // MOE_BLOCK * MOE_BLOCK
    pad_ends = jnp.cumsum(padded)
    pad_starts = pad_ends - padded
    dest = pad_starts[e_sorted] + jnp.arange(n_assign, dtype=jnp.int32) - starts[e_sorted]
    n_blocks = -(-n_assign // MOE_BLOCK) + N_EXPERTS
    n_rows = n_blocks * MOE_BLOCK
    row_tok = jnp.full((n_rows,), n_tok, jnp.int32).at[dest].set(tok_sorted)
    row_gate = jnp.zeros((n_rows,), jnp.float32).at[dest].set(g_sorted)
    block_expert = jnp.minimum(
        jnp.searchsorted(pad_ends, jnp.arange(n_blocks, dtype=jnp.int32) * MOE_BLOCK, side="right"),
        N_EXPERTS - 1)

    def expert_block(args):
        tok, e = args
        xb = jnp.take(xt, tok, axis=0, mode="clip")
        h = clamped_swiglu(xb @ lp["w_gate"][e] + lp["b_gate"][e], xb @ lp["w_up"][e] + lp["b_up"][e])
        return h @ lp["w_down"][e] + lp["b_down"][e]

    y_rows = lax.map(expert_block, (row_tok.reshape(n_blocks, MOE_BLOCK), block_expert))
    y_rows = (y_rows.reshape(n_rows, D_MODEL) * row_gate[:, None]).astype(xt.dtype)
    y = jnp.zeros_like(xt).at[row_tok].add(y_rows, mode="drop")
    return y.reshape(*lead, D_MODEL)


def decoder_layer(x, c, lp, attn_fn, conv_state, ssm_state):
    mod = jnp.einsum("bd,de->be", jax.nn.silu(c), lp["w_ada"]) + lp["b_ada"]
    sh1, sc1, gt1, sh2, sc2, gt2 = jnp.split(mod[:, None, :], 6, axis=-1)
    u = x * (1.0 + sc1) + sh1
    proj = u @ lp["w_in"]
    o = QKV_W
    qkv = proj[..., :o].reshape(*proj.shape[:2], N_ATTN_GROUPS, 3, HEADS_PER_GROUP, HEAD_DIM)
    z = proj[..., o:o + D_INNER]
    o += D_INNER
    xbc = proj[..., o:o + D_XBC]
    o += D_XBC
    dt_raw = proj[..., o:o + SSM_HEADS]
    o += SSM_HEADS
    gate_a = proj[..., o:o + D_MODEL]
    gate_s = proj[..., o + D_MODEL:]
    attn_o, kv_rows = attn_fn(qkv)
    ssd_o, new_conv, new_ssm = ssd_branch(z, xbc, dt_raw, conv_state, ssm_state, lp)
    merged = (jax.nn.sigmoid(gate_a) * (attn_o @ lp["w_attn_br"])
              + jax.nn.sigmoid(gate_s) * (ssd_o @ lp["w_ssd_br"]))
    x = layer_norm(DN_ALPHA * x + gt1 * (merged @ lp["w_out"]), lp["ln_mix_g"], lp["ln_mix_b"])
    u2 = x * (1.0 + sc2) + sh2
    x = layer_norm(DN_ALPHA * x + gt2 * moe_ffn(u2, lp), lp["ln_ffn_g"], lp["ln_ffn_b"])
    return x, kv_rows, new_conv, new_ssm


def setup_inputs(seed: int = 0) -> dict:
    key = jax.random.key(seed)
    keys = iter(jax.random.split(key, 48))
    f32 = jnp.float32

    def nrm(shape, scale=1.0):
        return jax.random.normal(next(keys), shape, f32) * scale

    def gain(shape):
        return 1.0 + nrm(shape, 0.1)

    qkv_scale = jnp.ones((N_ATTN_GROUPS, 3, ATTN_GROUP_W), f32).at[:, 2].set(DN_BETA).reshape(-1)
    in_col_scale = jnp.concatenate([qkv_scale, jnp.ones((D_IN - QKV_W,), f32)])
    dt_init = jnp.exp(jax.random.uniform(next(keys), (DEPTH, SSM_HEADS), f32, math.log(DT_MIN), math.log(DT_MAX)))
    return {
        "x_prompt": nrm((BATCH, SEQ, D_MODEL)),
        "x_sample": nrm((DEC_BATCH, DEC_SEQ, D_MODEL)),
        "c_prompt": nrm((BATCH, D_MODEL)),
        "c_sample": nrm((DEC_BATCH, D_MODEL)),
        "cache_kv_w128": nrm((DEPTH, DEC_BATCH, min(ATTN_GROUPS[0][0], PAST_LEN), 2, HEADS_PER_GROUP, HEAD_DIM)),
        "cache_kv_w512": nrm((DEPTH, DEC_BATCH, min(ATTN_GROUPS[1][0], PAST_LEN), 2, HEADS_PER_GROUP, HEAD_DIM)),
        "cache_kv_w2048": nrm((DEPTH, DEC_BATCH, min(ATTN_GROUPS[2][0], PAST_LEN), 2, HEADS_PER_GROUP, HEAD_DIM)),
        "state_conv": nrm((DEPTH, DEC_BATCH, CONV_W - 1, D_XBC)),
        "state_ssm": nrm((DEPTH, DEC_BATCH, SSM_HEADS, SSM_HEAD_DIM, D_STATE), 0.3),
        "ln_emb_g": gain((D_MODEL,)),
        "ln_emb_b": nrm((D_MODEL,), 0.01),
        "w_ada": nrm((DEPTH, D_MODEL, 6 * D_MODEL), D_MODEL ** -0.5),
        "b_ada": nrm((DEPTH, 6 * D_MODEL), 0.01),
        "w_in": nrm((DEPTH, D_MODEL, D_IN), D_MODEL ** -0.5) * in_col_scale,
        "conv_w": nrm((DEPTH, CONV_W, D_XBC), CONV_W ** -0.5),
        "conv_b": nrm((DEPTH, D_XBC), 0.01),
        "dt_bias": dt_init + jnp.log(-jnp.expm1(-dt_init)),
        "a_log": jnp.log(jax.random.uniform(next(keys), (DEPTH, SSM_HEADS), f32, A_MIN, A_MAX)),
        "d_skip": gain((DEPTH, SSM_HEADS)),
        "ssm_norm_g": gain((DEPTH, D_INNER)),
        "w_attn_br": nrm((DEPTH, ATTN_OUT_W, D_MODEL), ATTN_OUT_W ** -0.5 * DN_BETA),
        "w_ssd_br": nrm((DEPTH, D_INNER, D_MODEL), D_INNER ** -0.5 * DN_BETA),
        "w_out": nrm((DEPTH, D_MODEL, D_MODEL), D_MODEL ** -0.5 * DN_BETA),
        "ln_mix_g": gain((DEPTH, D_MODEL)),
        "ln_mix_b": nrm((DEPTH, D_MODEL), 0.01),
        "w_router": nrm((DEPTH, D_MODEL, N_EXPERTS), D_MODEL ** -0.5),
        "b_router": nrm((DEPTH, N_EXPERTS), 0.01),
        "w_gate": nrm((DEPTH, N_EXPERTS, D_MODEL, D_FF), D_MODEL ** -0.5),
        "b_gate": nrm((DEPTH, N_EXPERTS, D_FF), 0.01),
        "w_up": nrm((DEPTH, N_EXPERTS, D_MODEL, D_FF), D_MODEL ** -0.5),
        "b_up": nrm((DEPTH, N_EXPERTS, D_FF), 0.01),
        "w_down": nrm((DEPTH, N_EXPERTS, D_FF, D_MODEL), D_FF ** -0.5 * DN_BETA),
        "b_down": nrm((DEPTH, N_EXPERTS, D_MODEL), 0.01),
        "ln_ffn_g": gain((DEPTH, D_MODEL)),
        "ln_ffn_b": nrm((DEPTH, D_MODEL), 0.01),
    }


def reference(x_prompt, x_sample, c_prompt, c_sample, cache_kv_w128, cache_kv_w512, cache_kv_w2048,
              state_conv, state_ssm, ln_emb_g, ln_emb_b, w_ada, b_ada, w_in, conv_w, conv_b, dt_bias,
              a_log, d_skip, ssm_norm_g, w_attn_br, w_ssd_br, w_out, ln_mix_g, ln_mix_b, w_router,
              b_router, w_gate, b_gate, w_up, b_up, w_down, b_down, ln_ffn_g, ln_ffn_b):
    kv_caches = (cache_kv_w128, cache_kv_w512, cache_kv_w2048)
    xp = layer_norm(x_prompt, ln_emb_g, ln_emb_b)
    xs = layer_norm(x_sample, ln_emb_g, ln_emb_b)
    kv_p = [[] for _ in ATTN_GROUPS]
    kv_s = [[] for _ in ATTN_GROUPS]
    conv_p, conv_s, ssm_p, ssm_s = [], [], [], []
    for l in range(DEPTH):
        lp = {"w_ada": w_ada[l], "b_ada": b_ada[l], "w_in": w_in[l], "conv_w": conv_w[l],
              "conv_b": conv_b[l], "dt_bias": dt_bias[l], "a_log": a_log[l], "d_skip": d_skip[l],
              "ssm_norm_g": ssm_norm_g[l], "w_attn_br": w_attn_br[l], "w_ssd_br": w_ssd_br[l],
              "w_out": w_out[l], "ln_mix_g": ln_mix_g[l], "ln_mix_b": ln_mix_b[l],
              "w_router": w_router[l], "b_router": b_router[l], "w_gate": w_gate[l],
              "b_gate": b_gate[l], "w_up": w_up[l], "b_up": b_up[l], "w_down": w_down[l],
              "b_down": b_down[l], "ln_ffn_g": ln_ffn_g[l], "ln_ffn_b": ln_ffn_b[l]}
        zero_conv = jnp.zeros((xp.shape[0], CONV_W - 1, D_XBC), xp.dtype)
        zero_ssm = jnp.zeros((xp.shape[0], SSM_HEADS, SSM_HEAD_DIM, D_STATE), xp.dtype)
        xp, rows_p, cp, sp = decoder_layer(xp, c_prompt, lp, attn_prompt, zero_conv, zero_ssm)
        sample_attn = functools.partial(attn_sample, kv_bufs=tuple(kc[l] for kc in kv_caches))
        xs, rows_s, cs, ss = decoder_layer(xs, c_sample, lp, sample_attn, state_conv[l], state_ssm[l])
        for g in range(N_ATTN_GROUPS):
            kv_p[g].append(rows_p[g])
            kv_s[g].append(rows_s[g])
        conv_p.append(cp)
        conv_s.append(cs)
        ssm_p.append(sp)
        ssm_s.append(ss)
    new_kv_p = [jnp.stack(r, axis=0) for r in kv_p]
    new_kv_s = [jnp.stack(r, axis=0) for r in kv_s]
    return (xp, xs, new_kv_p[0], new_kv_s[0], new_kv_p[1], new_kv_s[1], new_kv_p[2], new_kv_s[2],
            jnp.stack(conv_p, axis=0), jnp.stack(conv_s, axis=0), jnp.stack(ssm_p, axis=0), jnp.stack(ssm_s, axis=0))
```

```python
import functools
import math

import jax
import jax.numpy as jnp
from jax import lax
from jax.experimental import pallas as pl
from jax.experimental.pallas import tpu as pltpu

f32 = jnp.float32
bf16 = jnp.bfloat16

ATTN_GROUPS = ((128, 1), (512, 4), (2048, 16))
N_GROUPS = len(ATTN_GROUPS)
N_HEADS = 8
HEAD_DIM = 64
GROUP_W = N_HEADS * HEAD_DIM
QKV_W = N_GROUPS * 3 * GROUP_W
BAND = 128
ATTN_SCALE = HEAD_DIM ** -0.5
SSM_HEAD_DIM = 64
SSM_GROUPS = 4
D_STATE = 128
CONV_W = 4
N_EXPERTS = 32
TOP_K = 4
SWIGLU_ALPHA = 1.702
SWIGLU_LIMIT = 7.0
LN_EPS = 1e-5
RMS_EPS = 1e-5
DEPTH = 1
DN_ALPHA = (2 * DEPTH) ** 0.25
NEG = -1e30

LANES = 128
SUBLANES = 8
VMEM_LIMIT = 56 * 1024 * 1024


def _params(*sem):
    return pltpu.CompilerParams(dimension_semantics=sem, vmem_limit_bytes=VMEM_LIMIT)


def _silu(x):
    return x * jax.nn.sigmoid(x)


def _softplus(x):
    return jnp.maximum(x, 0.0) + jnp.log(1.0 + jnp.exp(-jnp.abs(x)))


def _split3(x):
    hi = x.astype(bf16)
    r1 = x - hi.astype(f32)
    mid = r1.astype(bf16)
    lo = (r1 - mid.astype(f32)).astype(bf16)
    return hi, mid, lo


def _dot(a, b):
    return jnp.dot(a, b, preferred_element_type=f32)


def _dot_nt(a, b):
    return lax.dot_general(a, b, (((1,), (1,)), ((), ())), preferred_element_type=f32)


def _dot_tn(a, b):
    return lax.dot_general(a, b, (((0,), (0,)), ((), ())), preferred_element_type=f32)


def _dot3_lhs(x, w):
    hi, mid, lo = _split3(x)
    return _dot(hi, w) + _dot(mid, w) + _dot(lo, w)


def _dot3_rhs(w, x):
    hi, mid, lo = _split3(x)
    return _dot(w, hi) + _dot(w, mid) + _dot(w, lo)


def _layer_norm(x, g, b):
    mu = jnp.mean(x, axis=-1, keepdims=True)
    xc = x - mu
    var = jnp.mean(xc * xc, axis=-1, keepdims=True)
    return xc * lax.rsqrt(var + LN_EPS) * g + b


def _mm_body(a_ref, w_ref, b_ref, o_ref, *, act):
    a = a_ref[...]
    if act == "silu":
        a = _silu(a.astype(f32))
    o_ref[...] = (_dot(a.astype(bf16), w_ref[...]) + b_ref[...]).astype(o_ref.dtype)


def _mm(a, w, bias, *, tm, tn, act=None, out_dtype=f32, name):
    m, k = a.shape
    n = w.shape[1]
    return pl.pallas_call(
        functools.partial(_mm_body, act=act),
        grid=(m // tm, n // tn),
        in_specs=[pl.BlockSpec((tm, k), lambda i, j: (i, 0)),
                  pl.BlockSpec((k, tn), lambda i, j: (0, j)),
                  pl.BlockSpec((1, tn), lambda i, j: (0, j))],
        out_specs=pl.BlockSpec((tm, tn), lambda i, j: (i, j)),
        out_shape=jax.ShapeDtypeStruct((m, n), out_dtype),
        compiler_params=_params("parallel", "arbitrary"),
        name=name,
    )(a, w, bias)


def _ln_mod_body(x_ref, g_ref, b_ref, sc_ref, sh_ref, xn_ref, u_ref):
    xn = _layer_norm(x_ref[...], g_ref[...], b_ref[...])
    xn_ref[...] = xn
    u_ref[...] = (xn * (1.0 + sc_ref[0]) + sh_ref[0]).astype(bf16)


def _mod_spec(mod3, tm, tiles_per_mod):
    return pl.BlockSpec((1, mod3.shape[1], mod3.shape[2]), lambda i: (i // tiles_per_mod, 0, 0))


def _ln_mod(x, g, b, sc, sh, *, tm, tiles_per_mod, name):
    n, d = x.shape
    row = pl.BlockSpec((tm, d), lambda i: (i, 0))
    vec = pl.BlockSpec((1, d), lambda i: (0, 0))
    return pl.pallas_call(
        _ln_mod_body,
        grid=(n // tm,),
        in_specs=[row, vec, vec, _mod_spec(sc, tm, tiles_per_mod), _mod_spec(sh, tm, tiles_per_mod)],
        out_specs=(row, row),
        out_shape=(jax.ShapeDtypeStruct((n, d), f32), jax.ShapeDtypeStruct((n, d), bf16)),
        compiler_params=_params("parallel"),
        name=name,
    )(x, g, b, sc, sh)


def _attn_prompt_body(q_ref, kc_ref, kp_ref, vc_ref, vp_ref, o_ref, l_ref):
    n = pl.program_id(2)
    q = (q_ref[0] * ATTN_SCALE).astype(bf16)
    k = jnp.concatenate([kp_ref[0], kc_ref[0]], axis=0).astype(bf16)
    v = jnp.concatenate([vp_ref[0], vc_ref[0]], axis=0).astype(bf16)
    qi = lax.broadcasted_iota(jnp.int32, (BAND, 2 * BAND), 0)
    kj = lax.broadcasted_iota(jnp.int32, (BAND, 2 * BAND), 1)
    dist = qi + BAND - kj
    mask = (dist >= 0) & (dist <= BAND) & ((kj >= BAND) | (n > 0))
    for h in range(N_HEADS):
        sl = slice(h * HEAD_DIM, (h + 1) * HEAD_DIM)
        s = jnp.where(mask, _dot_nt(q[:, sl], k[:, sl]), NEG)
        m = jnp.max(s, axis=-1, keepdims=True)
        p = jnp.exp(s - m)
        den = jnp.sum(p, axis=-1, keepdims=True)
        o_ref[0, :, sl] = _dot(p.astype(bf16), v[:, sl]) / den
        l_ref[0, :, sl] = jnp.broadcast_to(m + jnp.log(den), (BAND, HEAD_DIM))


def _attn_prompt_group(qkv, g, bsz, seq):
    _, dil = ATTN_GROUPS[g]
    n_sub = seq // dil
    nb = n_sub // BAND
    qkv_r = qkv.reshape(bsz, n_sub, dil * QKV_W)
    cpb = QKV_W // GROUP_W
    base = g * 3

    def spec(which, prev):
        if prev:
            return pl.BlockSpec((1, BAND, GROUP_W),
                                lambda b, r, n: (b, jnp.maximum(n - 1, 0), r * cpb + base + which))
        return pl.BlockSpec((1, BAND, GROUP_W), lambda b, r, n: (b, n, r * cpb + base + which))

    out_spec = pl.BlockSpec((1, BAND, GROUP_W), lambda b, r, n: (b, n, r))
    shape = jax.ShapeDtypeStruct((bsz, n_sub, dil * GROUP_W), f32)
    o, l = pl.pallas_call(
        _attn_prompt_body,
        grid=(bsz, dil, nb),
        in_specs=[spec(0, False), spec(1, False), spec(1, True), spec(2, False), spec(2, True)],
        out_specs=(out_spec, out_spec),
        out_shape=(shape, shape),
        compiler_params=_params("parallel", "parallel", "arbitrary"),
        name=f"attn_prompt_g{g}",
    )(qkv_r, qkv_r, qkv_r, qkv_r, qkv_r)
    return o.reshape(bsz * seq, GROUP_W), l.reshape(bsz * seq, GROUP_W)


def _attn_combine_body(o0, o1, o2, l0, l1, l2, a_ref):
    ls = [l0[...], l1[...], l2[...]]
    m = jnp.maximum(jnp.maximum(ls[0], ls[1]), ls[2])
    ws = [jnp.exp(l - m) for l in ls]
    num = ws[0] * o0[...] + ws[1] * o1[...] + ws[2] * o2[...]
    a_ref[...] = num / (ws[0] + ws[1] + ws[2])


def _attn_combine(outs, lses, *, tm):
    n, w = outs[0].shape
    row = pl.BlockSpec((tm, w), lambda i: (i, 0))
    return pl.pallas_call(
        _attn_combine_body,
        grid=(n // tm,),
        in_specs=[row] * 6,
        out_specs=row,
        out_shape=jax.ShapeDtypeStruct((n, w), f32),
        compiler_params=_params("parallel"),
        name="attn_combine",
    )(*outs, *lses)


def _attn_sample_body(qkv_ref, c0_ref, c1_ref, c2_ref, a_ref, *, n_new):
    caches = (c0_ref, c1_ref, c2_ref)
    kv_w = 2 * GROUP_W
    qkv = qkv_ref[...]
    ci = lax.broadcasted_iota(jnp.int32, (GROUP_W, LANES), 0) // HEAD_DIM
    hi = lax.broadcasted_iota(jnp.int32, (GROUP_W, LANES), 1)
    e_mat = (ci == hi).astype(bf16)
    cj = lax.broadcasted_iota(jnp.int32, (LANES, GROUP_W), 1) // HEAD_DIM
    hj = lax.broadcasted_iota(jnp.int32, (LANES, GROUP_W), 0)
    et_mat = (cj == hj).astype(bf16)
    n_rows = BAND + n_new
    row = lax.broadcasted_iota(jnp.int32, (n_rows, LANES), 0)
    for s in range(n_new):
        scores, vals = [], []
        for g, (_, dil) in enumerate(ATTN_GROUPS):
            r = s % dil
            base = g * 3 * GROUP_W
            q = qkv[s:s + 1, base:base + GROUP_W] * ATTN_SCALE
            k_new = qkv[:, base + GROUP_W:base + 2 * GROUP_W]
            v_new = qkv[:, base + 2 * GROUP_W:base + 3 * GROUP_W]
            k_buf = caches[g][0, :, r * kv_w:r * kv_w + GROUP_W]
            v_buf = caches[g][0, :, r * kv_w + GROUP_W:(r + 1) * kv_w]
            k_all = jnp.concatenate([k_buf, k_new], axis=0)
            sc = _dot((k_all * q).astype(bf16), e_mat)
            new_i = row - BAND
            valid = (((row < BAND) & (row >= s // dil))
                     | ((row >= BAND) & (new_i <= s) & (((s - new_i) & (dil - 1)) == 0)))
            scores.append(jnp.where(valid, sc, NEG))
            vals.append(jnp.concatenate([v_buf, v_new], axis=0))
        sc = jnp.concatenate(scores, axis=0)
        v_all = jnp.concatenate(vals, axis=0)
        m = jnp.max(sc, axis=0, keepdims=True)
        p = jnp.exp(sc - m)
        p = p / jnp.sum(p, axis=0, keepdims=True)
        pe = _dot(p.astype(bf16), et_mat)
        a_ref[s:s + 1, :] = jnp.sum(pe * v_all, axis=0, keepdims=True)


def _attn_sample(qkv, caches, bsz, n_new):
    kv_w = 2 * GROUP_W
    views, specs = [], []
    for g, (window, dil) in enumerate(ATTN_GROUPS):
        assert caches[g].shape[1] == window == BAND * dil
        views.append(caches[g].reshape(bsz, BAND, dil * kv_w))
        n_res = min(dil, n_new)
        specs.append(pl.BlockSpec((1, BAND, n_res * kv_w), lambda b: (b, 0, 0)))
    return pl.pallas_call(
        functools.partial(_attn_sample_body, n_new=n_new),
        grid=(bsz,),
        in_specs=[pl.BlockSpec((n_new, QKV_W), lambda b: (b, 0))] + specs,
        out_specs=pl.BlockSpec((n_new, GROUP_W), lambda b: (b, 0)),
        out_shape=jax.ShapeDtypeStruct((bsz * n_new, GROUP_W), f32),
        compiler_params=_params("parallel"),
        name="attn_sample",
    )(qkv, *views)


def _ssd_conv(ext_ref, n, cw_ref, cb_ref):
    y = cb_ref[...]
    for j in range(CONV_W):
        off = SUBLANES - (CONV_W - 1) + j
        y = y + ext_ref[off:off + n, :] * cw_ref[j:j + 1, :]
    return _silu(y)


def _gate_norm(y, z, g, d_inner):
    y = y * _silu(z)
    gw = d_inner // SSM_GROUPS
    parts = []
    for gi in range(SSM_GROUPS):
        yg = y[:, gi * gw:(gi + 1) * gw]
        parts.append(yg * lax.rsqrt(jnp.mean(yg * yg, axis=-1, keepdims=True) + RMS_EPS))
    return jnp.concatenate(parts, axis=-1) * g


def _ssd_prompt_body(xbc_ref, z_ref, dt_ref, cw_ref, cb_ref, dtb_ref, alog_ref, dtbt_ref, alogt_ref,
                     dskip_ref, ng_ref, ex_ref, s_ref, hl_ref, ext_ref, ht_ref, y_ref, *, d_inner, n_heads):
    c = pl.program_id(1)
    q = xbc_ref.shape[0]
    gw = d_inner // SSM_GROUPS
    hpg = n_heads // SSM_GROUPS

    @pl.when(c == 0)
    def _():
        ext_ref[0:SUBLANES, :] = jnp.zeros((SUBLANES, ext_ref.shape[1]), f32)
        ht_ref[...] = jnp.zeros_like(ht_ref)

    @pl.when(c > 0)
    def _():
        ext_ref[0:SUBLANES, :] = ext_ref[q:q + SUBLANES, :]

    ext_ref[SUBLANES:SUBLANES + q, :] = xbc_ref[...]
    xc = _ssd_conv(ext_ref, q, cw_ref, cb_ref)
    xs = xc[:, :d_inner]
    gn = SSM_GROUPS * D_STATE
    bm = xc[:, d_inner:d_inner + gn].astype(bf16)
    cm = xc[:, d_inner + gn:].astype(bf16)

    dtr = dt_ref[...]
    dt = _softplus(dtr + dtb_ref[...])
    da = dt * (-jnp.exp(alog_ref[...]))
    dt_t = _softplus(dtr.T[0:n_heads, :] + dtbt_ref[...])
    da_t = dt_t * (-jnp.exp(alogt_ref[...]))
    ii = lax.broadcasted_iota(jnp.int32, (q, q), 0)
    jj = lax.broadcasted_iota(jnp.int32, (q, q), 1)
    causal = ii >= jj
    lower = causal.astype(bf16)
    upper = (ii <= jj).astype(bf16)
    cum = _dot3_rhs(lower, da)
    cum_t = _dot3_lhs(da_t, upper)
    ex = ex_ref[...]
    cumx = _dot3_lhs(cum, ex)
    dtx = _dot3_lhs(dt, ex)
    clx = cumx[q - 1:q, :]
    ecum = jnp.exp(cumx)
    xd = (jnp.exp(clx - cumx) * dtx * xs).astype(bf16)
    xsb = xs.astype(bf16)

    for g in range(SSM_GROUPS):
        gl = slice(g * gw, (g + 1) * gw)
        cg = cm[:, g * D_STATE:(g + 1) * D_STATE]
        bg = bm[:, g * D_STATE:(g + 1) * D_STATE]
        cb = _dot_nt(cg, bg)
        h_old = ht_ref[g]
        y_ref[:, gl] = ecum[:, gl] * _dot(cg, h_old.astype(bf16))
        ht_ref[g] = jnp.exp(clx[:, gl]) * h_old + _dot_tn(bg, xd[:, gl])
        for e in range(hpg):
            hd = g * hpg + e
            hl = slice(hd * SSM_HEAD_DIM, (hd + 1) * SSM_HEAD_DIM)
            seg = jnp.broadcast_to(cum[:, hd:hd + 1], (q, q)) - jnp.broadcast_to(cum_t[hd:hd + 1, :], (q, q))
            w = cb * jnp.exp(jnp.where(causal, seg, NEG)) * jnp.broadcast_to(dt_t[hd:hd + 1, :], (q, q))
            y_ref[:, hl] += _dot(w.astype(bf16), xsb[:, hl])

    y = y_ref[...] + dskip_ref[...] * xs
    s_ref[...] = _gate_norm(y, z_ref[...], ng_ref[...], d_inner).astype(s_ref.dtype)

    @pl.when(c == pl.num_programs(1) - 1)
    def _():
        for g in range(SSM_GROUPS):
            hl_ref[0, g * gw:(g + 1) * gw, :] = ht_ref[g].T


def _ssd_prompt(xbc, z, dt, prm, bsz, seq, *, chunk):
    n, d_xbc = xbc.shape
    d_inner = z.shape[1]
    n_heads = d_inner // SSM_HEAD_DIM
    nc = seq // chunk
    gw = d_inner // SSM_GROUPS

    def row(w):
        return pl.BlockSpec((chunk, w), lambda b, c: (b * nc + c, 0))

    def full(a):
        return pl.BlockSpec(a.shape, lambda b, c: (0,) * a.ndim)

    consts = [prm["conv_w"], prm["conv_b"], prm["dtb_row"], prm["alog_row"], prm["dtb_t"], prm["alog_t"],
              prm["dskip_x"], prm["norm_g"], prm["ex"]]
    return pl.pallas_call(
        functools.partial(_ssd_prompt_body, d_inner=d_inner, n_heads=n_heads),
        grid=(bsz, nc),
        in_specs=[row(d_xbc), row(d_inner), row(LANES)] + [full(a) for a in consts],
        out_specs=(row(d_inner), pl.BlockSpec((1, d_inner, D_STATE), lambda b, c: (b, 0, 0))),
        out_shape=(jax.ShapeDtypeStruct((n, d_inner), bf16),
                   jax.ShapeDtypeStruct((bsz, d_inner, D_STATE), f32)),
        scratch_shapes=[pltpu.VMEM((SUBLANES + chunk + SUBLANES, d_xbc), f32),
                        pltpu.VMEM((SSM_GROUPS, D_STATE, gw), f32),
                        pltpu.VMEM((chunk, d_inner), f32)],
        compiler_params=_params("parallel", "arbitrary"),
        name="ssd_prompt",
    )(xbc, z, dt, *consts)


def _ssd_sample_body(xbc_ref, cs_ref, z_ref, dt_ref, h_ref, cw_ref, cb_ref, dtb_ref, alog_ref,
                     dskip_ref, ng_ref, ex_ref, rsel_ref, s_ref, hn_ref, ext_ref, *, d_inner, n_heads):
    q = xbc_ref.shape[0]
    gw = d_inner // SSM_GROUPS
    ext_ref[0:SUBLANES, :] = cs_ref[0]
    ext_ref[SUBLANES:SUBLANES + q, :] = xbc_ref[...]
    xc = _ssd_conv(ext_ref, q, cw_ref, cb_ref)
    xs = xc[:, :d_inner]
    gn = SSM_GROUPS * D_STATE
    bm = xc[:, d_inner:d_inner + gn]
    cm = xc[:, d_inner + gn:]

    dt = _softplus(dt_ref[...] + dtb_ref[...])
    da = dt * (-jnp.exp(alog_ref[...]))
    row = lax.broadcasted_iota(jnp.int32, (q, LANES), 0)
    cum = jnp.zeros((q, LANES), f32)
    for j in range(q):
        cum = cum + jnp.where(row >= j, da[j:j + 1, :], 0.0)
    ex = ex_ref[...]
    cumx = _dot3_lhs(cum, ex)
    dtx = _dot3_lhs(dt, ex)
    clx = cumx[q - 1:q, :]
    rowx = lax.broadcasted_iota(jnp.int32, (q, d_inner), 0)

    y = dskip_ref[...] * xs
    for j in range(q):
        prod = cm * bm[j:j + 1, :]
        cbx = jnp.concatenate(
            [jnp.broadcast_to(jnp.sum(prod[:, g * D_STATE:(g + 1) * D_STATE], axis=-1, keepdims=True), (q, gw))
             for g in range(SSM_GROUPS)], axis=-1)
        seg = jnp.where(rowx >= j, cumx - cumx[j:j + 1, :], NEG)
        y = y + cbx * jnp.exp(seg) * (dtx[j:j + 1, :] * xs[j:j + 1, :])

    xd = (jnp.exp(clx - cumx) * dtx * xs).astype(bf16)
    ones = jnp.ones((q, LANES), bf16)
    da_hi, da_mid, da_lo = _split3(da)
    cl_b = _dot_tn(da_hi, ones) + _dot_tn(da_mid, ones) + _dot_tn(da_lo, ones)
    decay = jnp.exp(_dot3_rhs(rsel_ref[...], cl_b))
    ecum = jnp.exp(cumx)
    cmb = cm.astype(bf16)
    bmb = bm.astype(bf16)
    ys = []
    for g in range(SSM_GROUPS):
        rows = slice(g * gw, (g + 1) * gw)
        hg = h_ref[0, rows, :]
        ys.append(_dot_nt(cmb[:, g * D_STATE:(g + 1) * D_STATE], hg.astype(bf16)))
        hn_ref[0, rows, :] = decay[rows, :] * hg + _dot_tn(xd[:, rows], bmb[:, g * D_STATE:(g + 1) * D_STATE])
    y = y + ecum * jnp.concatenate(ys, axis=-1)
    s_ref[...] = _gate_norm(y, z_ref[...], ng_ref[...], d_inner)


def _ssd_sample(xbc, conv_state8, z, dt, h0, prm, bsz, n_new):
    n, d_xbc = xbc.shape
    d_inner = z.shape[1]
    n_heads = d_inner // SSM_HEAD_DIM

    def row(w):
        return pl.BlockSpec((n_new, w), lambda b: (b, 0))

    def full(a):
        return pl.BlockSpec(a.shape, lambda b: (0,) * a.ndim)

    consts = [prm["conv_w"], prm["conv_b"], prm["dtb_row"], prm["alog_row"],
              prm["dskip_x"], prm["norm_g"], prm["ex"], prm["rsel"]]
    state = pl.BlockSpec((1, d_inner, D_STATE), lambda b: (b, 0, 0))
    return pl.pallas_call(
        functools.partial(_ssd_sample_body, d_inner=d_inner, n_heads=n_heads),
        grid=(bsz,),
        in_specs=[row(d_xbc), pl.BlockSpec((1, SUBLANES, d_xbc), lambda b: (b, 0, 0)), row(d_inner), row(LANES),
                  state] + [full(a) for a in consts],
        out_specs=(row(d_inner), state),
        out_shape=(jax.ShapeDtypeStruct((n, d_inner), f32),
                   jax.ShapeDtypeStruct((bsz, d_inner, D_STATE), f32)),
        scratch_shapes=[pltpu.VMEM((2 * SUBLANES, d_xbc), f32)],
        compiler_params=_params("parallel"),
        name="ssd_sample",
    )(xbc, conv_state8, z, dt, h0, *consts)


def _mixer_body(a_ref, s_ref, ga_ref, gs_ref, xn_ref, gt_ref, sc_ref, sh_ref, wa_ref, ws_ref, wo_ref,
                lg_ref, lb_ref, wrh_ref, wrl_ref, br_ref, x1_ref, u2_ref, lo_ref):
    m = (jax.nn.sigmoid(ga_ref[...]) * _dot(a_ref[...].astype(bf16), wa_ref[...])
         + jax.nn.sigmoid(gs_ref[...]) * _dot(s_ref[...].astype(bf16), ws_ref[...]))
    o = _dot(m.astype(bf16), wo_ref[...])
    x1 = _layer_norm(DN_ALPHA * xn_ref[...] + gt_ref[0] * o, lg_ref[...], lb_ref[...])
    x1_ref[...] = x1
    u2 = x1 * (1.0 + sc_ref[0]) + sh_ref[0]
    u2_ref[...] = u2
    hi, mid, lo = _split3(u2)
    wrh = wrh_ref[...]
    wrl = wrl_ref[...]
    lo_ref[...] = (_dot(hi, wrh) + (_dot(hi, wrl) + _dot(mid, wrh)) + (_dot(mid, wrl) + _dot(lo, wrh))
                   + br_ref[...])


def _mixer(a, s, gates, xn, gt, sc, sh, wts, *, tm, tiles_per_mod, name):
    n, d = xn.shape

    def row(w, col=0):
        return pl.BlockSpec((tm, w), lambda i: (i, col))

    def full(arr):
        return pl.BlockSpec(arr.shape, lambda i: (0,) * arr.ndim)

    consts = [wts["w_attn_br"], wts["w_ssd_br"], wts["w_out"], wts["ln_mix_g"], wts["ln_mix_b"],
              wts["w_router_hi"], wts["w_router_lo"], wts["b_router"]]
    return pl.pallas_call(
        _mixer_body,
        grid=(n // tm,),
        in_specs=[row(a.shape[1]), row(s.shape[1]), row(d, 0), row(d, 1), row(d),
                  _mod_spec(gt, tm, tiles_per_mod), _mod_spec(sc, tm, tiles_per_mod),
                  _mod_spec(sh, tm, tiles_per_mod)] + [full(c) for c in consts],
        out_specs=(row(d), row(d), row(LANES)),
        out_shape=(jax.ShapeDtypeStruct((n, d), f32), jax.ShapeDtypeStruct((n, d), f32),
                   jax.ShapeDtypeStruct((n, LANES), f32)),
        compiler_params=_params("parallel"),
        name=name,
    )(a, s, gates, gates, xn, gt, sc, sh, *consts)


def _moe_body(be_ref, nv_ref, tok_ref, slot_ref, gate_ref, x_hbm, wg_ref, bg_ref, wu_ref, bu_ref, wd_ref, bd_ref,
              y_hbm, xbuf, ybuf, sem_in, sem_out, *, bm, n_slots):
    i = pl.program_id(0)

    def in_copy(r, tok):
        return pltpu.make_async_copy(x_hbm.at[pl.ds(tok, 1), :], xbuf.at[pl.ds(r, 1), :], sem_in)

    def out_copy(r, slot):
        return pltpu.make_async_copy(ybuf.at[pl.ds(r, 1), :], y_hbm.at[pl.ds(slot, 1), :], sem_out)

    @pl.when(i < nv_ref[0])
    def _():
        def start_in(r, carry):
            in_copy(r, tok_ref[0, 0, r]).start()
            return carry

        def wait_in(r, carry):
            in_copy(r, 0).wait()
            return carry

        lax.fori_loop(0, bm, start_in, 0)
        lax.fori_loop(0, bm, wait_in, 0)
        xb = xbuf[...].astype(bf16)
        gate = jnp.minimum(_dot(xb, wg_ref[0]) + bg_ref[0], SWIGLU_LIMIT)
        up = jnp.clip(_dot(xb, wu_ref[0]) + bu_ref[0], -SWIGLU_LIMIT, SWIGLU_LIMIT)
        h = gate * jax.nn.sigmoid(SWIGLU_ALPHA * gate) * (up + 1.0)
        ybuf[...] = (_dot(h.astype(bf16), wd_ref[0]) + bd_ref[0]) * gate_ref[...]

        def start_out(r, carry):
            slot = slot_ref[0, 0, r]

            @pl.when(slot < n_slots)
            def _():
                out_copy(r, slot).start()
            return carry

        def wait_out(r, carry):
            @pl.when(slot_ref[0, 0, r] < n_slots)
            def _():
                out_copy(r, 0).wait()
            return carry

        lax.fori_loop(0, bm, start_out, 0)
        lax.fori_loop(0, bm, wait_out, 0)


def _moe(u2, logits, wts, *, bm):
    n_tok, d = u2.shape
    d_ff = wts["w_gate"].shape[2]
    n_assign = n_tok * TOP_K
    top_logit, top_idx = lax.top_k(logits[:, :N_EXPERTS], TOP_K)
    top_gate = jax.nn.softmax(top_logit, axis=-1)
    e_flat = top_idx.reshape(-1)
    order = jnp.argsort(e_flat)
    e_sorted = e_flat[order]
    counts = jnp.zeros((N_EXPERTS,), jnp.int32).at[e_flat].add(1)
    starts = jnp.cumsum(counts) - counts
    padded = (counts + bm - 1) // bm * bm
    pad_ends = jnp.cumsum(padded)
    pad_starts = pad_ends - padded
    dest = pad_starts[e_sorted] + jnp.arange(n_assign, dtype=jnp.int32) - starts[e_sorted]
    n_blocks = -(-n_assign // bm) + N_EXPERTS
    n_rows = n_blocks * bm
    order = order.astype(jnp.int32)
    row_tok = jnp.zeros((n_rows,), jnp.int32).at[dest].set(order // TOP_K)
    row_slot = jnp.full((n_rows,), n_assign, jnp.int32).at[dest].set(order)
    row_gate = jnp.zeros((n_rows,), f32).at[dest].set(top_gate.reshape(-1)[order])
    block_expert = jnp.minimum(
        jnp.searchsorted(pad_ends, jnp.arange(n_blocks, dtype=jnp.int32) * bm, side="right"),
        N_EXPERTS - 1).astype(jnp.int32)
    n_valid = (pad_ends[-1] // bm).astype(jnp.int32).reshape(1)

    def wspec(shape):
        return pl.BlockSpec((1,) + shape, lambda i, be, nv: (be[i], 0, 0))

    idx_spec = pl.BlockSpec((1, 1, bm), lambda i, be, nv: (i, 0, 0), memory_space=pltpu.SMEM)
    grid_spec = pltpu.PrefetchScalarGridSpec(
        num_scalar_prefetch=2,
        grid=(n_blocks,),
        in_specs=[idx_spec, idx_spec,
                  pl.BlockSpec((bm, 1), lambda i, be, nv: (i, 0)),
                  pl.BlockSpec(memory_space=pl.ANY),
                  wspec((d, d_ff)), wspec((1, d_ff)), wspec((d, d_ff)), wspec((1, d_ff)),
                  wspec((d_ff, d)), wspec((1, d))],
        out_specs=pl.BlockSpec(memory_space=pl.ANY),
        scratch_shapes=[pltpu.VMEM((bm, d), f32), pltpu.VMEM((bm, d), f32),
                        pltpu.SemaphoreType.DMA(()), pltpu.SemaphoreType.DMA(())],
    )
    y = pl.pallas_call(
        functools.partial(_moe_body, bm=bm, n_slots=n_assign),
        grid_spec=grid_spec,
        out_shape=jax.ShapeDtypeStruct((n_assign, d), f32),
        compiler_params=_params("arbitrary"),
        name=f"moe_bm{bm}",
    )(block_expert, n_valid, row_tok.reshape(n_blocks, 1, bm), row_slot.reshape(n_blocks, 1, bm),
      row_gate.reshape(n_rows, 1), u2,
      wts["w_gate"], wts["b_gate"], wts["w_up"], wts["b_up"], wts["w_down"], wts["b_down"])
    return y.reshape(n_tok, TOP_K * d)


def _final_body(x1_ref, yk_ref, gt_ref, g_ref, b_ref, o_ref):
    d = x1_ref.shape[1]
    y = yk_ref[:, 0:d]
    for k in range(1, TOP_K):
        y = y + yk_ref[:, k * d:(k + 1) * d]
    o_ref[...] = _layer_norm(DN_ALPHA * x1_ref[...] + gt_ref[0] * y, g_ref[...], b_ref[...])


def _final(x1, yk, gt, g, b, *, tm, tiles_per_mod, name):
    n, d = x1.shape
    row = pl.BlockSpec((tm, d), lambda i: (i, 0))
    vec = pl.BlockSpec((1, d), lambda i: (0, 0))
    return pl.pallas_call(
        _final_body,
        grid=(n // tm,),
        in_specs=[row, pl.BlockSpec((tm, TOP_K * d), lambda i: (i, 0)), _mod_spec(gt, tm, tiles_per_mod), vec, vec],
        out_specs=row,
        out_shape=jax.ShapeDtypeStruct((n, d), f32),
        compiler_params=_params("parallel"),
        name=name,
    )(x1, yk, gt, g, b)


def _stream(x, c, wts, ssd_prm, *, caches=None, conv_state=None, ssm_state=None):
    bsz, seq, d = x.shape
    n = bsz * seq
    is_prompt = caches is None
    d_inner = wts["w_z"].shape[1]
    d_xbc = wts["w_xbc"].shape[1]

    c_pad = jnp.pad(c, ((0, (-bsz) % SUBLANES), (0, 0)))
    mod = _mm(c_pad, wts["w_ada"], wts["b_ada"], tm=c_pad.shape[0], tn=2 * d, act="silu", name="ada_mod")[:bsz]
    sh1, sc1, gt1, sh2, sc2, gt2 = jnp.split(mod, 6, axis=-1)

    def per_tile(p, tile):
        if seq % tile == 0:
            return p.reshape(bsz, 1, d), seq // tile
        return jnp.broadcast_to(p[:, None, :], (bsz, seq, d)).reshape(n // tile, tile, d), 1

    tm = 512
    tpm = per_tile(sc1, tm)[1]
    xn, u = _ln_mod(x.reshape(n, d), wts["ln_emb_g"], wts["ln_emb_b"], per_tile(sc1, tm)[0], per_tile(sh1, tm)[0],
                    tm=tm, tiles_per_mod=tpm, name="ln_mod")
    tmm = 1024
    qkv = _mm(u, wts["w_qkv"], wts["zero_b"][:, :QKV_W], tm=tmm, tn=512, name="proj_qkv")
    z = _mm(u, wts["w_z"], wts["zero_b"][:, :d_inner], tm=tmm, tn=512, name="proj_z")
    xbc = _mm(u, wts["w_xbc"], wts["zero_b"][:, :d_xbc], tm=tmm, tn=512, name="proj_xbc")
    dt = _mm(u, wts["w_dt"], wts["zero_b"][:, :LANES], tm=tmm, tn=LANES, name="proj_dt")
    gates = _mm(u, wts["w_gates"], wts["zero_b"][:, :2 * d], tm=tmm, tn=512, name="proj_gates")

    qkv5 = qkv.reshape(bsz, seq, N_GROUPS, 3, N_HEADS, HEAD_DIM)
    xbc3 = xbc.reshape(bsz, seq, d_xbc)
    if is_prompt:
        outs, lses = zip(*[_attn_prompt_group(qkv, g, bsz, seq) for g in range(N_GROUPS)])
        a = _attn_combine(outs, lses, tm=tm)
        s, h_last = _ssd_prompt(xbc, z, dt, ssd_prm, bsz, seq, chunk=128)
        kv_rows = [jnp.stack([qkv5[:, -min(w, seq):, g, 1], qkv5[:, -min(w, seq):, g, 2]], axis=2)
                   for g, (w, _) in enumerate(ATTN_GROUPS)]
        new_conv = xbc3[:, -(CONV_W - 1):]
        moe_bm = 256
    else:
        a = _attn_sample(qkv, caches, bsz, seq)
        cs8 = jnp.pad(conv_state, ((0, 0), (SUBLANES - (CONV_W - 1), 0), (0, 0)))
        s, h_last = _ssd_sample(xbc, cs8, z, dt, ssm_state.reshape(bsz, d_inner, D_STATE), ssd_prm, bsz, seq)
        kv_rows = [jnp.stack([qkv5[:, :, g, 1], qkv5[:, :, g, 2]], axis=2) for g in range(N_GROUPS)]
        new_conv = jnp.concatenate([conv_state, xbc3], axis=1)[:, -(CONV_W - 1):]
        moe_bm = 128
    new_ssm = h_last.reshape(bsz, d_inner // SSM_HEAD_DIM, SSM_HEAD_DIM, D_STATE)

    tmx = 256
    x1, u2, logits = _mixer(a, s, gates, xn, per_tile(gt1, tmx)[0], per_tile(sc2, tmx)[0], per_tile(sh2, tmx)[0],
                            wts, tm=tmx, tiles_per_mod=per_tile(gt1, tmx)[1], name="mixer")
    yk = _moe(u2, logits, wts, bm=moe_bm)
    y = _final(x1, yk, per_tile(gt2, tm)[0], wts["ln_ffn_g"], wts["ln_ffn_b"], tm=tm, tiles_per_mod=tpm,
               name="final_ln")
    return y.reshape(bsz, seq, d), kv_rows, new_conv, new_ssm


def kernel(x_prompt, x_sample, c_prompt, c_sample, cache_kv_w128, cache_kv_w512, cache_kv_w2048, state_conv, state_ssm, ln_emb_g, ln_emb_b, w_ada, b_ada, w_in, conv_w, conv_b, dt_bias, a_log, d_skip, ssm_norm_g, w_attn_br, w_ssd_br, w_out, ln_mix_g, ln_mix_b, w_router, b_router, w_gate, b_gate, w_up, b_up, w_down, b_down, ln_ffn_g, ln_ffn_b):
    d = x_prompt.shape[-1]
    d_inner = ssm_norm_g.shape[-1]
    d_xbc = conv_w.shape[-1]
    n_heads = dt_bias.shape[-1]
    lyr = 0

    def rowv(v):
        return v.reshape(1, -1).astype(f32)

    def lane_pad(v):
        return jnp.pad(v.astype(f32), (0, LANES - n_heads)).reshape(1, LANES)

    wi = w_in[lyr]
    o0 = QKV_W
    o1 = o0 + d_inner
    o2 = o1 + d_xbc
    o3 = o2 + n_heads
    wr = jnp.pad(w_router[lyr], ((0, 0), (0, LANES - N_EXPERTS)))
    wr_hi = wr.astype(bf16)
    wts = {
        "ln_emb_g": rowv(ln_emb_g), "ln_emb_b": rowv(ln_emb_b),
        "w_ada": w_ada[lyr].astype(bf16), "b_ada": rowv(b_ada[lyr]),
        "w_qkv": wi[:, :o0].astype(bf16), "w_z": wi[:, o0:o1].astype(bf16), "w_xbc": wi[:, o1:o2].astype(bf16),
        "w_dt": jnp.pad(wi[:, o2:o3], ((0, 0), (0, LANES - n_heads))).astype(bf16),
        "w_gates": wi[:, o3:].astype(bf16),
        "zero_b": jnp.zeros((1, max(QKV_W, d_xbc, 2 * d)), f32),
        "w_attn_br": w_attn_br[lyr].astype(bf16), "w_ssd_br": w_ssd_br[lyr].astype(bf16),
        "w_out": w_out[lyr].astype(bf16),
        "ln_mix_g": rowv(ln_mix_g[lyr]), "ln_mix_b": rowv(ln_mix_b[lyr]),
        "w_router_hi": wr_hi, "w_router_lo": (wr - wr_hi.astype(f32)).astype(bf16),
        "b_router": jnp.pad(b_router[lyr], (0, LANES - N_EXPERTS)).reshape(1, LANES),
        "w_gate": w_gate[lyr].astype(bf16), "b_gate": b_gate[lyr][:, None, :],
        "w_up": w_up[lyr].astype(bf16), "b_up": b_up[lyr][:, None, :],
        "w_down": w_down[lyr].astype(bf16), "b_down": b_down[lyr][:, None, :],
        "ln_ffn_g": rowv(ln_ffn_g[lyr]), "ln_ffn_b": rowv(ln_ffn_b[lyr]),
    }
    head_of_lane = jnp.arange(d_inner, dtype=jnp.int32) // SSM_HEAD_DIM
    ssd_prm = {
        "conv_w": conv_w[lyr], "conv_b": rowv(conv_b[lyr]),
        "dtb_row": lane_pad(dt_bias[lyr]), "alog_row": lane_pad(a_log[lyr]),
        "dtb_t": jnp.broadcast_to(dt_bias[lyr][:, None], (n_heads, LANES)),
        "alog_t": jnp.broadcast_to(a_log[lyr][:, None], (n_heads, LANES)),
        "dskip_x": jnp.repeat(d_skip[lyr], SSM_HEAD_DIM).reshape(1, d_inner),
        "norm_g": rowv(ssm_norm_g[lyr]),
        "ex": (jnp.arange(LANES, dtype=jnp.int32)[:, None] == head_of_lane[None, :]).astype(bf16),
        "rsel": (head_of_lane[:, None] == jnp.arange(LANES, dtype=jnp.int32)[None, :]).astype(bf16),
    }

    yp, kv_p, conv_p, ssm_p = _stream(x_prompt, c_prompt, wts, ssd_prm)
    ys, kv_s, conv_s, ssm_s = _stream(
        x_sample, c_sample, wts, ssd_prm,
        caches=(cache_kv_w128[lyr], cache_kv_w512[lyr], cache_kv_w2048[lyr]),
        conv_state=state_conv[lyr], ssm_state=state_ssm[lyr])
    return (yp, ys, kv_p[0][None], kv_s[0][None], kv_p[1][None], kv_s[1][None], kv_p[2][None], kv_s[2][None],
            conv_p[None], conv_s[None], ssm_p[None], ssm_s[None])
```

```python
import functools
import math

import jax
import jax.numpy as jnp
from jax import lax
from jax.experimental import pallas as pl
from jax.experimental.pallas import tpu as pltpu

f32 = jnp.float32
bf16 = jnp.bfloat16

ATTN_GROUPS = ((128, 1), (512, 4), (2048, 16))
N_GROUPS = len(ATTN_GROUPS)
N_HEADS = 8
HEAD_DIM = 64
GROUP_W = N_HEADS * HEAD_DIM
QKV_W = N_GROUPS * 3 * GROUP_W
BAND = 128
ATTN_SCALE = HEAD_DIM ** -0.5
SSM_HEAD_DIM = 64
SSM_GROUPS = 4
D_STATE = 128
CONV_W = 4
N_EXPERTS = 32
TOP_K = 4
SWIGLU_ALPHA = 1.702
SWIGLU_LIMIT = 7.0
LN_EPS = 1e-5
RMS_EPS = 1e-5
DEPTH = 1
DN_ALPHA = (2 * DEPTH) ** 0.25
NEG = -1e30

LANES = 128
SUBLANES = 8
VMEM_LIMIT = 56 * 1024 * 1024


def _params(*sem):
    return pltpu.CompilerParams(dimension_semantics=sem, vmem_limit_bytes=VMEM_LIMIT)


def _silu(x):
    return x * jax.nn.sigmoid(x)


def _softplus(x):
    return jnp.maximum(x, 0.0) + jnp.log(1.0 + jnp.exp(-jnp.abs(x)))


def _split3(x):
    hi = x.astype(bf16)
    r1 = x - hi.astype(f32)
    mid = r1.astype(bf16)
    lo = (r1 - mid.astype(f32)).astype(bf16)
    return hi, mid, lo


def _dot(a, b):
    return jnp.dot(a, b, preferred_element_type=f32)


def _dot_nt(a, b):
    return lax.dot_general(a, b, (((1,), (1,)), ((), ())), preferred_element_type=f32)


def _dot_tn(a, b):
    return lax.dot_general(a, b, (((0,), (0,)), ((), ())), preferred_element_type=f32)


def _dot3_lhs(x, w):
    hi, mid, lo = _split3(x)
    return _dot(hi, w) + _dot(mid, w) + _dot(lo, w)


def _dot3_rhs(w, x):
    hi, mid, lo = _split3(x)
    return _dot(w, hi) + _dot(w, mid) + _dot(w, lo)


def _layer_norm(x, g, b):
    mu = jnp.mean(x, axis=-1, keepdims=True)
    xc = x - mu
    var = jnp.mean(xc * xc, axis=-1, keepdims=True)
    return xc * lax.rsqrt(var + LN_EPS) * g + b


def _mm_body(a_ref, w_ref, b_ref, o_ref, *, act):
    a = a_ref[...]
    if act == "silu":
        a = _silu(a.astype(f32))
    o_ref[...] = (_dot(a.astype(bf16), w_ref[...]) + b_ref[...]).astype(o_ref.dtype)


def _mm(a, w, bias, *, tm, tn, act=None, out_dtype=f32, name):
    m, k = a.shape
    n = w.shape[1]
    return pl.pallas_call(
        functools.partial(_mm_body, act=act),
        grid=(m // tm, n // tn),
        in_specs=[pl.BlockSpec((tm, k), lambda i, j: (i, 0)),
                  pl.BlockSpec((k, tn), lambda i, j: (0, j)),
                  pl.BlockSpec((1, tn), lambda i, j: (0, j))],
        out_specs=pl.BlockSpec((tm, tn), lambda i, j: (i, j)),
        out_shape=jax.ShapeDtypeStruct((m, n), out_dtype),
        compiler_params=_params("parallel", "arbitrary"),
        name=name,
    )(a, w, bias)


def _ln_mod_body(x_ref, g_ref, b_ref, sc_ref, sh_ref, xn_ref, u_ref):
    xn = _layer_norm(x_ref[...], g_ref[...], b_ref[...])
    xn_ref[...] = xn
    u_ref[...] = (xn * (1.0 + sc_ref[0]) + sh_ref[0]).astype(bf16)


def _mod_spec(mod3, tm, tiles_per_mod):
    return pl.BlockSpec((1, mod3.shape[1], mod3.shape[2]), lambda i: (i // tiles_per_mod, 0, 0))


def _ln_mod(x, g, b, sc, sh, *, tm, tiles_per_mod, name):
    n, d = x.shape
    row = pl.BlockSpec((tm, d), lambda i: (i, 0))
    vec = pl.BlockSpec((1, d), lambda i: (0, 0))
    return pl.pallas_call(
        _ln_mod_body,
        grid=(n // tm,),
        in_specs=[row, vec, vec, _mod_spec(sc, tm, tiles_per_mod), _mod_spec(sh, tm, tiles_per_mod)],
        out_specs=(row, row),
        out_shape=(jax.ShapeDtypeStruct((n, d), f32), jax.ShapeDtypeStruct((n, d), bf16)),
        compiler_params=_params("parallel"),
        name=name,
    )(x, g, b, sc, sh)


def _attn_prompt_body(q_ref, kc_ref, kp_ref, vc_ref, vp_ref, o_ref, l_ref):
    n = pl.program_id(2)
    q = (q_ref[0] * ATTN_SCALE).astype(bf16)
    k = jnp.concatenate([kp_ref[0], kc_ref[0]], axis=0).astype(bf16)
    v = jnp.concatenate([vp_ref[0], vc_ref[0]], axis=0).astype(bf16)
    qi = lax.broadcasted_iota(jnp.int32, (BAND, 2 * BAND), 0)
    kj = lax.broadcasted_iota(jnp.int32, (BAND, 2 * BAND), 1)
    dist = qi + BAND - kj
    mask = (dist >= 0) & (dist <= BAND) & ((kj >= BAND) | (n > 0))
    for h in range(N_HEADS):
        sl = slice(h * HEAD_DIM, (h + 1) * HEAD_DIM)
        s = jnp.where(mask, _dot_nt(q[:, sl], k[:, sl]), NEG)
        m = jnp.max(s, axis=-1, keepdims=True)
        p = jnp.exp(s - m)
        den = jnp.sum(p, axis=-1, keepdims=True)
        o_ref[0, :, sl] = _dot(p.astype(bf16), v[:, sl]) / den
        l_ref[0, :, sl] = jnp.broadcast_to(m + jnp.log(den), (BAND, HEAD_DIM))


def _attn_prompt_group(qkv, g, bsz, seq):
    _, dil = ATTN_GROUPS[g]
    n_sub = seq // dil
    nb = n_sub // BAND
    qkv_r = qkv.reshape(bsz, n_sub, dil * QKV_W)
    cpb = QKV_W // GROUP_W
    base = g * 3

    def spec(which, prev):
        if prev:
            return pl.BlockSpec((1, BAND, GROUP_W),
                                lambda b, r, n: (b, jnp.maximum(n - 1, 0), r * cpb + base + which))
        return pl.BlockSpec((1, BAND, GROUP_W), lambda b, r, n: (b, n, r * cpb + base + which))

    out_spec = pl.BlockSpec((1, BAND, GROUP_W), lambda b, r, n: (b, n, r))
    shape = jax.ShapeDtypeStruct((bsz, n_sub, dil * GROUP_W), f32)
    o, l = pl.pallas_call(
        _attn_prompt_body,
        grid=(bsz, dil, nb),
        in_specs=[spec(0, False), spec(1, False), spec(1, True), spec(2, False), spec(2, True)],
        out_specs=(out_spec, out_spec),
        out_shape=(shape, shape),
        compiler_params=_params("parallel", "parallel", "arbitrary"),
        name=f"attn_prompt_g{g}",
    )(qkv_r, qkv_r, qkv_r, qkv_r, qkv_r)
    return o.reshape(bsz * seq, GROUP_W), l.reshape(bsz * seq, GROUP_W)


def _attn_combine_body(o0, o1, o2, l0, l1, l2, a_ref):
    ls = [l0[...], l1[...], l2[...]]
    m = jnp.maximum(jnp.maximum(ls[0], ls[1]), ls[2])
    ws = [jnp.exp(l - m) for l in ls]
    num = ws[0] * o0[...] + ws[1] * o1[...] + ws[2] * o2[...]
    a_ref[...] = num / (ws[0] + ws[1] + ws[2])


def _attn_combine(outs, lses, *, tm):
    n, w = outs[0].shape
    row = pl.BlockSpec((tm, w), lambda i: (i, 0))
    return pl.pallas_call(
        _attn_combine_body,
        grid=(n // tm,),
        in_specs=[row] * 6,
        out_specs=row,
        out_shape=jax.ShapeDtypeStruct((n, w), f32),
        compiler_params=_params("parallel"),
        name="attn_combine",
    )(*outs, *lses)


def _attn_sample_body(qkv_ref, c0_ref, c1_ref, c2_ref, a_ref, *, n_new):
    caches = (c0_ref, c1_ref, c2_ref)
    nh = N_HEADS
    nr = n_new * nh
    nk = BAND * nh
    qkv = qkv_ref[...]

    def heads_to_rows(x):
        return jnp.concatenate([x[:, h * HEAD_DIM:(h + 1) * HEAD_DIM] for h in range(nh)], axis=0)

    pr = lax.broadcasted_iota(jnp.int32, (nr, nr), 0)
    pc = lax.broadcasted_iota(jnp.int32, (nr, nr), 1)
    to_sh = ((pr // nh == pc % n_new) & (pr % nh == pc // n_new)).astype(bf16)
    to_hs = ((pc // nh == pr % n_new) & (pc % nh == pr // n_new)).astype(bf16)
    kh = lax.broadcasted_iota(jnp.int32, (nh, nk), 0)
    kc = lax.broadcasted_iota(jnp.int32, (nh, nk), 1)
    same_head = kh == (kc % nh)
    key_i = kc // nh
    qs, qh = pr // nh, pr % nh
    ks, kh2 = pc % n_new, pc // n_new

    scores, new_scores, new_vals = [], [], []
    for g, (_, dil) in enumerate(ATTN_GROUPS):
        base = g * 3 * GROUP_W
        q_hs = heads_to_rows(qkv[:, base:base + GROUP_W] * ATTN_SCALE).astype(bf16)
        q_sh = _dot(to_sh, q_hs).astype(bf16)
        kn = heads_to_rows(qkv[:, base + GROUP_W:base + 2 * GROUP_W]).astype(bf16)
        new_vals.append(heads_to_rows(qkv[:, base + 2 * GROUP_W:base + 3 * GROUP_W]).astype(bf16))
        valid_new = (qh == kh2) & (ks <= qs) & (((qs - ks) & (dil - 1)) == 0)
        new_scores.append(jnp.where(valid_new, _dot_nt(q_sh, kn), NEG))
        shift = dil.bit_length() - 1
        slabs = [None] * n_new
        for r in range(min(dil, n_new)):
            steps = [s for s in range(n_new) if s % dil == r]
            q_rows = jnp.concatenate([q_sh[s * nh:(s + 1) * nh] for s in steps], axis=0)
            k2 = caches[g][0, :, r, 0].reshape(nk, HEAD_DIM).astype(bf16)
            sc = _dot_nt(q_rows, k2)
            for j, s in enumerate(steps):
                slabs[s] = jnp.where(same_head & (key_i >= (s >> shift)), sc[j * nh:(j + 1) * nh], NEG)
        scores.append(jnp.concatenate(slabs, axis=0))

    m = scores[0].max(axis=-1, keepdims=True)
    for sc in scores[1:] + new_scores:
        m = jnp.maximum(m, sc.max(axis=-1, keepdims=True))
    den = jnp.zeros((nr, 1), f32)
    o = jnp.zeros((nr, HEAD_DIM), f32)
    for g, (_, dil) in enumerate(ATTN_GROUPS):
        pn = jnp.exp(new_scores[g] - m)
        p = jnp.exp(scores[g] - m)
        den = den + jnp.sum(pn, axis=-1, keepdims=True) + jnp.sum(p, axis=-1, keepdims=True)
        o = o + _dot(pn.astype(bf16), new_vals[g])
        p = p.astype(bf16)
        parts = [None] * n_new
        for r in range(min(dil, n_new)):
            steps = [s for s in range(n_new) if s % dil == r]
            p_rows = jnp.concatenate([p[s * nh:(s + 1) * nh] for s in steps], axis=0)
            v2 = caches[g][0, :, r, 1].reshape(nk, HEAD_DIM).astype(bf16)
            pv = _dot(p_rows, v2)
            for j, s in enumerate(steps):
                parts[s] = pv[j * nh:(j + 1) * nh]
        o = o + jnp.concatenate(parts, axis=0)
    o = o / den
    o_hi = o.astype(bf16)
    o_lo = (o - o_hi.astype(f32)).astype(bf16)
    o_hs = _dot(to_hs, o_hi) + _dot(to_hs, o_lo)
    for h in range(nh):
        a_ref[:, h * HEAD_DIM:(h + 1) * HEAD_DIM] = o_hs[h * n_new:(h + 1) * n_new]


def _attn_sample(qkv, caches, bsz, n_new):
    views, specs = [], []
    for g, (window, dil) in enumerate(ATTN_GROUPS):
        assert caches[g].shape[1] == window == BAND * dil and dil & (dil - 1) == 0
        views.append(caches[g].reshape(bsz, BAND, dil, 2, N_HEADS, HEAD_DIM))
        n_res = min(dil, n_new)
        specs.append(pl.BlockSpec((1, BAND, n_res, 2, N_HEADS, HEAD_DIM), lambda b: (b, 0, 0, 0, 0, 0)))
    return pl.pallas_call(
        functools.partial(_attn_sample_body, n_new=n_new),
        grid=(bsz,),
        in_specs=[pl.BlockSpec((n_new, QKV_W), lambda b: (b, 0))] + specs,
        out_specs=pl.BlockSpec((n_new, GROUP_W), lambda b: (b, 0)),
        out_shape=jax.ShapeDtypeStruct((bsz * n_new, GROUP_W), f32),
        compiler_params=_params("parallel"),
        name="attn_sample",
    )(qkv, *views)


def _ssd_conv(ext_ref, n, cw_ref, cb_ref):
    y = cb_ref[...]
    for j in range(CONV_W):
        off = SUBLANES - (CONV_W - 1) + j
        y = y + ext_ref[off:off + n, :] * cw_ref[j:j + 1, :]
    return _silu(y)


def _gate_norm(y, z, g, d_inner):
    y = y * _silu(z)
    gw = d_inner // SSM_GROUPS
    parts = []
    for gi in range(SSM_GROUPS):
        yg = y[:, gi * gw:(gi + 1) * gw]
        parts.append(yg * lax.rsqrt(jnp.mean(yg * yg, axis=-1, keepdims=True) + RMS_EPS))
    return jnp.concatenate(parts, axis=-1) * g


def _ssd_prompt_body(xbc_ref, z_ref, dt_ref, cw_ref, cb_ref, dtb_ref, alog_ref, dtbt_ref, alogt_ref,
                     dskip_ref, ng_ref, ex_ref, s_ref, hl_ref, ext_ref, ht_ref, y_ref, *, d_inner, n_heads):
    c = pl.program_id(1)
    q = xbc_ref.shape[0]
    gw = d_inner // SSM_GROUPS
    hpg = n_heads // SSM_GROUPS

    @pl.when(c == 0)
    def _():
        ext_ref[0:SUBLANES, :] = jnp.zeros((SUBLANES, ext_ref.shape[1]), f32)
        ht_ref[...] = jnp.zeros_like(ht_ref)

    @pl.when(c > 0)
    def _():
        ext_ref[0:SUBLANES, :] = ext_ref[q:q + SUBLANES, :]

    ext_ref[SUBLANES:SUBLANES + q, :] = xbc_ref[...]
    xc = _ssd_conv(ext_ref, q, cw_ref, cb_ref)
    xs = xc[:, :d_inner]
    gn = SSM_GROUPS * D_STATE
    bm = xc[:, d_inner:d_inner + gn].astype(bf16)
    cm = xc[:, d_inner + gn:].astype(bf16)

    dtr = dt_ref[...]
    dt = _softplus(dtr + dtb_ref[...])
    da = dt * (-jnp.exp(alog_ref[...]))
    dt_t = _softplus(dtr.T[0:n_heads, :] + dtbt_ref[...])
    da_t = dt_t * (-jnp.exp(alogt_ref[...]))
    ii = lax.broadcasted_iota(jnp.int32, (q, q), 0)
    jj = lax.broadcasted_iota(jnp.int32, (q, q), 1)
    causal = ii >= jj
    lower = causal.astype(bf16)
    upper = (ii <= jj).astype(bf16)
    cum = _dot3_rhs(lower, da)
    cum_t = _dot3_lhs(da_t, upper)
    ex = ex_ref[...]
    cumx = _dot3_lhs(cum, ex)
    dtx = _dot3_lhs(dt, ex)
    clx = cumx[q - 1:q, :]
    ecum = jnp.exp(cumx)
    xd = (jnp.exp(clx - cumx) * dtx * xs).astype(bf16)
    xsb = xs.astype(bf16)

    for g in range(SSM_GROUPS):
        gl = slice(g * gw, (g + 1) * gw)
        cg = cm[:, g * D_STATE:(g + 1) * D_STATE]
        bg = bm[:, g * D_STATE:(g + 1) * D_STATE]
        cb = _dot_nt(cg, bg)
        h_old = ht_ref[g]
        y_ref[:, gl] = ecum[:, gl] * _dot(cg, h_old.astype(bf16))
        ht_ref[g] = jnp.exp(clx[:, gl]) * h_old + _dot_tn(bg, xd[:, gl])
        for e in range(hpg):
            hd = g * hpg + e
            hl = slice(hd * SSM_HEAD_DIM, (hd + 1) * SSM_HEAD_DIM)
            seg = jnp.broadcast_to(cum[:, hd:hd + 1], (q, q)) - jnp.broadcast_to(cum_t[hd:hd + 1, :], (q, q))
            w = cb * jnp.exp(jnp.where(causal, seg, NEG)) * jnp.broadcast_to(dt_t[hd:hd + 1, :], (q, q))
            y_ref[:, hl] += _dot(w.astype(bf16), xsb[:, hl])

    y = y_ref[...] + dskip_ref[...] * xs
    s_ref[...] = _gate_norm(y, z_ref[...], ng_ref[...], d_inner).astype(s_ref.dtype)

    @pl.when(c == pl.num_programs(1) - 1)
    def _():
        for g in range(SSM_GROUPS):
            hl_ref[0, g * gw:(g + 1) * gw, :] = ht_ref[g].T


def _ssd_prompt(xbc, z, dt, prm, bsz, seq, *, chunk):
    n, d_xbc = xbc.shape
    d_inner = z.shape[1]
    n_heads = d_inner // SSM_HEAD_DIM
    nc = seq // chunk
    gw = d_inner // SSM_GROUPS

    def row(w):
        return pl.BlockSpec((chunk, w), lambda b, c: (b * nc + c, 0))

    def full(a):
        return pl.BlockSpec(a.shape, lambda b, c: (0,) * a.ndim)

    consts = [prm["conv_w"], prm["conv_b"], prm["dtb_row"], prm["alog_row"], prm["dtb_t"], prm["alog_t"],
              prm["dskip_x"], prm["norm_g"], prm["ex"]]
    return pl.pallas_call(
        functools.partial(_ssd_prompt_body, d_inner=d_inner, n_heads=n_heads),
        grid=(bsz, nc),
        in_specs=[row(d_xbc), row(d_inner), row(LANES)] + [full(a) for a in consts],
        out_specs=(row(d_inner), pl.BlockSpec((1, d_inner, D_STATE), lambda b, c: (b, 0, 0))),
        out_shape=(jax.ShapeDtypeStruct((n, d_inner), bf16),
                   jax.ShapeDtypeStruct((bsz, d_inner, D_STATE), f32)),
        scratch_shapes=[pltpu.VMEM((SUBLANES + chunk + SUBLANES, d_xbc), f32),
                        pltpu.VMEM((SSM_GROUPS, D_STATE, gw), f32),
                        pltpu.VMEM((chunk, d_inner), f32)],
        compiler_params=_params("parallel", "arbitrary"),
        name="ssd_prompt",
    )(xbc, z, dt, *consts)


def _ssd_sample_body(xbc_ref, cs_ref, z_ref, dt_ref, h_ref, cw_ref, cb_ref, dtb_ref, alog_ref,
                     dskip_ref, ng_ref, ex_ref, rsel_ref, s_ref, hn_ref, ext_ref, *, d_inner, n_heads):
    q = xbc_ref.shape[0]
    gw = d_inner // SSM_GROUPS
    ext_ref[0:SUBLANES, :] = cs_ref[0]
    ext_ref[SUBLANES:SUBLANES + q, :] = xbc_ref[...]
    xc = _ssd_conv(ext_ref, q, cw_ref, cb_ref)
    xs = xc[:, :d_inner]
    gn = SSM_GROUPS * D_STATE
    bm = xc[:, d_inner:d_inner + gn]
    cm = xc[:, d_inner + gn:]

    dt = _softplus(dt_ref[...] + dtb_ref[...])
    da = dt * (-jnp.exp(alog_ref[...]))
    row = lax.broadcasted_iota(jnp.int32, (q, LANES), 0)
    cum = jnp.zeros((q, LANES), f32)
    for j in range(q):
        cum = cum + jnp.where(row >= j, da[j:j + 1, :], 0.0)
    ex = ex_ref[...]
    cumx = _dot3_lhs(cum, ex)
    dtx = _dot3_lhs(dt, ex)
    clx = cumx[q - 1:q, :]
    rowx = lax.broadcasted_iota(jnp.int32, (q, d_inner), 0)

    y = dskip_ref[...] * xs
    for j in range(q):
        prod = cm * bm[j:j + 1, :]
        cbx = jnp.concatenate(
            [jnp.broadcast_to(jnp.sum(prod[:, g * D_STATE:(g + 1) * D_STATE], axis=-1, keepdims=True), (q, gw))
             for g in range(SSM_GROUPS)], axis=-1)
        seg = jnp.where(rowx >= j, cumx - cumx[j:j + 1, :], NEG)
        y = y + cbx * jnp.exp(seg) * (dtx[j:j + 1, :] * xs[j:j + 1, :])

    xd = (jnp.exp(clx - cumx) * dtx * xs).astype(bf16)
    ones = jnp.ones((q, LANES), bf16)
    da_hi, da_mid, da_lo = _split3(da)
    cl_b = _dot_tn(da_hi, ones) + _dot_tn(da_mid, ones) + _dot_tn(da_lo, ones)
    decay = jnp.exp(_dot3_rhs(rsel_ref[...], cl_b))
    ecum = jnp.exp(cumx)
    cmb = cm.astype(bf16)
    bmb = bm.astype(bf16)
    ys = []
    for g in range(SSM_GROUPS):
        rows = slice(g * gw, (g + 1) * gw)
        hg = h_ref[0, rows, :]
        ys.append(_dot_nt(cmb[:, g * D_STATE:(g + 1) * D_STATE], hg.astype(bf16)))
        hn_ref[0, rows, :] = decay[rows, :] * hg + _dot_tn(xd[:, rows], bmb[:, g * D_STATE:(g + 1) * D_STATE])
    y = y + ecum * jnp.concatenate(ys, axis=-1)
    s_ref[...] = _gate_norm(y, z_ref[...], ng_ref[...], d_inner)


def _ssd_sample(xbc, conv_state8, z, dt, h0, prm, bsz, n_new):
    n, d_xbc = xbc.shape
    d_inner = z.shape[1]
    n_heads = d_inner // SSM_HEAD_DIM

    def row(w):
        return pl.BlockSpec((n_new, w), lambda b: (b, 0))

    def full(a):
        return pl.BlockSpec(a.shape, lambda b: (0,) * a.ndim)

    consts = [prm["conv_w"], prm["conv_b"], prm["dtb_row"], prm["alog_row"],
              prm["dskip_x"], prm["norm_g"], prm["ex"], prm["rsel"]]
    state = pl.BlockSpec((1, d_inner, D_STATE), lambda b: (b, 0, 0))
    return pl.pallas_call(
        functools.partial(_ssd_sample_body, d_inner=d_inner, n_heads=n_heads),
        grid=(bsz,),
        in_specs=[row(d_xbc), pl.BlockSpec((1, SUBLANES, d_xbc), lambda b: (b, 0, 0)), row(d_inner), row(LANES),
                  state] + [full(a) for a in consts],
        out_specs=(row(d_inner), state),
        out_shape=(jax.ShapeDtypeStruct((n, d_inner), f32),
                   jax.ShapeDtypeStruct((bsz, d_inner, D_STATE), f32)),
        scratch_shapes=[pltpu.VMEM((2 * SUBLANES, d_xbc), f32)],
        compiler_params=_params("parallel"),
        name="ssd_sample",
    )(xbc, conv_state8, z, dt, h0, *consts)


def _mixer_body(a_ref, s_ref, ga_ref, gs_ref, xn_ref, gt_ref, sc_ref, sh_ref, wa_ref, ws_ref, wo_ref,
                lg_ref, lb_ref, wrh_ref, wrl_ref, br_ref, x1_ref, u2_ref, lo_ref):
    m = (jax.nn.sigmoid(ga_ref[...]) * _dot(a_ref[...].astype(bf16), wa_ref[...])
         + jax.nn.sigmoid(gs_ref[...]) * _dot(s_ref[...].astype(bf16), ws_ref[...]))
    o = _dot(m.astype(bf16), wo_ref[...])
    x1 = _layer_norm(DN_ALPHA * xn_ref[...] + gt_ref[0] * o, lg_ref[...], lb_ref[...])
    x1_ref[...] = x1
    u2 = x1 * (1.0 + sc_ref[0]) + sh_ref[0]
    u2_ref[...] = u2
    hi, mid, lo = _split3(u2)
    wrh = wrh_ref[...]
    wrl = wrl_ref[...]
    lo_ref[...] = (_dot(hi, wrh) + (_dot(hi, wrl) + _dot(mid, wrh)) + (_dot(mid, wrl) + _dot(lo, wrh))
                   + br_ref[...])


def _mixer(a, s, gates, xn, gt, sc, sh, wts, *, tm, tiles_per_mod, name):
    n, d = xn.shape

    def row(w, col=0):
        return pl.BlockSpec((tm, w), lambda i: (i, col))

    def full(arr):
        return pl.BlockSpec(arr.shape, lambda i: (0,) * arr.ndim)

    consts = [wts["w_attn_br"], wts["w_ssd_br"], wts["w_out"], wts["ln_mix_g"], wts["ln_mix_b"],
              wts["w_router_hi"], wts["w_router_lo"], wts["b_router"]]
    return pl.pallas_call(
        _mixer_body,
        grid=(n // tm,),
        in_specs=[row(a.shape[1]), row(s.shape[1]), row(d, 0), row(d, 1), row(d),
                  _mod_spec(gt, tm, tiles_per_mod), _mod_spec(sc, tm, tiles_per_mod),
                  _mod_spec(sh, tm, tiles_per_mod)] + [full(c) for c in consts],
        out_specs=(row(d), row(d), row(LANES)),
        out_shape=(jax.ShapeDtypeStruct((n, d), f32), jax.ShapeDtypeStruct((n, d), f32),
                   jax.ShapeDtypeStruct((n, LANES), f32)),
        compiler_params=_params("parallel"),
        name=name,
    )(a, s, gates, gates, xn, gt, sc, sh, *consts)


def _moe_body(be_ref, nv_ref, nr_ref, tok_ref, tok_next_ref, slot_ref, x_hbm, wg_ref, bg_ref, wu_ref, bu_ref, wd_ref,
              bd_ref, y_hbm, xbuf, ybuf, sem_in, sem_out, *, bm):
    i = pl.program_id(0)
    nv = nv_ref[0]
    cur = i % 2
    nxt = 1 - cur

    def start_gather(idx_ref, buf):
        def body(r, carry):
            pltpu.make_async_copy(x_hbm.at[pl.ds(idx_ref[0, 0, r], 1), :], xbuf.at[buf, pl.ds(r, 1), :],
                                  sem_in.at[buf]).start()
            return carry
        lax.fori_loop(0, bm, body, 0, unroll=8)

    def wait_gather(buf):
        pltpu.make_async_copy(x_hbm.at[pl.ds(0, bm), :], xbuf.at[buf], sem_in.at[buf]).wait()

    def scatter_row(buf, r, slot):
        return pltpu.make_async_copy(ybuf.at[buf, pl.ds(r, 1), :], y_hbm.at[pl.ds(slot, 1), :], sem_out.at[buf])

    def start_scatter(buf, n_real):
        def body(r, carry):
            scatter_row(buf, r, slot_ref[0, 0, r]).start()
            return carry

        @pl.when(n_real == bm)
        def _():
            lax.fori_loop(0, bm, body, 0, unroll=8)

        @pl.when(n_real < bm)
        def _():
            lax.fori_loop(0, n_real, body, 0)

    def wait_scatter(buf, n_real):
        @pl.when(n_real == bm)
        def _():
            pltpu.make_async_copy(ybuf.at[buf], y_hbm.at[pl.ds(0, bm), :], sem_out.at[buf]).wait()

        @pl.when(n_real < bm)
        def _():
            def body(r, carry):
                scatter_row(buf, 0, 0).wait()
                return carry
            lax.fori_loop(0, n_real, body, 0)

    @pl.when((i == 0) & (nv > 0))
    def _():
        start_gather(tok_ref, cur)

    @pl.when(i + 1 < nv)
    def _():
        start_gather(tok_next_ref, nxt)

    @pl.when(i < nv)
    def _():
        wait_gather(cur)
        xb = xbuf[cur].astype(bf16)
        gate = jnp.minimum(_dot(xb, wg_ref[0]) + bg_ref[0], SWIGLU_LIMIT)
        up = jnp.clip(_dot(xb, wu_ref[0]) + bu_ref[0], -SWIGLU_LIMIT, SWIGLU_LIMIT)
        h = gate * jax.nn.sigmoid(SWIGLU_ALPHA * gate) * (up + 1.0)
        y = _dot(h.astype(bf16), wd_ref[0]) + bd_ref[0]

        @pl.when(i >= 1)
        def _():
            wait_scatter(nxt, nr_ref[jnp.maximum(i - 1, 0)])

        ybuf[cur] = y
        start_scatter(cur, nr_ref[i])

        @pl.when(i == nv - 1)
        def _():
            wait_scatter(cur, nr_ref[i])


def _moe(u2, logits, wts, *, bm):
    n_tok, d = u2.shape
    d_ff = wts["w_gate"].shape[2]
    n_assign = n_tok * TOP_K
    i32 = jnp.int32
    top_logit, top_idx = lax.top_k(logits[:, :N_EXPERTS], TOP_K)
    top_gate = jax.nn.softmax(top_logit, axis=-1)
    e_flat = top_idx.reshape(-1)
    order = jnp.argsort(e_flat).astype(i32)
    experts = jnp.arange(N_EXPERTS, dtype=i32)
    counts = jnp.sum((e_flat[:, None] == experts[None, :]).astype(i32), axis=0)
    starts = jnp.cumsum(counts) - counts
    padded = (counts + bm - 1) // bm * bm
    pad_ends = jnp.cumsum(padded)
    pad_starts = pad_ends - padded
    n_blocks = -(-n_assign // bm) + N_EXPERTS
    blk_row = jnp.arange(n_blocks, dtype=i32) * bm
    block_expert = jnp.minimum(jnp.sum((pad_ends[None, :] <= blk_row[:, None]).astype(i32), axis=1), N_EXPERTS - 1)
    off = blk_row - pad_starts[block_expert]
    n_real = jnp.clip(counts[block_expert] - off, 0, bm)
    j = jnp.arange(bm, dtype=i32)[None, :]
    src = jnp.clip(starts[block_expert][:, None] + off[:, None] + j, 0, n_assign - 1)
    a_id = jnp.take(order, src.reshape(-1), axis=0).reshape(n_blocks, bm)
    real = j < n_real[:, None]
    row_tok = jnp.where(real, a_id // TOP_K, 0)
    row_slot = jnp.where(real, (a_id % TOP_K) * n_tok + a_id // TOP_K, 0)
    n_valid = (pad_ends[-1] // bm).astype(i32).reshape(1)

    def wspec(shape):
        return pl.BlockSpec((1,) + shape, lambda i, be, nv, nr: (be[i], 0, 0))

    idx_spec = pl.BlockSpec((1, 1, bm), lambda i, be, nv, nr: (i, 0, 0), memory_space=pltpu.SMEM)
    idx_next = pl.BlockSpec((1, 1, bm), lambda i, be, nv, nr: (jnp.minimum(i + 1, n_blocks - 1), 0, 0),
                            memory_space=pltpu.SMEM)
    grid_spec = pltpu.PrefetchScalarGridSpec(
        num_scalar_prefetch=3,
        grid=(n_blocks,),
        in_specs=[idx_spec, idx_next, idx_spec,
                  pl.BlockSpec(memory_space=pl.ANY),
                  wspec((d, d_ff)), wspec((1, d_ff)), wspec((d, d_ff)), wspec((1, d_ff)),
                  wspec((d_ff, d)), wspec((1, d))],
        out_specs=pl.BlockSpec(memory_space=pl.ANY),
        scratch_shapes=[pltpu.VMEM((2, bm, d), f32), pltpu.VMEM((2, bm, d), f32),
                        pltpu.SemaphoreType.DMA((2,)), pltpu.SemaphoreType.DMA((2,))],
    )
    row_tok3 = row_tok.reshape(n_blocks, 1, bm)
    y = pl.pallas_call(
        functools.partial(_moe_body, bm=bm),
        grid_spec=grid_spec,
        out_shape=jax.ShapeDtypeStruct((n_assign, d), f32),
        compiler_params=_params("arbitrary"),
        name=f"moe_bm{bm}",
    )(block_expert, n_valid, n_real, row_tok3, row_tok3, row_slot.reshape(n_blocks, 1, bm), u2,
      wts["w_gate"], wts["b_gate"], wts["w_up"], wts["b_up"], wts["w_down"], wts["b_down"])
    return y, top_gate


def _final_body(x1_ref, *rest):
    yk_refs, (pg_ref, gt_ref, g_ref, b_ref, o_ref) = rest[:TOP_K], rest[TOP_K:]
    pg = pg_ref[...]
    y = pg[:, 0:1] * yk_refs[0][...]
    for k in range(1, TOP_K):
        y = y + pg[:, k:k + 1] * yk_refs[k][...]
    o_ref[...] = _layer_norm(DN_ALPHA * x1_ref[...] + gt_ref[0] * y, g_ref[...], b_ref[...])


def _final(x1, yk, top_gate, gt, g, b, *, tm, tiles_per_mod, name):
    n, d = x1.shape
    nt = n // tm
    row = pl.BlockSpec((tm, d), lambda i: (i, 0))
    vec = pl.BlockSpec((1, d), lambda i: (0, 0))
    planes = [pl.BlockSpec((tm, d), functools.partial(lambda i, k: (k * nt + i, 0), k=k)) for k in range(TOP_K)]
    return pl.pallas_call(
        _final_body,
        grid=(nt,),
        in_specs=[row] + planes + [pl.BlockSpec((tm, TOP_K), lambda i: (i, 0)),
                                   _mod_spec(gt, tm, tiles_per_mod), vec, vec],
        out_specs=row,
        out_shape=jax.ShapeDtypeStruct((n, d), f32),
        compiler_params=_params("parallel"),
        name=name,
    )(x1, *([yk] * TOP_K), top_gate, gt, g, b)


def _stream(x, c, wts, ssd_prm, *, caches=None, conv_state=None, ssm_state=None):
    bsz, seq, d = x.shape
    n = bsz * seq
    is_prompt = caches is None
    d_inner = wts["w_z"].shape[1]
    d_xbc = wts["w_xbc"].shape[1]

    c_pad = jnp.pad(c, ((0, (-bsz) % SUBLANES), (0, 0)))
    mod = _mm(c_pad, wts["w_ada"], wts["b_ada"], tm=c_pad.shape[0], tn=2 * d, act="silu", name="ada_mod")[:bsz]
    sh1, sc1, gt1, sh2, sc2, gt2 = jnp.split(mod, 6, axis=-1)

    def per_tile(p, tile):
        if seq % tile == 0:
            return p.reshape(bsz, 1, d), seq // tile
        return jnp.broadcast_to(p[:, None, :], (bsz, seq, d)).reshape(n // tile, tile, d), 1

    tm = 512
    tpm = per_tile(sc1, tm)[1]
    xn, u = _ln_mod(x.reshape(n, d), wts["ln_emb_g"], wts["ln_emb_b"], per_tile(sc1, tm)[0], per_tile(sh1, tm)[0],
                    tm=tm, tiles_per_mod=tpm, name="ln_mod")
    tmm = 1024
    qkv = _mm(u, wts["w_qkv"], wts["zero_b"][:, :QKV_W], tm=tmm, tn=512, name="proj_qkv")
    z = _mm(u, wts["w_z"], wts["zero_b"][:, :d_inner], tm=tmm, tn=512, name="proj_z")
    xbc = _mm(u, wts["w_xbc"], wts["zero_b"][:, :d_xbc], tm=tmm, tn=512, name="proj_xbc")
    dt = _mm(u, wts["w_dt"], wts["zero_b"][:, :LANES], tm=tmm, tn=LANES, name="proj_dt")
    gates = _mm(u, wts["w_gates"], wts["zero_b"][:, :2 * d], tm=tmm, tn=512, name="proj_gates")

    qkv3 = qkv.reshape(bsz, seq, QKV_W)
    xbc3 = xbc.reshape(bsz, seq, d_xbc)

    def kv_rows_of(g, keep):
        cols = qkv3[:, seq - keep:, (g * 3 + 1) * GROUP_W:(g * 3 + 3) * GROUP_W]
        return cols.reshape(bsz, keep, 2, N_HEADS, HEAD_DIM)

    if is_prompt:
        outs, lses = zip(*[_attn_prompt_group(qkv, g, bsz, seq) for g in range(N_GROUPS)])
        a = _attn_combine(outs, lses, tm=tm)
        s, h_last = _ssd_prompt(xbc, z, dt, ssd_prm, bsz, seq, chunk=128)
        kv_rows = [kv_rows_of(g, min(w, seq)) for g, (w, _) in enumerate(ATTN_GROUPS)]
        new_conv = xbc3[:, -(CONV_W - 1):]
        moe_bm = 256
    else:
        a = _attn_sample(qkv, caches, bsz, seq)
        cs8 = jnp.pad(conv_state, ((0, 0), (SUBLANES - (CONV_W - 1), 0), (0, 0)))
        s, h_last = _ssd_sample(xbc, cs8, z, dt, ssm_state.reshape(bsz, d_inner, D_STATE), ssd_prm, bsz, seq)
        kv_rows = [kv_rows_of(g, seq) for g in range(N_GROUPS)]
        new_conv = jnp.concatenate([conv_state, xbc3], axis=1)[:, -(CONV_W - 1):]
        moe_bm = 128
    new_ssm = h_last.reshape(bsz, d_inner // SSM_HEAD_DIM, SSM_HEAD_DIM, D_STATE)

    tmx = 256
    x1, u2, logits = _mixer(a, s, gates, xn, per_tile(gt1, tmx)[0], per_tile(sc2, tmx)[0], per_tile(sh2, tmx)[0],
                            wts, tm=tmx, tiles_per_mod=per_tile(gt1, tmx)[1], name="mixer")
    yk, top_gate = _moe(u2, logits, wts, bm=moe_bm)
    y = _final(x1, yk, top_gate, per_tile(gt2, tm)[0], wts["ln_ffn_g"], wts["ln_ffn_b"], tm=tm, tiles_per_mod=tpm,
               name="final_ln")
    return y.reshape(bsz, seq, d), kv_rows, new_conv, new_ssm


def kernel(x_prompt, x_sample, c_prompt, c_sample, cache_kv_w128, cache_kv_w512, cache_kv_w2048, state_conv, state_ssm, ln_emb_g, ln_emb_b, w_ada, b_ada, w_in, conv_w, conv_b, dt_bias, a_log, d_skip, ssm_norm_g, w_attn_br, w_ssd_br, w_out, ln_mix_g, ln_mix_b, w_router, b_router, w_gate, b_gate, w_up, b_up, w_down, b_down, ln_ffn_g, ln_ffn_b):
    d = x_prompt.shape[-1]
    d_inner = ssm_norm_g.shape[-1]
    d_xbc = conv_w.shape[-1]
    n_heads = dt_bias.shape[-1]
    lyr = 0

    def rowv(v):
        return v.reshape(1, -1).astype(f32)

    def lane_pad(v):
        return jnp.pad(v.astype(f32), (0, LANES - n_heads)).reshape(1, LANES)

    wi = w_in[lyr]
    o0 = QKV_W
    o1 = o0 + d_inner
    o2 = o1 + d_xbc
    o3 = o2 + n_heads
    wr = jnp.pad(w_router[lyr], ((0, 0), (0, LANES - N_EXPERTS)))
    wr_hi = wr.astype(bf16)
    wts = {
        "ln_emb_g": rowv(ln_emb_g), "ln_emb_b": rowv(ln_emb_b),
        "w_ada": w_ada[lyr].astype(bf16), "b_ada": rowv(b_ada[lyr]),
        "w_qkv": wi[:, :o0].astype(bf16), "w_z": wi[:, o0:o1].astype(bf16), "w_xbc": wi[:, o1:o2].astype(bf16),
        "w_dt": jnp.pad(wi[:, o2:o3], ((0, 0), (0, LANES - n_heads))).astype(bf16),
        "w_gates": wi[:, o3:].astype(bf16),
        "zero_b": jnp.zeros((1, max(QKV_W, d_xbc, 2 * d)), f32),
        "w_attn_br": w_attn_br[lyr].astype(bf16), "w_ssd_br": w_ssd_br[lyr].astype(bf16),
        "w_out": w_out[lyr].astype(bf16),
        "ln_mix_g": rowv(ln_mix_g[lyr]), "ln_mix_b": rowv(ln_mix_b[lyr]),
        "w_router_hi": wr_hi, "w_router_lo": (wr - wr_hi.astype(f32)).astype(bf16),
        "b_router": jnp.pad(b_router[lyr], (0, LANES - N_EXPERTS)).reshape(1, LANES),
        "w_gate": w_gate[lyr].astype(bf16), "b_gate": b_gate[lyr][:, None, :],
        "w_up": w_up[lyr].astype(bf16), "b_up": b_up[lyr][:, None, :],
        "w_down": w_down[lyr].astype(bf16), "b_down": b_down[lyr][:, None, :],
        "ln_ffn_g": rowv(ln_ffn_g[lyr]), "ln_ffn_b": rowv(ln_ffn_b[lyr]),
    }
    head_of_lane = jnp.arange(d_inner, dtype=jnp.int32) // SSM_HEAD_DIM
    ssd_prm = {
        "conv_w": conv_w[lyr], "conv_b": rowv(conv_b[lyr]),
        "dtb_row": lane_pad(dt_bias[lyr]), "alog_row": lane_pad(a_log[lyr]),
        "dtb_t": jnp.broadcast_to(dt_bias[lyr][:, None], (n_heads, LANES)),
        "alog_t": jnp.broadcast_to(a_log[lyr][:, None], (n_heads, LANES)),
        "dskip_x": jnp.repeat(d_skip[lyr], SSM_HEAD_DIM).reshape(1, d_inner),
        "norm_g": rowv(ssm_norm_g[lyr]),
        "ex": (jnp.arange(LANES, dtype=jnp.int32)[:, None] == head_of_lane[None, :]).astype(bf16),
        "rsel": (head_of_lane[:, None] == jnp.arange(LANES, dtype=jnp.int32)[None, :]).astype(bf16),
    }

    yp, kv_p, conv_p, ssm_p = _stream(x_prompt, c_prompt, wts, ssd_prm)
    ys, kv_s, conv_s, ssm_s = _stream(
        x_sample, c_sample, wts, ssd_prm,
        caches=(cache_kv_w128[lyr], cache_kv_w512[lyr], cache_kv_w2048[lyr]),
        conv_state=state_conv[lyr], ssm_state=state_ssm[lyr])
    return (yp, ys, kv_p[0][None], kv_s[0][None], kv_p[1][None], kv_s[1][None], kv_p[2][None], kv_s[2][None],
            conv_p[None], conv_s[None], ssm_p[None], ssm_s[None])
```

```python
import functools
import math

import jax
import jax.numpy as jnp
from jax import lax
from jax.experimental import pallas as pl
from jax.experimental.pallas import tpu as pltpu

f32 = jnp.float32
bf16 = jnp.bfloat16

ATTN_GROUPS = ((128, 1), (512, 4), (2048, 16))
N_GROUPS = len(ATTN_GROUPS)
N_HEADS = 8
HEAD_DIM = 64
GROUP_W = N_HEADS * HEAD_DIM
QKV_W = N_GROUPS * 3 * GROUP_W
BAND = 128
ATTN_SCALE = HEAD_DIM ** -0.5
SSM_HEAD_DIM = 64
SSM_GROUPS = 4
D_STATE = 128
CONV_W = 4
N_EXPERTS = 32
TOP_K = 4
SWIGLU_ALPHA = 1.702
SWIGLU_LIMIT = 7.0
LN_EPS = 1e-5
RMS_EPS = 1e-5
DEPTH = 1
DN_ALPHA = (2 * DEPTH) ** 0.25
NEG = -1e30

LANES = 128
SUBLANES = 8
VMEM_LIMIT = 56 * 1024 * 1024


def _params(*sem):
    return pltpu.CompilerParams(dimension_semantics=sem, vmem_limit_bytes=VMEM_LIMIT)


def _silu(x):
    return x * jax.nn.sigmoid(x)


def _softplus(x):
    return jnp.maximum(x, 0.0) + jnp.log(1.0 + jnp.exp(-jnp.abs(x)))


def _split3(x):
    hi = x.astype(bf16)
    r1 = x - hi.astype(f32)
    mid = r1.astype(bf16)
    lo = (r1 - mid.astype(f32)).astype(bf16)
    return hi, mid, lo


def _dot(a, b):
    return jnp.dot(a, b, preferred_element_type=f32)


def _dot_nt(a, b):
    return lax.dot_general(a, b, (((1,), (1,)), ((), ())), preferred_element_type=f32)


def _dot_tn(a, b):
    return lax.dot_general(a, b, (((0,), (0,)), ((), ())), preferred_element_type=f32)


def _dot3_lhs(x, w):
    hi, mid, lo = _split3(x)
    return _dot(hi, w) + _dot(mid, w) + _dot(lo, w)


def _dot3_rhs(w, x):
    hi, mid, lo = _split3(x)
    return _dot(w, hi) + _dot(w, mid) + _dot(w, lo)


def _layer_norm(x, g, b):
    mu = jnp.mean(x, axis=-1, keepdims=True)
    xc = x - mu
    var = jnp.mean(xc * xc, axis=-1, keepdims=True)
    return xc * lax.rsqrt(var + LN_EPS) * g + b


def _mm_body(a_ref, w_ref, b_ref, o_ref, *, act):
    a = a_ref[...]
    if act == "silu":
        a = _silu(a.astype(f32))
    o_ref[...] = (_dot(a.astype(bf16), w_ref[...]) + b_ref[...]).astype(o_ref.dtype)


def _mm(a, w, bias, *, tm, tn, act=None, out_dtype=f32, name):
    m, k = a.shape
    n = w.shape[1]
    return pl.pallas_call(
        functools.partial(_mm_body, act=act),
        grid=(m // tm, n // tn),
        in_specs=[pl.BlockSpec((tm, k), lambda i, j: (i, 0)),
                  pl.BlockSpec((k, tn), lambda i, j: (0, j)),
                  pl.BlockSpec((1, tn), lambda i, j: (0, j))],
        out_specs=pl.BlockSpec((tm, tn), lambda i, j: (i, j)),
        out_shape=jax.ShapeDtypeStruct((m, n), out_dtype),
        compiler_params=_params("parallel", "arbitrary"),
        name=name,
    )(a, w, bias)


def _ln_mod_body(x_ref, g_ref, b_ref, sc_ref, sh_ref, xn_ref, u_ref):
    xn = _layer_norm(x_ref[...], g_ref[...], b_ref[...])
    xn_ref[...] = xn
    u_ref[...] = (xn * (1.0 + sc_ref[0]) + sh_ref[0]).astype(bf16)


def _mod_spec(mod3, tm, tiles_per_mod):
    return pl.BlockSpec((1, mod3.shape[1], mod3.shape[2]), lambda i: (i // tiles_per_mod, 0, 0))


def _ln_mod(x, g, b, sc, sh, *, tm, tiles_per_mod, name):
    n, d = x.shape
    row = pl.BlockSpec((tm, d), lambda i: (i, 0))
    vec = pl.BlockSpec((1, d), lambda i: (0, 0))
    return pl.pallas_call(
        _ln_mod_body,
        grid=(n // tm,),
        in_specs=[row, vec, vec, _mod_spec(sc, tm, tiles_per_mod), _mod_spec(sh, tm, tiles_per_mod)],
        out_specs=(row, row),
        out_shape=(jax.ShapeDtypeStruct((n, d), f32), jax.ShapeDtypeStruct((n, d), bf16)),
        compiler_params=_params("parallel"),
        name=name,
    )(x, g, b, sc, sh)


ATTN_SPAN = BAND * max(dil for _, dil in ATTN_GROUPS)
ATTN_LANES = 2 * HEAD_DIM


def _attn_prompt_fused_body(*refs):
    q_refs = refs[0:N_GROUPS]
    k_refs = refs[N_GROUPS:2 * N_GROUPS]
    v_refs = refs[2 * N_GROUPS:3 * N_GROUPS]
    a_ref = refs[3 * N_GROUPS]
    o_scr, l_scr = refs[3 * N_GROUPS + 1:]
    span0 = pl.program_id(2) * ATTN_SPAN
    qi = lax.broadcasted_iota(jnp.int32, (BAND, 2 * BAND), 0)
    kj = lax.broadcasted_iota(jnp.int32, (BAND, 2 * BAND), 1)
    dist = qi + BAND - kj
    band = (dist >= 0) & (dist <= BAND)
    heads = ATTN_LANES // HEAD_DIM
    head_of_lane = lax.broadcasted_iota(jnp.int32, (BAND, ATTN_LANES), 1) // HEAD_DIM
    for g, (_, dil) in enumerate(ATTN_GROUPS):
        n_blocks = ATTN_SPAN // (BAND * dil)

        def block(idx, carry, g=g, dil=dil, n_blocks=n_blocks):
            r = idx % dil
            start = r + (idx // dil) * (BAND * dil)
            cur = span0 + start
            prev = cur - BAND * dil
            has_prev = prev >= 0
            prev = jnp.maximum(prev, 0)
            rows = pl.ds(start, BAND, stride=dil)
            q = (q_refs[g][0, rows, :] * ATTN_SCALE).astype(bf16)
            k = jnp.concatenate([k_refs[g][0, pl.ds(prev, BAND, stride=dil), :],
                                 k_refs[g][0, pl.ds(cur, BAND, stride=dil), :]], axis=0).astype(bf16)
            v = jnp.concatenate([v_refs[g][0, pl.ds(prev, BAND, stride=dil), :],
                                 v_refs[g][0, pl.ds(cur, BAND, stride=dil), :]], axis=0).astype(bf16)
            mask = band & ((kj >= BAND) | has_prev)
            v1 = jnp.concatenate([v, jnp.ones_like(v)], axis=-1)
            out = lse = None
            for h in range(heads):
                mine = head_of_lane == h
                s = jnp.where(mask, _dot_nt(jnp.where(mine, q, jnp.zeros_like(q)), k), NEG)
                m = jnp.max(s, axis=-1, keepdims=True)
                pv = _dot(jnp.exp(s - m).astype(bf16), v1)
                den = pv[:, ATTN_LANES:]
                o_h = pv[:, :ATTN_LANES] / den
                l_h = m + jnp.log(den)
                out = o_h if out is None else jnp.where(mine, o_h, out)
                lse = l_h if lse is None else jnp.where(mine, l_h, lse)
            o_scr[g, rows, :] = out
            l_scr[g, rows, :] = lse
            return carry

        lax.fori_loop(0, n_blocks * dil, block, 0, unroll=4)
    ls = [l_scr[g] for g in range(N_GROUPS)]
    m = functools.reduce(jnp.maximum, ls)
    ws = [jnp.exp(l - m) for l in ls]
    num = sum(ws[g] * o_scr[g] for g in range(N_GROUPS))
    a_ref[...] = num / sum(ws)


def _attn_prompt(qkv, bsz, seq):
    assert seq % ATTN_SPAN == 0
    qkv3 = qkv.reshape(bsz, seq, QKV_W)
    n_spans = seq // ATTN_SPAN
    lane_blocks = GROUP_W // ATTN_LANES

    def col(g, which):
        return lambda b, hp, sp: (g * 3 + which) * lane_blocks + hp

    q_specs = [pl.BlockSpec((1, ATTN_SPAN, ATTN_LANES),
                            functools.partial(lambda b, hp, sp, c: (b, sp, c(b, hp, sp)), c=col(g, 0)))
               for g in range(N_GROUPS)]
    kv_specs = [pl.BlockSpec((1, seq, ATTN_LANES),
                             functools.partial(lambda b, hp, sp, c: (b, 0, c(b, hp, sp)), c=col(g, which)))
                for which in (1, 2) for g in range(N_GROUPS)]
    a = pl.pallas_call(
        _attn_prompt_fused_body,
        grid=(bsz, lane_blocks, n_spans),
        in_specs=q_specs + kv_specs,
        out_specs=pl.BlockSpec((ATTN_SPAN, ATTN_LANES), lambda b, hp, sp: (b * n_spans + sp, hp)),
        out_shape=jax.ShapeDtypeStruct((bsz * seq, GROUP_W), f32),
        scratch_shapes=[pltpu.VMEM((N_GROUPS, ATTN_SPAN, ATTN_LANES), f32),
                        pltpu.VMEM((N_GROUPS, ATTN_SPAN, ATTN_LANES), f32)],
        compiler_params=_params("parallel", "parallel", "arbitrary"),
        name="attn_prompt",
    )(*([qkv3] * (3 * N_GROUPS)))
    return a


def _attn_sample_body(qkv_ref, c0_ref, c1_ref, c2_ref, a_ref, *, n_new):
    caches = (c0_ref, c1_ref, c2_ref)
    qkv = qkv_ref[...]
    masks = []
    for window, dil in ATTN_GROUPS:
        s_i = lax.broadcasted_iota(jnp.int32, (n_new, window), 0)
        p_i = lax.broadcasted_iota(jnp.int32, (n_new, window), 1)
        masks.append((p_i >= s_i) & (((s_i - p_i) & (dil - 1)) == 0))
    s_n = lax.broadcasted_iota(jnp.int32, (n_new, n_new), 0)
    k_n = lax.broadcasted_iota(jnp.int32, (n_new, n_new), 1)
    new_masks = [(k_n <= s_n) & (((s_n - k_n) & (dil - 1)) == 0) for _, dil in ATTN_GROUPS]
    for h in range(N_HEADS):
        hs = slice(h * HEAD_DIM, (h + 1) * HEAD_DIM)
        scores, new_scores, new_vals = [], [], []
        for g in range(N_GROUPS):
            base = g * 3 * GROUP_W
            q = (qkv[:, base + h * HEAD_DIM:base + (h + 1) * HEAD_DIM] * ATTN_SCALE).astype(bf16)
            kn = qkv[:, base + GROUP_W + h * HEAD_DIM:base + GROUP_W + (h + 1) * HEAD_DIM].astype(bf16)
            new_vals.append(qkv[:, base + 2 * GROUP_W + h * HEAD_DIM:base + 2 * GROUP_W + (h + 1) * HEAD_DIM]
                            .astype(bf16))
            scores.append(jnp.where(masks[g], _dot(q, caches[g][0, 0, h].astype(bf16)), NEG))
            new_scores.append(jnp.where(new_masks[g], _dot_nt(q, kn), NEG))
        m = scores[0].max(axis=-1, keepdims=True)
        for sc in scores[1:] + new_scores:
            m = jnp.maximum(m, sc.max(axis=-1, keepdims=True))
        den = jnp.zeros((n_new, 1), f32)
        o = jnp.zeros((n_new, HEAD_DIM), f32)
        for g in range(N_GROUPS):
            p = jnp.exp(scores[g] - m)
            pn = jnp.exp(new_scores[g] - m)
            den = den + jnp.sum(p, axis=-1, keepdims=True) + jnp.sum(pn, axis=-1, keepdims=True)
            o = o + _dot_nt(p.astype(bf16), caches[g][0, 1, h].astype(bf16)) + _dot(pn.astype(bf16), new_vals[g])
        a_ref[:, hs] = o / den


def _attn_sample(qkv, caches, bsz, n_new):
    views, specs = [], []
    for g, (window, dil) in enumerate(ATTN_GROUPS):
        assert caches[g].shape[1] == window == BAND * dil and dil & (dil - 1) == 0
        views.append(jnp.transpose(caches[g], (0, 2, 3, 4, 1)))
        specs.append(pl.BlockSpec((1, 2, N_HEADS, HEAD_DIM, window), lambda b: (b, 0, 0, 0, 0)))
    return pl.pallas_call(
        functools.partial(_attn_sample_body, n_new=n_new),
        grid=(bsz,),
        in_specs=[pl.BlockSpec((n_new, QKV_W), lambda b: (b, 0))] + specs,
        out_specs=pl.BlockSpec((n_new, GROUP_W), lambda b: (b, 0)),
        out_shape=jax.ShapeDtypeStruct((bsz * n_new, GROUP_W), f32),
        compiler_params=_params("parallel"),
        name="attn_sample",
    )(qkv, *views)


def _ssd_conv(ext_ref, n, cw_ref, cb_ref):
    y = cb_ref[...]
    for j in range(CONV_W):
        off = SUBLANES - (CONV_W - 1) + j
        y = y + ext_ref[off:off + n, :] * cw_ref[j:j + 1, :]
    return _silu(y)


def _gate_norm(y, z, g, d_inner):
    y = y * _silu(z)
    gw = d_inner // SSM_GROUPS
    parts = []
    for gi in range(SSM_GROUPS):
        yg = y[:, gi * gw:(gi + 1) * gw]
        parts.append(yg * lax.rsqrt(jnp.mean(yg * yg, axis=-1, keepdims=True) + RMS_EPS))
    return jnp.concatenate(parts, axis=-1) * g


def _ssd_prompt_body(xbc_ref, z_ref, dt_ref, cw_ref, cb_ref, dtb_ref, alog_ref, dtbt_ref, alogt_ref,
                     dskip_ref, ng_ref, ex_ref, s_ref, hl_ref, ext_ref, ht_ref, y_ref, *, d_inner, n_heads):
    c = pl.program_id(1)
    q = xbc_ref.shape[0]
    gw = d_inner // SSM_GROUPS
    hpg = n_heads // SSM_GROUPS

    @pl.when(c == 0)
    def _():
        ext_ref[0:SUBLANES, :] = jnp.zeros((SUBLANES, ext_ref.shape[1]), f32)
        ht_ref[...] = jnp.zeros_like(ht_ref)

    @pl.when(c > 0)
    def _():
        ext_ref[0:SUBLANES, :] = ext_ref[q:q + SUBLANES, :]

    ext_ref[SUBLANES:SUBLANES + q, :] = xbc_ref[...]
    xc = _ssd_conv(ext_ref, q, cw_ref, cb_ref)
    xs = xc[:, :d_inner]
    gn = SSM_GROUPS * D_STATE
    bm = xc[:, d_inner:d_inner + gn].astype(bf16)
    cm = xc[:, d_inner + gn:].astype(bf16)

    dtr = dt_ref[...]
    dt = _softplus(dtr + dtb_ref[...])
    da = dt * (-jnp.exp(alog_ref[...]))
    dt_t = _softplus(dtr.T[0:n_heads, :] + dtbt_ref[...])
    da_t = dt_t * (-jnp.exp(alogt_ref[...]))
    ii = lax.broadcasted_iota(jnp.int32, (q, q), 0)
    jj = lax.broadcasted_iota(jnp.int32, (q, q), 1)
    causal = ii >= jj
    lower = causal.astype(bf16)
    upper = (ii <= jj).astype(bf16)
    cum = _dot3_rhs(lower, da)
    cum_t = _dot3_lhs(da_t, upper)
    ex = ex_ref[...]
    cumx = _dot3_lhs(cum, ex)
    dtx = _dot3_lhs(dt, ex)
    clx = cumx[q - 1:q, :]
    ecum = jnp.exp(cumx)
    xd = (jnp.exp(clx - cumx) * dtx * xs).astype(bf16)
    xsb = xs.astype(bf16)

    for g in range(SSM_GROUPS):
        gl = slice(g * gw, (g + 1) * gw)
        cg = cm[:, g * D_STATE:(g + 1) * D_STATE]
        bg = bm[:, g * D_STATE:(g + 1) * D_STATE]
        cb = _dot_nt(cg, bg)
        h_old = ht_ref[g]
        y_ref[:, gl] = ecum[:, gl] * _dot(cg, h_old.astype(bf16))
        ht_ref[g] = jnp.exp(clx[:, gl]) * h_old + _dot_tn(bg, xd[:, gl])
        for e in range(hpg):
            hd = g * hpg + e
            hl = slice(hd * SSM_HEAD_DIM, (hd + 1) * SSM_HEAD_DIM)
            seg = jnp.broadcast_to(cum[:, hd:hd + 1], (q, q)) - jnp.broadcast_to(cum_t[hd:hd + 1, :], (q, q))
            w = cb * jnp.exp(jnp.where(causal, seg, NEG)) * jnp.broadcast_to(dt_t[hd:hd + 1, :], (q, q))
            y_ref[:, hl] += _dot(w.astype(bf16), xsb[:, hl])

    y = y_ref[...] + dskip_ref[...] * xs
    s_ref[...] = _gate_norm(y, z_ref[...], ng_ref[...], d_inner).astype(s_ref.dtype)

    @pl.when(c == pl.num_programs(1) - 1)
    def _():
        for g in range(SSM_GROUPS):
            hl_ref[0, g * gw:(g + 1) * gw, :] = ht_ref[g].T


def _ssd_prompt(xbc, z, dt, prm, bsz, seq, *, chunk):
    n, d_xbc = xbc.shape
    d_inner = z.shape[1]
    n_heads = d_inner // SSM_HEAD_DIM
    nc = seq // chunk
    gw = d_inner // SSM_GROUPS

    def row(w):
        return pl.BlockSpec((chunk, w), lambda b, c: (b * nc + c, 0))

    def full(a):
        return pl.BlockSpec(a.shape, lambda b, c: (0,) * a.ndim)

    consts = [prm["conv_w"], prm["conv_b"], prm["dtb_row"], prm["alog_row"], prm["dtb_t"], prm["alog_t"],
              prm["dskip_x"], prm["norm_g"], prm["ex"]]
    return pl.pallas_call(
        functools.partial(_ssd_prompt_body, d_inner=d_inner, n_heads=n_heads),
        grid=(bsz, nc),
        in_specs=[row(d_xbc), row(d_inner), row(LANES)] + [full(a) for a in consts],
        out_specs=(row(d_inner), pl.BlockSpec((1, d_inner, D_STATE), lambda b, c: (b, 0, 0))),
        out_shape=(jax.ShapeDtypeStruct((n, d_inner), bf16),
                   jax.ShapeDtypeStruct((bsz, d_inner, D_STATE), f32)),
        scratch_shapes=[pltpu.VMEM((SUBLANES + chunk + SUBLANES, d_xbc), f32),
                        pltpu.VMEM((SSM_GROUPS, D_STATE, gw), f32),
                        pltpu.VMEM((chunk, d_inner), f32)],
        compiler_params=_params("parallel", "arbitrary"),
        name="ssd_prompt",
    )(xbc, z, dt, *consts)


def _ssd_sample_body(xbc_ref, cs_ref, z_ref, dt_ref, h_ref, cw_ref, cb_ref, dtb_ref, alog_ref,
                     dskip_ref, ng_ref, ex_ref, rsel_ref, s_ref, hn_ref, ext_ref, *, d_inner, n_heads):
    q = xbc_ref.shape[0]
    gw = d_inner // SSM_GROUPS
    ext_ref[0:SUBLANES, :] = cs_ref[0]
    ext_ref[SUBLANES:SUBLANES + q, :] = xbc_ref[...]
    xc = _ssd_conv(ext_ref, q, cw_ref, cb_ref)
    xs = xc[:, :d_inner]
    gn = SSM_GROUPS * D_STATE
    bm = xc[:, d_inner:d_inner + gn]
    cm = xc[:, d_inner + gn:]

    dt = _softplus(dt_ref[...] + dtb_ref[...])
    da = dt * (-jnp.exp(alog_ref[...]))
    row = lax.broadcasted_iota(jnp.int32, (q, LANES), 0)
    cum = jnp.zeros((q, LANES), f32)
    for j in range(q):
        cum = cum + jnp.where(row >= j, da[j:j + 1, :], 0.0)
    ex = ex_ref[...]
    cumx = _dot3_lhs(cum, ex)
    dtx = _dot3_lhs(dt, ex)
    clx = cumx[q - 1:q, :]
    rowx = lax.broadcasted_iota(jnp.int32, (q, d_inner), 0)

    y = dskip_ref[...] * xs
    for j in range(q):
        prod = cm * bm[j:j + 1, :]
        cbx = jnp.concatenate(
            [jnp.broadcast_to(jnp.sum(prod[:, g * D_STATE:(g + 1) * D_STATE], axis=-1, keepdims=True), (q, gw))
             for g in range(SSM_GROUPS)], axis=-1)
        seg = jnp.where(rowx >= j, cumx - cumx[j:j + 1, :], NEG)
        y = y + cbx * jnp.exp(seg) * (dtx[j:j + 1, :] * xs[j:j + 1, :])

    xd = (jnp.exp(clx - cumx) * dtx * xs).astype(bf16)
    ones = jnp.ones((q, LANES), bf16)
    da_hi, da_mid, da_lo = _split3(da)
    cl_b = _dot_tn(da_hi, ones) + _dot_tn(da_mid, ones) + _dot_tn(da_lo, ones)
    decay = jnp.exp(_dot3_rhs(rsel_ref[...], cl_b))
    ecum = jnp.exp(cumx)
    cmb = cm.astype(bf16)
    bmb = bm.astype(bf16)
    ys = []
    for g in range(SSM_GROUPS):
        rows = slice(g * gw, (g + 1) * gw)
        hg = h_ref[0, rows, :]
        ys.append(_dot_nt(cmb[:, g * D_STATE:(g + 1) * D_STATE], hg.astype(bf16)))
        hn_ref[0, rows, :] = decay[rows, :] * hg + _dot_tn(xd[:, rows], bmb[:, g * D_STATE:(g + 1) * D_STATE])
    y = y + ecum * jnp.concatenate(ys, axis=-1)
    s_ref[...] = _gate_norm(y, z_ref[...], ng_ref[...], d_inner)


def _ssd_sample(xbc, conv_state8, z, dt, h0, prm, bsz, n_new):
    n, d_xbc = xbc.shape
    d_inner = z.shape[1]
    n_heads = d_inner // SSM_HEAD_DIM

    def row(w):
        return pl.BlockSpec((n_new, w), lambda b: (b, 0))

    def full(a):
        return pl.BlockSpec(a.shape, lambda b: (0,) * a.ndim)

    consts = [prm["conv_w"], prm["conv_b"], prm["dtb_row"], prm["alog_row"],
              prm["dskip_x"], prm["norm_g"], prm["ex"], prm["rsel"]]
    state = pl.BlockSpec((1, d_inner, D_STATE), lambda b: (b, 0, 0))
    return pl.pallas_call(
        functools.partial(_ssd_sample_body, d_inner=d_inner, n_heads=n_heads),
        grid=(bsz,),
        in_specs=[row(d_xbc), pl.BlockSpec((1, SUBLANES, d_xbc), lambda b: (b, 0, 0)), row(d_inner), row(LANES),
                  state] + [full(a) for a in consts],
        out_specs=(row(d_inner), state),
        out_shape=(jax.ShapeDtypeStruct((n, d_inner), f32),
                   jax.ShapeDtypeStruct((bsz, d_inner, D_STATE), f32)),
        scratch_shapes=[pltpu.VMEM((2 * SUBLANES, d_xbc), f32)],
        compiler_params=_params("parallel"),
        name="ssd_sample",
    )(xbc, conv_state8, z, dt, h0, *consts)


def _mixer_body(a_ref, s_ref, ga_ref, gs_ref, xn_ref, gt_ref, sc_ref, sh_ref, wa_ref, ws_ref, wo_ref,
                lg_ref, lb_ref, wrh_ref, wrl_ref, br_ref, x1_ref, u2_ref, lo_ref):
    m = (jax.nn.sigmoid(ga_ref[...]) * _dot(a_ref[...].astype(bf16), wa_ref[...])
         + jax.nn.sigmoid(gs_ref[...]) * _dot(s_ref[...].astype(bf16), ws_ref[...]))
    o = _dot(m.astype(bf16), wo_ref[...])
    x1 = _layer_norm(DN_ALPHA * xn_ref[...] + gt_ref[0] * o, lg_ref[...], lb_ref[...])
    x1_ref[...] = x1
    u2 = x1 * (1.0 + sc_ref[0]) + sh_ref[0]
    u2_ref[...] = u2
    hi, mid, lo = _split3(u2)
    wrh = wrh_ref[...]
    wrl = wrl_ref[...]
    lo_ref[...] = (_dot(hi, wrh) + (_dot(hi, wrl) + _dot(mid, wrh)) + (_dot(mid, wrl) + _dot(lo, wrh))
                   + br_ref[...])


def _mixer(a, s, gates, xn, gt, sc, sh, wts, *, tm, tiles_per_mod, name):
    n, d = xn.shape

    def row(w, col=0):
        return pl.BlockSpec((tm, w), lambda i: (i, col))

    def full(arr):
        return pl.BlockSpec(arr.shape, lambda i: (0,) * arr.ndim)

    consts = [wts["w_attn_br"], wts["w_ssd_br"], wts["w_out"], wts["ln_mix_g"], wts["ln_mix_b"],
              wts["w_router_hi"], wts["w_router_lo"], wts["b_router"]]
    return pl.pallas_call(
        _mixer_body,
        grid=(n // tm,),
        in_specs=[row(a.shape[1]), row(s.shape[1]), row(d, 0), row(d, 1), row(d),
                  _mod_spec(gt, tm, tiles_per_mod), _mod_spec(sc, tm, tiles_per_mod),
                  _mod_spec(sh, tm, tiles_per_mod)] + [full(c) for c in consts],
        out_specs=(row(d), row(d), row(LANES)),
        out_shape=(jax.ShapeDtypeStruct((n, d), f32), jax.ShapeDtypeStruct((n, d), f32),
                   jax.ShapeDtypeStruct((n, LANES), f32)),
        compiler_params=_params("parallel"),
        name=name,
    )(a, s, gates, gates, xn, gt, sc, sh, *consts)


def _moe_body(be_ref, nv_ref, nr_ref, tok_ref, tok_next_ref, slot_ref, x_hbm, wg_ref, bg_ref, wu_ref, bu_ref, wd_ref,
              bd_ref, y_hbm, xbuf, ybuf, sem_in, sem_out, *, bm):
    i = pl.program_id(0)
    nv = nv_ref[0]
    cur = i % 2
    nxt = 1 - cur

    def start_gather(idx_ref, buf):
        def body(r, carry):
            pltpu.make_async_copy(x_hbm.at[pl.ds(idx_ref[0, 0, r], 1), :], xbuf.at[buf, pl.ds(r, 1), :],
                                  sem_in.at[buf]).start()
            return carry
        lax.fori_loop(0, bm, body, 0, unroll=8)

    def wait_gather(buf):
        pltpu.make_async_copy(x_hbm.at[pl.ds(0, bm), :], xbuf.at[buf], sem_in.at[buf]).wait()

    def scatter_row(buf, r, slot):
        return pltpu.make_async_copy(ybuf.at[buf, pl.ds(r, 1), :], y_hbm.at[pl.ds(slot, 1), :], sem_out.at[buf])

    def start_scatter(buf, n_real):
        def body(r, carry):
            scatter_row(buf, r, slot_ref[0, 0, r]).start()
            return carry

        @pl.when(n_real == bm)
        def _():
            lax.fori_loop(0, bm, body, 0, unroll=8)

        @pl.when(n_real < bm)
        def _():
            lax.fori_loop(0, n_real, body, 0)

    def wait_scatter(buf, n_real):
        @pl.when(n_real == bm)
        def _():
            pltpu.make_async_copy(ybuf.at[buf], y_hbm.at[pl.ds(0, bm), :], sem_out.at[buf]).wait()

        @pl.when(n_real < bm)
        def _():
            def body(r, carry):
                scatter_row(buf, 0, 0).wait()
                return carry
            lax.fori_loop(0, n_real, body, 0)

    @pl.when((i == 0) & (nv > 0))
    def _():
        start_gather(tok_ref, cur)

    @pl.when(i + 1 < nv)
    def _():
        start_gather(tok_next_ref, nxt)

    @pl.when(i < nv)
    def _():
        wait_gather(cur)
        xb = xbuf[cur].astype(bf16)
        gate = jnp.minimum(_dot(xb, wg_ref[0]) + bg_ref[0], SWIGLU_LIMIT)
        up = jnp.clip(_dot(xb, wu_ref[0]) + bu_ref[0], -SWIGLU_LIMIT, SWIGLU_LIMIT)
        h = gate * jax.nn.sigmoid(SWIGLU_ALPHA * gate) * (up + 1.0)
        y = _dot(h.astype(bf16), wd_ref[0]) + bd_ref[0]

        @pl.when(i >= 1)
        def _():
            wait_scatter(nxt, nr_ref[jnp.maximum(i - 1, 0)])

        ybuf[cur] = y
        start_scatter(cur, nr_ref[i])

        @pl.when(i == nv - 1)
        def _():
            wait_scatter(cur, nr_ref[i])


def _moe(u2, logits, wts, *, bm):
    n_tok, d = u2.shape
    d_ff = wts["w_gate"].shape[2]
    n_assign = n_tok * TOP_K
    i32 = jnp.int32
    top_logit, top_idx = lax.top_k(logits[:, :N_EXPERTS], TOP_K)
    top_gate = jax.nn.softmax(top_logit, axis=-1)
    e_flat = top_idx.reshape(-1)
    order = jnp.argsort(e_flat).astype(i32)
    experts = jnp.arange(N_EXPERTS, dtype=i32)
    counts = jnp.sum((e_flat[:, None] == experts[None, :]).astype(i32), axis=0)
    starts = jnp.cumsum(counts) - counts
    padded = (counts + bm - 1) // bm * bm
    pad_ends = jnp.cumsum(padded)
    pad_starts = pad_ends - padded
    n_blocks = -(-n_assign // bm) + N_EXPERTS
    blk_row = jnp.arange(n_blocks, dtype=i32) * bm
    block_expert = jnp.minimum(jnp.sum((pad_ends[None, :] <= blk_row[:, None]).astype(i32), axis=1), N_EXPERTS - 1)
    off = blk_row - pad_starts[block_expert]
    n_real = jnp.clip(counts[block_expert] - off, 0, bm)
    j = jnp.arange(bm, dtype=i32)[None, :]
    src = jnp.clip(starts[block_expert][:, None] + off[:, None] + j, 0, n_assign - 1)
    a_id = jnp.take(order, src.reshape(-1), axis=0).reshape(n_blocks, bm)
    real = j < n_real[:, None]
    row_tok = jnp.where(real, a_id // TOP_K, 0)
    row_slot = jnp.where(real, (a_id % TOP_K) * n_tok + a_id // TOP_K, 0)
    n_valid = (pad_ends[-1] // bm).astype(i32).reshape(1)

    def wspec(shape):
        return pl.BlockSpec((1,) + shape, lambda i, be, nv, nr: (be[i], 0, 0))

    idx_spec = pl.BlockSpec((1, 1, bm), lambda i, be, nv, nr: (i, 0, 0), memory_space=pltpu.SMEM)
    idx_next = pl.BlockSpec((1, 1, bm), lambda i, be, nv, nr: (jnp.minimum(i + 1, n_blocks - 1), 0, 0),
                            memory_space=pltpu.SMEM)
    grid_spec = pltpu.PrefetchScalarGridSpec(
        num_scalar_prefetch=3,
        grid=(n_blocks,),
        in_specs=[idx_spec, idx_next, idx_spec,
                  pl.BlockSpec(memory_space=pl.ANY),
                  wspec((d, d_ff)), wspec((1, d_ff)), wspec((d, d_ff)), wspec((1, d_ff)),
                  wspec((d_ff, d)), wspec((1, d))],
        out_specs=pl.BlockSpec(memory_space=pl.ANY),
        scratch_shapes=[pltpu.VMEM((2, bm, d), f32), pltpu.VMEM((2, bm, d), f32),
                        pltpu.SemaphoreType.DMA((2,)), pltpu.SemaphoreType.DMA((2,))],
    )
    row_tok3 = row_tok.reshape(n_blocks, 1, bm)
    y = pl.pallas_call(
        functools.partial(_moe_body, bm=bm),
        grid_spec=grid_spec,
        out_shape=jax.ShapeDtypeStruct((n_assign, d), f32),
        compiler_params=_params("arbitrary"),
        name=f"moe_bm{bm}",
    )(block_expert, n_valid, n_real, row_tok3, row_tok3, row_slot.reshape(n_blocks, 1, bm), u2,
      wts["w_gate"], wts["b_gate"], wts["w_up"], wts["b_up"], wts["w_down"], wts["b_down"])
    return y, top_gate


def _final_body(x1_ref, *rest):
    yk_refs, (pg_ref, gt_ref, g_ref, b_ref, o_ref) = rest[:TOP_K], rest[TOP_K:]
    pg = pg_ref[...]
    y = pg[:, 0:1] * yk_refs[0][...]
    for k in range(1, TOP_K):
        y = y + pg[:, k:k + 1] * yk_refs[k][...]
    o_ref[...] = _layer_norm(DN_ALPHA * x1_ref[...] + gt_ref[0] * y, g_ref[...], b_ref[...])


def _final(x1, yk, top_gate, gt, g, b, *, tm, tiles_per_mod, name):
    n, d = x1.shape
    nt = n // tm
    row = pl.BlockSpec((tm, d), lambda i: (i, 0))
    vec = pl.BlockSpec((1, d), lambda i: (0, 0))
    planes = [pl.BlockSpec((tm, d), functools.partial(lambda i, k: (k * nt + i, 0), k=k)) for k in range(TOP_K)]
    return pl.pallas_call(
        _final_body,
        grid=(nt,),
        in_specs=[row] + planes + [pl.BlockSpec((tm, TOP_K), lambda i: (i, 0)),
                                   _mod_spec(gt, tm, tiles_per_mod), vec, vec],
        out_specs=row,
        out_shape=jax.ShapeDtypeStruct((n, d), f32),
        compiler_params=_params("parallel"),
        name=name,
    )(x1, *([yk] * TOP_K), top_gate, gt, g, b)


def _stream(x, c, wts, ssd_prm, *, caches=None, conv_state=None, ssm_state=None):
    bsz, seq, d = x.shape
    n = bsz * seq
    is_prompt = caches is None
    d_inner = wts["w_z"].shape[1]
    d_xbc = wts["w_xbc"].shape[1]

    c_pad = jnp.pad(c, ((0, (-bsz) % SUBLANES), (0, 0)))
    mod = _mm(c_pad, wts["w_ada"], wts["b_ada"], tm=c_pad.shape[0], tn=2 * d, act="silu", name="ada_mod")[:bsz]
    sh1, sc1, gt1, sh2, sc2, gt2 = jnp.split(mod, 6, axis=-1)

    def per_tile(p, tile):
        if seq % tile == 0:
            return p.reshape(bsz, 1, d), seq // tile
        return jnp.broadcast_to(p[:, None, :], (bsz, seq, d)).reshape(n // tile, tile, d), 1

    tm = 512
    tpm = per_tile(sc1, tm)[1]
    xn, u = _ln_mod(x.reshape(n, d), wts["ln_emb_g"], wts["ln_emb_b"], per_tile(sc1, tm)[0], per_tile(sh1, tm)[0],
                    tm=tm, tiles_per_mod=tpm, name="ln_mod")
    tmm = 1024
    qkv = _mm(u, wts["w_qkv"], wts["zero_b"][:, :QKV_W], tm=tmm, tn=512, name="proj_qkv")
    z = _mm(u, wts["w_z"], wts["zero_b"][:, :d_inner], tm=tmm, tn=512, name="proj_z")
    xbc = _mm(u, wts["w_xbc"], wts["zero_b"][:, :d_xbc], tm=tmm, tn=512, name="proj_xbc")
    dt = _mm(u, wts["w_dt"], wts["zero_b"][:, :LANES], tm=tmm, tn=LANES, name="proj_dt")
    gates = _mm(u, wts["w_gates"], wts["zero_b"][:, :2 * d], tm=tmm, tn=512, name="proj_gates")

    qkv3 = qkv.reshape(bsz, seq, QKV_W)
    xbc3 = xbc.reshape(bsz, seq, d_xbc)

    def kv_rows_of(g, keep):
        cols = qkv3[:, seq - keep:, (g * 3 + 1) * GROUP_W:(g * 3 + 3) * GROUP_W]
        return cols.reshape(bsz, keep, 2, N_HEADS, HEAD_DIM)

    if is_prompt:
        a = _attn_prompt(qkv, bsz, seq)
        s, h_last = _ssd_prompt(xbc, z, dt, ssd_prm, bsz, seq, chunk=128)
        kv_rows = [kv_rows_of(g, min(w, seq)) for g, (w, _) in enumerate(ATTN_GROUPS)]
        new_conv = xbc3[:, -(CONV_W - 1):]
        moe_bm = 256
    else:
        a = _attn_sample(qkv, caches, bsz, seq)
        cs8 = jnp.pad(conv_state, ((0, 0), (SUBLANES - (CONV_W - 1), 0), (0, 0)))
        s, h_last = _ssd_sample(xbc, cs8, z, dt, ssm_state.reshape(bsz, d_inner, D_STATE), ssd_prm, bsz, seq)
        kv_rows = [kv_rows_of(g, seq) for g in range(N_GROUPS)]
        new_conv = jnp.concatenate([conv_state, xbc3], axis=1)[:, -(CONV_W - 1):]
        moe_bm = 128
    new_ssm = h_last.reshape(bsz, d_inner // SSM_HEAD_DIM, SSM_HEAD_DIM, D_STATE)

    tmx = 256
    x1, u2, logits = _mixer(a, s, gates, xn, per_tile(gt1, tmx)[0], per_tile(sc2, tmx)[0], per_tile(sh2, tmx)[0],
                            wts, tm=tmx, tiles_per_mod=per_tile(gt1, tmx)[1], name="mixer")
    yk, top_gate = _moe(u2, logits, wts, bm=moe_bm)
    y = _final(x1, yk, top_gate, per_tile(gt2, tm)[0], wts["ln_ffn_g"], wts["ln_ffn_b"], tm=tm, tiles_per_mod=tpm,
               name="final_ln")
    return y.reshape(bsz, seq, d), kv_rows, new_conv, new_ssm


def kernel(x_prompt, x_sample, c_prompt, c_sample, cache_kv_w128, cache_kv_w512, cache_kv_w2048, state_conv, state_ssm, ln_emb_g, ln_emb_b, w_ada, b_ada, w_in, conv_w, conv_b, dt_bias, a_log, d_skip, ssm_norm_g, w_attn_br, w_ssd_br, w_out, ln_mix_g, ln_mix_b, w_router, b_router, w_gate, b_gate, w_up, b_up, w_down, b_down, ln_ffn_g, ln_ffn_b):
    d = x_prompt.shape[-1]
    d_inner = ssm_norm_g.shape[-1]
    d_xbc = conv_w.shape[-1]
    n_heads = dt_bias.shape[-1]
    lyr = 0

    def rowv(v):
        return v.reshape(1, -1).astype(f32)

    def lane_pad(v):
        return jnp.pad(v.astype(f32), (0, LANES - n_heads)).reshape(1, LANES)

    wi = w_in[lyr]
    o0 = QKV_W
    o1 = o0 + d_inner
    o2 = o1 + d_xbc
    o3 = o2 + n_heads
    wr = jnp.pad(w_router[lyr], ((0, 0), (0, LANES - N_EXPERTS)))
    wr_hi = wr.astype(bf16)
    wts = {
        "ln_emb_g": rowv(ln_emb_g), "ln_emb_b": rowv(ln_emb_b),
        "w_ada": w_ada[lyr].astype(bf16), "b_ada": rowv(b_ada[lyr]),
        "w_qkv": wi[:, :o0].astype(bf16), "w_z": wi[:, o0:o1].astype(bf16), "w_xbc": wi[:, o1:o2].astype(bf16),
        "w_dt": jnp.pad(wi[:, o2:o3], ((0, 0), (0, LANES - n_heads))).astype(bf16),
        "w_gates": wi[:, o3:].astype(bf16),
        "zero_b": jnp.zeros((1, max(QKV_W, d_xbc, 2 * d)), f32),
        "w_attn_br": w_attn_br[lyr].astype(bf16), "w_ssd_br": w_ssd_br[lyr].astype(bf16),
        "w_out": w_out[lyr].astype(bf16),
        "ln_mix_g": rowv(ln_mix_g[lyr]), "ln_mix_b": rowv(ln_mix_b[lyr]),
        "w_router_hi": wr_hi, "w_router_lo": (wr - wr_hi.astype(f32)).astype(bf16),
        "b_router": jnp.pad(b_router[lyr], (0, LANES - N_EXPERTS)).reshape(1, LANES),
        "w_gate": w_gate[lyr].astype(bf16), "b_gate": b_gate[lyr][:, None, :],
        "w_up": w_up[lyr].astype(bf16), "b_up": b_up[lyr][:, None, :],
        "w_down": w_down[lyr].astype(bf16), "b_down": b_down[lyr][:, None, :],
        "ln_ffn_g": rowv(ln_ffn_g[lyr]), "ln_ffn_b": rowv(ln_ffn_b[lyr]),
    }
    head_of_lane = jnp.arange(d_inner, dtype=jnp.int32) // SSM_HEAD_DIM
    ssd_prm = {
        "conv_w": conv_w[lyr], "conv_b": rowv(conv_b[lyr]),
        "dtb_row": lane_pad(dt_bias[lyr]), "alog_row": lane_pad(a_log[lyr]),
        "dtb_t": jnp.broadcast_to(dt_bias[lyr][:, None], (n_heads, LANES)),
        "alog_t": jnp.broadcast_to(a_log[lyr][:, None], (n_heads, LANES)),
        "dskip_x": jnp.repeat(d_skip[lyr], SSM_HEAD_DIM).reshape(1, d_inner),
        "norm_g": rowv(ssm_norm_g[lyr]),
        "ex": (jnp.arange(LANES, dtype=jnp.int32)[:, None] == head_of_lane[None, :]).astype(bf16),
        "rsel": (head_of_lane[:, None] == jnp.arange(LANES, dtype=jnp.int32)[None, :]).astype(bf16),
    }

    yp, kv_p, conv_p, ssm_p = _stream(x_prompt, c_prompt, wts, ssd_prm)
    ys, kv_s, conv_s, ssm_s = _stream(
        x_sample, c_sample, wts, ssd_prm,
        caches=(cache_kv_w128[lyr], cache_kv_w512[lyr], cache_kv_w2048[lyr]),
        conv_state=state_conv[lyr], ssm_state=state_ssm[lyr])
    return (yp, ys, kv_p[0][None], kv_s[0][None], kv_p[1][None], kv_s[1][None], kv_p[2][None], kv_s[2][None],
            conv_p[None], conv_s[None], ssm_p[None], ssm_s[None])
```

```python
import functools
import math

import jax
import jax.numpy as jnp
from jax import lax
from jax.experimental import pallas as pl
from jax.experimental.pallas import tpu as pltpu

f32 = jnp.float32
bf16 = jnp.bfloat16

ATTN_GROUPS = ((128, 1), (512, 4), (2048, 16))
N_GROUPS = len(ATTN_GROUPS)
N_HEADS = 8
HEAD_DIM = 64
GROUP_W = N_HEADS * HEAD_DIM
QKV_W = N_GROUPS * 3 * GROUP_W
BAND = 128
ATTN_SCALE = HEAD_DIM ** -0.5
SSM_HEAD_DIM = 64
SSM_GROUPS = 4
D_STATE = 128
CONV_W = 4
N_EXPERTS = 32
TOP_K = 4
SWIGLU_ALPHA = 1.702
SWIGLU_LIMIT = 7.0
LN_EPS = 1e-5
RMS_EPS = 1e-5
DEPTH = 1
DN_ALPHA = (2 * DEPTH) ** 0.25
NEG = -1e30

LANES = 128
SUBLANES = 8
VMEM_LIMIT = 56 * 1024 * 1024


def _params(*sem):
    return pltpu.CompilerParams(dimension_semantics=sem, vmem_limit_bytes=VMEM_LIMIT)


def _silu(x):
    return x * jax.nn.sigmoid(x)


def _softplus(x):
    return jnp.maximum(x, 0.0) + jnp.log(1.0 + jnp.exp(-jnp.abs(x)))


def _split3(x):
    hi = x.astype(bf16)
    r1 = x - hi.astype(f32)
    mid = r1.astype(bf16)
    lo = (r1 - mid.astype(f32)).astype(bf16)
    return hi, mid, lo


def _dot(a, b):
    return jnp.dot(a, b, preferred_element_type=f32)


def _dot_nt(a, b):
    return lax.dot_general(a, b, (((1,), (1,)), ((), ())), preferred_element_type=f32)


def _dot_tn(a, b):
    return lax.dot_general(a, b, (((0,), (0,)), ((), ())), preferred_element_type=f32)


def _dot3_lhs(x, w):
    hi, mid, lo = _split3(x)
    return _dot(hi, w) + _dot(mid, w) + _dot(lo, w)


def _dot3_rhs(w, x):
    hi, mid, lo = _split3(x)
    return _dot(w, hi) + _dot(w, mid) + _dot(w, lo)


def _layer_norm(x, g, b):
    mu = jnp.mean(x, axis=-1, keepdims=True)
    xc = x - mu
    var = jnp.mean(xc * xc, axis=-1, keepdims=True)
    return xc * lax.rsqrt(var + LN_EPS) * g + b


def _store_row_tiles(ref, x, lead=()):
    rows, d = x.shape
    nc = d // LANES
    for c in range(nc):
        ref[lead + (pl.ds(c, rows, stride=nc), slice(None))] = x[:, c * LANES:(c + 1) * LANES]


def _load_row_tiles(ref, rows, nc, lead=()):
    return jnp.concatenate([ref[lead + (pl.ds(c, rows, stride=nc), slice(None))] for c in range(nc)], axis=-1)


def _mm_body(a_ref, w_ref, b_ref, o_ref, *, act):
    a = a_ref[...]
    if act == "silu":
        a = _silu(a.astype(f32))
    o_ref[...] = (_dot(a.astype(bf16), w_ref[...]) + b_ref[...]).astype(o_ref.dtype)


def _mm(a, w, bias, *, tm, tn, act=None, out_dtype=f32, name):
    m, k = a.shape
    n = w.shape[1]
    return pl.pallas_call(
        functools.partial(_mm_body, act=act),
        grid=(m // tm, n // tn),
        in_specs=[pl.BlockSpec((tm, k), lambda i, j: (i, 0)),
                  pl.BlockSpec((k, tn), lambda i, j: (0, j)),
                  pl.BlockSpec((1, tn), lambda i, j: (0, j))],
        out_specs=pl.BlockSpec((tm, tn), lambda i, j: (i, j)),
        out_shape=jax.ShapeDtypeStruct((m, n), out_dtype),
        compiler_params=_params("parallel", "arbitrary"),
        name=name,
    )(a, w, bias)


def _ln_mod_body(x_ref, g_ref, b_ref, sc_ref, sh_ref, xn_ref, u_ref):
    xn = _layer_norm(x_ref[...], g_ref[...], b_ref[...])
    xn_ref[...] = xn
    u_ref[...] = (xn * (1.0 + sc_ref[0]) + sh_ref[0]).astype(bf16)


def _mod_spec(mod3, tm, tiles_per_mod):
    return pl.BlockSpec((1, mod3.shape[1], mod3.shape[2]), lambda i: (i // tiles_per_mod, 0, 0))


def _ln_mod(x, g, b, sc, sh, *, tm, tiles_per_mod, name):
    n, d = x.shape
    row = pl.BlockSpec((tm, d), lambda i: (i, 0))
    vec = pl.BlockSpec((1, d), lambda i: (0, 0))
    return pl.pallas_call(
        _ln_mod_body,
        grid=(n // tm,),
        in_specs=[row, vec, vec, _mod_spec(sc, tm, tiles_per_mod), _mod_spec(sh, tm, tiles_per_mod)],
        out_specs=(row, row),
        out_shape=(jax.ShapeDtypeStruct((n, d), f32), jax.ShapeDtypeStruct((n, d), bf16)),
        compiler_params=_params("parallel"),
        name=name,
    )(x, g, b, sc, sh)


ATTN_SPAN = BAND * max(dil for _, dil in ATTN_GROUPS)
ATTN_LANES = 2 * HEAD_DIM


def _attn_prompt_fused_body(*refs):
    q_refs = refs[0:N_GROUPS]
    k_refs = refs[N_GROUPS:2 * N_GROUPS]
    v_refs = refs[2 * N_GROUPS:3 * N_GROUPS]
    a_ref = refs[3 * N_GROUPS]
    o_scr, l_scr = refs[3 * N_GROUPS + 1:]
    span0 = pl.program_id(2) * ATTN_SPAN
    qi = lax.broadcasted_iota(jnp.int32, (BAND, 2 * BAND), 0)
    kj = lax.broadcasted_iota(jnp.int32, (BAND, 2 * BAND), 1)
    dist = qi + BAND - kj
    band = (dist >= 0) & (dist <= BAND)
    heads = ATTN_LANES // HEAD_DIM
    head_of_lane = lax.broadcasted_iota(jnp.int32, (BAND, ATTN_LANES), 1) // HEAD_DIM
    for g, (_, dil) in enumerate(ATTN_GROUPS):
        n_blocks = ATTN_SPAN // (BAND * dil)

        def block(idx, carry, g=g, dil=dil, n_blocks=n_blocks):
            r = idx % dil
            start = r + (idx // dil) * (BAND * dil)
            cur = span0 + start
            prev = cur - BAND * dil
            has_prev = prev >= 0
            prev = jnp.maximum(prev, 0)
            rows = pl.ds(start, BAND, stride=dil)
            q = (q_refs[g][0, rows, :] * ATTN_SCALE).astype(bf16)
            k = jnp.concatenate([k_refs[g][0, pl.ds(prev, BAND, stride=dil), :],
                                 k_refs[g][0, pl.ds(cur, BAND, stride=dil), :]], axis=0).astype(bf16)
            v = jnp.concatenate([v_refs[g][0, pl.ds(prev, BAND, stride=dil), :],
                                 v_refs[g][0, pl.ds(cur, BAND, stride=dil), :]], axis=0).astype(bf16)
            mask = band & ((kj >= BAND) | has_prev)
            v1 = jnp.concatenate([v, jnp.ones_like(v)], axis=-1)
            out = lse = None
            for h in range(heads):
                mine = head_of_lane == h
                s = jnp.where(mask, _dot_nt(jnp.where(mine, q, jnp.zeros_like(q)), k), NEG)
                m = jnp.max(s, axis=-1, keepdims=True)
                pv = _dot(jnp.exp(s - m).astype(bf16), v1)
                den = pv[:, ATTN_LANES:]
                o_h = pv[:, :ATTN_LANES] / den
                l_h = m + jnp.log(den)
                out = o_h if out is None else jnp.where(mine, o_h, out)
                lse = l_h if lse is None else jnp.where(mine, l_h, lse)
            o_scr[g, rows, :] = out
            l_scr[g, rows, :] = lse
            return carry

        lax.fori_loop(0, n_blocks * dil, block, 0, unroll=4)
    ls = [l_scr[g] for g in range(N_GROUPS)]
    m = functools.reduce(jnp.maximum, ls)
    ws = [jnp.exp(l - m) for l in ls]
    num = sum(ws[g] * o_scr[g] for g in range(N_GROUPS))
    a_ref[...] = num / sum(ws)


def _attn_prompt(qkv, bsz, seq):
    assert seq % ATTN_SPAN == 0
    qkv3 = qkv.reshape(bsz, seq, QKV_W)
    n_spans = seq // ATTN_SPAN
    lane_blocks = GROUP_W // ATTN_LANES

    def col(g, which):
        return lambda b, hp, sp: (g * 3 + which) * lane_blocks + hp

    q_specs = [pl.BlockSpec((1, ATTN_SPAN, ATTN_LANES),
                            functools.partial(lambda b, hp, sp, c: (b, sp, c(b, hp, sp)), c=col(g, 0)))
               for g in range(N_GROUPS)]
    kv_specs = [pl.BlockSpec((1, seq, ATTN_LANES),
                             functools.partial(lambda b, hp, sp, c: (b, 0, c(b, hp, sp)), c=col(g, which)))
                for which in (1, 2) for g in range(N_GROUPS)]
    a = pl.pallas_call(
        _attn_prompt_fused_body,
        grid=(bsz, lane_blocks, n_spans),
        in_specs=q_specs + kv_specs,
        out_specs=pl.BlockSpec((ATTN_SPAN, ATTN_LANES), lambda b, hp, sp: (b * n_spans + sp, hp)),
        out_shape=jax.ShapeDtypeStruct((bsz * seq, GROUP_W), f32),
        scratch_shapes=[pltpu.VMEM((N_GROUPS, ATTN_SPAN, ATTN_LANES), f32),
                        pltpu.VMEM((N_GROUPS, ATTN_SPAN, ATTN_LANES), f32)],
        compiler_params=_params("parallel", "parallel", "arbitrary"),
        name="attn_prompt",
    )(*([qkv3] * (3 * N_GROUPS)))
    return a


def _attn_sample_body(qkv_ref, c0_ref, c1_ref, c2_ref, a_ref, *, n_new):
    caches = (c0_ref, c1_ref, c2_ref)
    qkv = qkv_ref[...]
    masks = []
    for window, dil in ATTN_GROUPS:
        s_i = lax.broadcasted_iota(jnp.int32, (n_new, window), 0)
        p_i = lax.broadcasted_iota(jnp.int32, (n_new, window), 1)
        masks.append((p_i >= s_i) & (((s_i - p_i) & (dil - 1)) == 0))
    s_n = lax.broadcasted_iota(jnp.int32, (n_new, n_new), 0)
    k_n = lax.broadcasted_iota(jnp.int32, (n_new, n_new), 1)
    new_masks = [(k_n <= s_n) & (((s_n - k_n) & (dil - 1)) == 0) for _, dil in ATTN_GROUPS]
    for h in range(N_HEADS):
        hs = slice(h * HEAD_DIM, (h + 1) * HEAD_DIM)
        scores, new_scores, new_vals = [], [], []
        for g in range(N_GROUPS):
            base = g * 3 * GROUP_W
            q = (qkv[:, base + h * HEAD_DIM:base + (h + 1) * HEAD_DIM] * ATTN_SCALE).astype(bf16)
            kn = qkv[:, base + GROUP_W + h * HEAD_DIM:base + GROUP_W + (h + 1) * HEAD_DIM].astype(bf16)
            new_vals.append(qkv[:, base + 2 * GROUP_W + h * HEAD_DIM:base + 2 * GROUP_W + (h + 1) * HEAD_DIM]
                            .astype(bf16))
            scores.append(jnp.where(masks[g], _dot(q, caches[g][0, 0, h].astype(bf16)), NEG))
            new_scores.append(jnp.where(new_masks[g], _dot_nt(q, kn), NEG))
        m = scores[0].max(axis=-1, keepdims=True)
        for sc in scores[1:] + new_scores:
            m = jnp.maximum(m, sc.max(axis=-1, keepdims=True))
        den = jnp.zeros((n_new, 1), f32)
        o = jnp.zeros((n_new, HEAD_DIM), f32)
        for g in range(N_GROUPS):
            p = jnp.exp(scores[g] - m)
            pn = jnp.exp(new_scores[g] - m)
            den = den + jnp.sum(p, axis=-1, keepdims=True) + jnp.sum(pn, axis=-1, keepdims=True)
            o = o + _dot_nt(p.astype(bf16), caches[g][0, 1, h].astype(bf16)) + _dot(pn.astype(bf16), new_vals[g])
        a_ref[:, hs] = o / den


def _attn_sample(qkv, caches, bsz, n_new):
    views, specs = [], []
    for g, (window, dil) in enumerate(ATTN_GROUPS):
        assert caches[g].shape[1] == window == BAND * dil and dil & (dil - 1) == 0
        views.append(jnp.transpose(caches[g], (0, 2, 3, 4, 1)))
        specs.append(pl.BlockSpec((1, 2, N_HEADS, HEAD_DIM, window), lambda b: (b, 0, 0, 0, 0)))
    return pl.pallas_call(
        functools.partial(_attn_sample_body, n_new=n_new),
        grid=(bsz,),
        in_specs=[pl.BlockSpec((n_new, QKV_W), lambda b: (b, 0))] + specs,
        out_specs=pl.BlockSpec((n_new, GROUP_W), lambda b: (b, 0)),
        out_shape=jax.ShapeDtypeStruct((bsz * n_new, GROUP_W), f32),
        compiler_params=_params("parallel"),
        name="attn_sample",
    )(qkv, *views)


def _ssd_conv(ext_ref, n, cw_ref, cb_ref):
    y = cb_ref[...]
    for j in range(CONV_W):
        off = SUBLANES - (CONV_W - 1) + j
        y = y + ext_ref[off:off + n, :] * cw_ref[j:j + 1, :]
    return _silu(y)


def _gate_norm(y, z, g, d_inner):
    y = y * _silu(z.astype(f32))
    gw = d_inner // SSM_GROUPS
    parts = []
    for gi in range(SSM_GROUPS):
        yg = y[:, gi * gw:(gi + 1) * gw]
        parts.append(yg * lax.rsqrt(jnp.mean(yg * yg, axis=-1, keepdims=True) + RMS_EPS))
    return jnp.concatenate(parts, axis=-1) * g


def _ssd_prompt_body(xbc_ref, z_ref, dt_ref, cw_ref, cb_ref, dtb_ref, alog_ref, dtbt_ref, alogt_ref,
                     dskip_ref, ng_ref, ex_ref, s_ref, hl_ref, ext_ref, ht_ref, y_ref, *, d_inner, n_heads):
    c = pl.program_id(1)
    q = xbc_ref.shape[0]
    gw = d_inner // SSM_GROUPS
    hpg = n_heads // SSM_GROUPS

    @pl.when(c == 0)
    def _():
        ext_ref[0:SUBLANES, :] = jnp.zeros((SUBLANES, ext_ref.shape[1]), f32)
        ht_ref[...] = jnp.zeros_like(ht_ref)

    @pl.when(c > 0)
    def _():
        ext_ref[0:SUBLANES, :] = ext_ref[q:q + SUBLANES, :]

    ext_ref[SUBLANES:SUBLANES + q, :] = xbc_ref[...]
    xc = _ssd_conv(ext_ref, q, cw_ref, cb_ref)
    xs = xc[:, :d_inner]
    gn = SSM_GROUPS * D_STATE
    bm = xc[:, d_inner:d_inner + gn].astype(bf16)
    cm = xc[:, d_inner + gn:].astype(bf16)

    dtr = dt_ref[...]
    dt = _softplus(dtr + dtb_ref[...])
    da = dt * (-jnp.exp(alog_ref[...]))
    dt_t = _softplus(dtr.T[0:n_heads, :] + dtbt_ref[...])
    da_t = dt_t * (-jnp.exp(alogt_ref[...]))
    ii = lax.broadcasted_iota(jnp.int32, (q, q), 0)
    jj = lax.broadcasted_iota(jnp.int32, (q, q), 1)
    causal = ii >= jj
    lower = causal.astype(bf16)
    upper = (ii <= jj).astype(bf16)
    cum = _dot3_rhs(lower, da)
    cum_t = _dot3_lhs(da_t, upper)
    ex = ex_ref[...]
    cumx = _dot3_lhs(cum, ex)
    dtx = _dot3_lhs(dt, ex)
    clx = cumx[q - 1:q, :]
    ecum = jnp.exp(cumx)
    xd = (jnp.exp(clx - cumx) * dtx * xs).astype(bf16)
    xsb = xs.astype(bf16)

    for g in range(SSM_GROUPS):
        gl = slice(g * gw, (g + 1) * gw)
        cg = cm[:, g * D_STATE:(g + 1) * D_STATE]
        bg = bm[:, g * D_STATE:(g + 1) * D_STATE]
        cb = _dot_nt(cg, bg)
        h_old = ht_ref[g]
        y_ref[:, gl] = ecum[:, gl] * _dot(cg, h_old.astype(bf16))
        ht_ref[g] = jnp.exp(clx[:, gl]) * h_old + _dot_tn(bg, xd[:, gl])
        for e in range(hpg):
            hd = g * hpg + e
            hl = slice(hd * SSM_HEAD_DIM, (hd + 1) * SSM_HEAD_DIM)
            seg = jnp.broadcast_to(cum[:, hd:hd + 1], (q, q)) - jnp.broadcast_to(cum_t[hd:hd + 1, :], (q, q))
            w = cb * jnp.exp(jnp.where(causal, seg, NEG)) * jnp.broadcast_to(dt_t[hd:hd + 1, :], (q, q))
            y_ref[:, hl] += _dot(w.astype(bf16), xsb[:, hl])

    y = y_ref[...] + dskip_ref[...] * xs
    s_ref[...] = _gate_norm(y, z_ref[...], ng_ref[...], d_inner).astype(s_ref.dtype)

    @pl.when(c == pl.num_programs(1) - 1)
    def _():
        for g in range(SSM_GROUPS):
            hl_ref[0, g * gw:(g + 1) * gw, :] = ht_ref[g].T


def _ssd_prompt(xbc, z, dt, prm, bsz, seq, *, chunk):
    n, d_xbc = xbc.shape
    d_inner = z.shape[1]
    n_heads = d_inner // SSM_HEAD_DIM
    nc = seq // chunk
    gw = d_inner // SSM_GROUPS

    def row(w):
        return pl.BlockSpec((chunk, w), lambda b, c: (b * nc + c, 0))

    def full(a):
        return pl.BlockSpec(a.shape, lambda b, c: (0,) * a.ndim)

    consts = [prm["conv_w"], prm["conv_b"], prm["dtb_row"], prm["alog_row"], prm["dtb_t"], prm["alog_t"],
              prm["dskip_x"], prm["norm_g"], prm["ex"]]
    return pl.pallas_call(
        functools.partial(_ssd_prompt_body, d_inner=d_inner, n_heads=n_heads),
        grid=(bsz, nc),
        in_specs=[row(d_xbc), row(d_inner), row(LANES)] + [full(a) for a in consts],
        out_specs=(row(d_inner), pl.BlockSpec((1, d_inner, D_STATE), lambda b, c: (b, 0, 0))),
        out_shape=(jax.ShapeDtypeStruct((n, d_inner), bf16),
                   jax.ShapeDtypeStruct((bsz, d_inner, D_STATE), f32)),
        scratch_shapes=[pltpu.VMEM((SUBLANES + chunk + SUBLANES, d_xbc), f32),
                        pltpu.VMEM((SSM_GROUPS, D_STATE, gw), f32),
                        pltpu.VMEM((chunk, d_inner), f32)],
        compiler_params=_params("parallel", "arbitrary"),
        name="ssd_prompt",
    )(xbc, z, dt, *consts)


def _ssd_sample_body(xbc_ref, cs_ref, z_ref, dt_ref, h_ref, cw_ref, cb_ref, dtb_ref, alog_ref,
                     dskip_ref, ng_ref, ex_ref, rsel_ref, s_ref, hn_ref, ext_ref, *, d_inner, n_heads):
    q = xbc_ref.shape[0]
    gw = d_inner // SSM_GROUPS
    ext_ref[0:SUBLANES, :] = cs_ref[0]
    ext_ref[SUBLANES:SUBLANES + q, :] = xbc_ref[...]
    xc = _ssd_conv(ext_ref, q, cw_ref, cb_ref)
    xs = xc[:, :d_inner]
    gn = SSM_GROUPS * D_STATE
    bm = xc[:, d_inner:d_inner + gn]
    cm = xc[:, d_inner + gn:]

    dt = _softplus(dt_ref[...] + dtb_ref[...])
    da = dt * (-jnp.exp(alog_ref[...]))
    row = lax.broadcasted_iota(jnp.int32, (q, LANES), 0)
    cum = jnp.zeros((q, LANES), f32)
    for j in range(q):
        cum = cum + jnp.where(row >= j, da[j:j + 1, :], 0.0)
    ex = ex_ref[...]
    cumx = _dot3_lhs(cum, ex)
    dtx = _dot3_lhs(dt, ex)
    clx = cumx[q - 1:q, :]
    rowx = lax.broadcasted_iota(jnp.int32, (q, d_inner), 0)

    y = dskip_ref[...] * xs
    for j in range(q):
        prod = cm * bm[j:j + 1, :]
        cbx = jnp.concatenate(
            [jnp.broadcast_to(jnp.sum(prod[:, g * D_STATE:(g + 1) * D_STATE], axis=-1, keepdims=True), (q, gw))
             for g in range(SSM_GROUPS)], axis=-1)
        seg = jnp.where(rowx >= j, cumx - cumx[j:j + 1, :], NEG)
        y = y + cbx * jnp.exp(seg) * (dtx[j:j + 1, :] * xs[j:j + 1, :])

    xd = (jnp.exp(clx - cumx) * dtx * xs).astype(bf16)
    ones = jnp.ones((q, LANES), bf16)
    da_hi, da_mid, da_lo = _split3(da)
    cl_b = _dot_tn(da_hi, ones) + _dot_tn(da_mid, ones) + _dot_tn(da_lo, ones)
    decay = jnp.exp(_dot3_rhs(rsel_ref[...], cl_b))
    ecum = jnp.exp(cumx)
    cmb = cm.astype(bf16)
    bmb = bm.astype(bf16)
    ys = []
    for g in range(SSM_GROUPS):
        rows = slice(g * gw, (g + 1) * gw)
        hg = h_ref[0, rows, :]
        ys.append(_dot_nt(cmb[:, g * D_STATE:(g + 1) * D_STATE], hg.astype(bf16)))
        hn_ref[0, rows, :] = decay[rows, :] * hg + _dot_tn(xd[:, rows], bmb[:, g * D_STATE:(g + 1) * D_STATE])
    y = y + ecum * jnp.concatenate(ys, axis=-1)
    s_ref[...] = _gate_norm(y, z_ref[...], ng_ref[...], d_inner)


def _ssd_sample(xbc, conv_state8, z, dt, h0, prm, bsz, n_new):
    n, d_xbc = xbc.shape
    d_inner = z.shape[1]
    n_heads = d_inner // SSM_HEAD_DIM

    def row(w):
        return pl.BlockSpec((n_new, w), lambda b: (b, 0))

    def full(a):
        return pl.BlockSpec(a.shape, lambda b: (0,) * a.ndim)

    consts = [prm["conv_w"], prm["conv_b"], prm["dtb_row"], prm["alog_row"],
              prm["dskip_x"], prm["norm_g"], prm["ex"], prm["rsel"]]
    state = pl.BlockSpec((1, d_inner, D_STATE), lambda b: (b, 0, 0))
    return pl.pallas_call(
        functools.partial(_ssd_sample_body, d_inner=d_inner, n_heads=n_heads),
        grid=(bsz,),
        in_specs=[row(d_xbc), pl.BlockSpec((1, SUBLANES, d_xbc), lambda b: (b, 0, 0)), row(d_inner), row(LANES),
                  state] + [full(a) for a in consts],
        out_specs=(row(d_inner), state),
        out_shape=(jax.ShapeDtypeStruct((n, d_inner), f32),
                   jax.ShapeDtypeStruct((bsz, d_inner, D_STATE), f32)),
        scratch_shapes=[pltpu.VMEM((2 * SUBLANES, d_xbc), f32)],
        compiler_params=_params("parallel"),
        name="ssd_sample",
    )(xbc, conv_state8, z, dt, h0, *consts)


def _mixer_body(a_ref, s_ref, ga_ref, gs_ref, xn_ref, gt_ref, sc_ref, sh_ref, wa_ref, ws_ref, wo_ref,
                lg_ref, lb_ref, wrh_ref, wrl_ref, br_ref, x1_ref, u2_ref, lo_ref):
    m = (jax.nn.sigmoid(ga_ref[...].astype(f32)) * _dot(a_ref[...].astype(bf16), wa_ref[...])
         + jax.nn.sigmoid(gs_ref[...].astype(f32)) * _dot(s_ref[...].astype(bf16), ws_ref[...]))
    o = _dot(m.astype(bf16), wo_ref[...])
    x1 = _layer_norm(DN_ALPHA * xn_ref[...] + gt_ref[0] * o, lg_ref[...], lb_ref[...])
    x1_ref[...] = x1
    u2 = x1 * (1.0 + sc_ref[0]) + sh_ref[0]
    _store_row_tiles(u2_ref, u2)
    hi, mid, lo = _split3(u2)
    wrh = wrh_ref[...]
    wrl = wrl_ref[...]
    lo_ref[...] = (_dot(hi, wrh) + (_dot(hi, wrl) + _dot(mid, wrh)) + (_dot(mid, wrl) + _dot(lo, wrh))
                   + br_ref[...])


def _mixer(a, s, gates, xn, gt, sc, sh, wts, *, tm, tiles_per_mod, name):
    n, d = xn.shape

    def row(w, col=0):
        return pl.BlockSpec((tm, w), lambda i: (i, col))

    def full(arr):
        return pl.BlockSpec(arr.shape, lambda i: (0,) * arr.ndim)

    consts = [wts["w_attn_br"], wts["w_ssd_br"], wts["w_out"], wts["ln_mix_g"], wts["ln_mix_b"],
              wts["w_router_hi"], wts["w_router_lo"], wts["b_router"]]
    return pl.pallas_call(
        _mixer_body,
        grid=(n // tm,),
        in_specs=[row(a.shape[1]), row(s.shape[1]), row(d, 0), row(d, 1), row(d),
                  _mod_spec(gt, tm, tiles_per_mod), _mod_spec(sc, tm, tiles_per_mod),
                  _mod_spec(sh, tm, tiles_per_mod)] + [full(c) for c in consts],
        out_specs=(row(d), pl.BlockSpec((tm * (d // LANES), LANES), lambda i: (i, 0)), row(LANES)),
        out_shape=(jax.ShapeDtypeStruct((n, d), f32), jax.ShapeDtypeStruct((n * (d // LANES), LANES), f32),
                   jax.ShapeDtypeStruct((n, LANES), f32)),
        compiler_params=_params("parallel"),
        name=name,
    )(a, s, gates, gates, xn, gt, sc, sh, *consts)


def _moe_body(be_ref, nv_ref, nr_ref, tok_ref, tok_next_ref, slot_ref, x_hbm, wg_ref, bg_ref, wu_ref, bu_ref, wd_ref,
              bd_ref, y_hbm, xbuf, ybuf, sem_in, sem_out, *, bm):
    i = pl.program_id(0)
    nv = nv_ref[0]
    nc = xbuf.shape[1] // bm
    cur = i % 2
    nxt = 1 - cur

    def start_gather(idx_ref, buf):
        def body(r, carry):
            src = pl.multiple_of(idx_ref[0, 0, r], nc)
            pltpu.make_async_copy(x_hbm.at[pl.ds(src, nc), :], xbuf.at[buf, pl.ds(r * nc, nc), :],
                                  sem_in.at[buf]).start()
            return carry
        lax.fori_loop(0, bm, body, 0, unroll=8)

    def wait_gather(buf):
        pltpu.make_async_copy(x_hbm.at[pl.ds(0, bm * nc), :], xbuf.at[buf], sem_in.at[buf]).wait()

    def scatter_row(buf, r, slot):
        dst = pl.multiple_of(slot, nc)
        return pltpu.make_async_copy(ybuf.at[buf, pl.ds(r * nc, nc), :], y_hbm.at[pl.ds(dst, nc), :], sem_out.at[buf])

    def start_scatter(buf, n_real):
        def body(r, carry):
            scatter_row(buf, r, slot_ref[0, 0, r]).start()
            return carry

        @pl.when(n_real == bm)
        def _():
            lax.fori_loop(0, bm, body, 0, unroll=8)

        @pl.when(n_real < bm)
        def _():
            lax.fori_loop(0, n_real, body, 0)

    def wait_scatter(buf, n_real):
        @pl.when(n_real == bm)
        def _():
            pltpu.make_async_copy(ybuf.at[buf], y_hbm.at[pl.ds(0, bm * nc), :], sem_out.at[buf]).wait()

        @pl.when(n_real < bm)
        def _():
            def body(r, carry):
                scatter_row(buf, 0, 0).wait()
                return carry
            lax.fori_loop(0, n_real, body, 0)

    @pl.when((i == 0) & (nv > 0))
    def _():
        start_gather(tok_ref, cur)

    @pl.when(i + 1 < nv)
    def _():
        start_gather(tok_next_ref, nxt)

    @pl.when(i < nv)
    def _():
        wait_gather(cur)
        xb = _load_row_tiles(xbuf, bm, nc, (cur,)).astype(bf16)
        gate = jnp.minimum(_dot(xb, wg_ref[0]) + bg_ref[0], SWIGLU_LIMIT)
        up = jnp.clip(_dot(xb, wu_ref[0]) + bu_ref[0], -SWIGLU_LIMIT, SWIGLU_LIMIT)
        h = gate * jax.nn.sigmoid(SWIGLU_ALPHA * gate) * (up + 1.0)
        y = _dot(h.astype(bf16), wd_ref[0]) + bd_ref[0]

        @pl.when(i >= 1)
        def _():
            wait_scatter(nxt, nr_ref[jnp.maximum(i - 1, 0)])

        _store_row_tiles(ybuf, y, (cur,))
        start_scatter(cur, nr_ref[i])

        @pl.when(i == nv - 1)
        def _():
            wait_scatter(cur, nr_ref[i])


def _moe(u2, logits, wts, *, bm):
    d_ff, d = wts["w_down"].shape[1:]
    nc = d // LANES
    n_tok = u2.shape[0] // nc
    n_assign = n_tok * TOP_K
    i32 = jnp.int32
    top_logit, top_idx = lax.top_k(logits[:, :N_EXPERTS], TOP_K)
    top_gate = jax.nn.softmax(top_logit, axis=-1)
    e_flat = top_idx.reshape(-1)
    order = jnp.argsort(e_flat).astype(i32)
    experts = jnp.arange(N_EXPERTS, dtype=i32)
    counts = jnp.sum((e_flat[:, None] == experts[None, :]).astype(i32), axis=0)
    starts = jnp.cumsum(counts) - counts
    padded = (counts + bm - 1) // bm * bm
    pad_ends = jnp.cumsum(padded)
    pad_starts = pad_ends - padded
    n_blocks = -(-n_assign // bm) + N_EXPERTS
    blk_row = jnp.arange(n_blocks, dtype=i32) * bm
    block_expert = jnp.minimum(jnp.sum((pad_ends[None, :] <= blk_row[:, None]).astype(i32), axis=1), N_EXPERTS - 1)
    off = blk_row - pad_starts[block_expert]
    n_real = jnp.clip(counts[block_expert] - off, 0, bm)
    j = jnp.arange(bm, dtype=i32)[None, :]
    src = jnp.clip(starts[block_expert][:, None] + off[:, None] + j, 0, n_assign - 1)
    a_id = jnp.take(order, src.reshape(-1), axis=0).reshape(n_blocks, bm)
    real = j < n_real[:, None]
    row_tok = jnp.where(real, a_id // TOP_K, 0) * nc
    row_slot = jnp.where(real, (a_id % TOP_K) * n_tok + a_id // TOP_K, 0) * nc
    n_valid = (pad_ends[-1] // bm).astype(i32).reshape(1)

    def wspec(shape):
        return pl.BlockSpec((1,) + shape, lambda i, be, nv, nr: (be[i], 0, 0))

    idx_spec = pl.BlockSpec((1, 1, bm), lambda i, be, nv, nr: (i, 0, 0), memory_space=pltpu.SMEM)
    idx_next = pl.BlockSpec((1, 1, bm), lambda i, be, nv, nr: (jnp.minimum(i + 1, n_blocks - 1), 0, 0),
                            memory_space=pltpu.SMEM)
    grid_spec = pltpu.PrefetchScalarGridSpec(
        num_scalar_prefetch=3,
        grid=(n_blocks,),
        in_specs=[idx_spec, idx_next, idx_spec,
                  pl.BlockSpec(memory_space=pl.ANY),
                  wspec((d, d_ff)), wspec((1, d_ff)), wspec((d, d_ff)), wspec((1, d_ff)),
                  wspec((d_ff, d)), wspec((1, d))],
        out_specs=pl.BlockSpec(memory_space=pl.ANY),
        scratch_shapes=[pltpu.VMEM((2, bm * nc, LANES), f32), pltpu.VMEM((2, bm * nc, LANES), f32),
                        pltpu.SemaphoreType.DMA((2,)), pltpu.SemaphoreType.DMA((2,))],
    )
    row_tok3 = row_tok.reshape(n_blocks, 1, bm)
    y = pl.pallas_call(
        functools.partial(_moe_body, bm=bm),
        grid_spec=grid_spec,
        out_shape=jax.ShapeDtypeStruct((n_assign * nc, LANES), f32),
        compiler_params=_params("arbitrary"),
        name=f"moe_bm{bm}",
    )(block_expert, n_valid, n_real, row_tok3, row_tok3, row_slot.reshape(n_blocks, 1, bm), u2,
      wts["w_gate"], wts["b_gate"], wts["w_up"], wts["b_up"], wts["w_down"], wts["b_down"])
    return y, top_gate


def _final_body(x1_ref, *rest):
    yk_refs, (pg_ref, gt_ref, g_ref, b_ref, o_ref) = rest[:TOP_K], rest[TOP_K:]
    pg = pg_ref[...]
    tm, d = x1_ref.shape
    y = pg[:, 0:1] * _load_row_tiles(yk_refs[0], tm, d // LANES)
    for k in range(1, TOP_K):
        y = y + pg[:, k:k + 1] * _load_row_tiles(yk_refs[k], tm, d // LANES)
    o_ref[...] = _layer_norm(DN_ALPHA * x1_ref[...] + gt_ref[0] * y, g_ref[...], b_ref[...])


def _final(x1, yk, top_gate, gt, g, b, *, tm, tiles_per_mod, name):
    n, d = x1.shape
    nt = n // tm
    row = pl.BlockSpec((tm, d), lambda i: (i, 0))
    vec = pl.BlockSpec((1, d), lambda i: (0, 0))
    planes = [pl.BlockSpec((tm * (d // LANES), LANES), functools.partial(lambda i, k: (k * nt + i, 0), k=k))
              for k in range(TOP_K)]
    return pl.pallas_call(
        _final_body,
        grid=(nt,),
        in_specs=[row] + planes + [pl.BlockSpec((tm, TOP_K), lambda i: (i, 0)),
                                   _mod_spec(gt, tm, tiles_per_mod), vec, vec],
        out_specs=row,
        out_shape=jax.ShapeDtypeStruct((n, d), f32),
        compiler_params=_params("parallel"),
        name=name,
    )(x1, *([yk] * TOP_K), top_gate, gt, g, b)


def _stream(x, c, wts, ssd_prm, *, caches=None, conv_state=None, ssm_state=None):
    bsz, seq, d = x.shape
    n = bsz * seq
    is_prompt = caches is None
    d_inner = wts["w_z"].shape[1]
    d_xbc = wts["w_xbc"].shape[1]

    c_pad = jnp.pad(c, ((0, (-bsz) % SUBLANES), (0, 0)))
    mod = _mm(c_pad, wts["w_ada"], wts["b_ada"], tm=c_pad.shape[0], tn=2 * d, act="silu", name="ada_mod")[:bsz]
    sh1, sc1, gt1, sh2, sc2, gt2 = jnp.split(mod, 6, axis=-1)

    def per_tile(p, tile):
        if seq % tile == 0:
            return p.reshape(bsz, 1, d), seq // tile
        return jnp.broadcast_to(p[:, None, :], (bsz, seq, d)).reshape(n // tile, tile, d), 1

    tm = 512
    tpm = per_tile(sc1, tm)[1]
    xn, u = _ln_mod(x.reshape(n, d), wts["ln_emb_g"], wts["ln_emb_b"], per_tile(sc1, tm)[0], per_tile(sh1, tm)[0],
                    tm=tm, tiles_per_mod=tpm, name="ln_mod")
    tmm = min(n, 2048)
    act_dtype = bf16 if seq % 16 == 0 else f32
    qkv = _mm(u, wts["w_qkv"], wts["zero_b"][:, :QKV_W], tm=tmm, tn=512, name="proj_qkv")
    z = _mm(u, wts["w_z"], wts["zero_b"][:, :d_inner], tm=tmm, tn=512, out_dtype=act_dtype, name="proj_z")
    xbc = _mm(u, wts["w_xbc"], wts["zero_b"][:, :d_xbc], tm=tmm, tn=512, name="proj_xbc")
    dt = _mm(u, wts["w_dt"], wts["zero_b"][:, :LANES], tm=tmm, tn=LANES, name="proj_dt")
    gates = _mm(u, wts["w_gates"], wts["zero_b"][:, :2 * d], tm=tmm, tn=512, out_dtype=bf16, name="proj_gates")

    qkv3 = qkv.reshape(bsz, seq, QKV_W)
    xbc3 = xbc.reshape(bsz, seq, d_xbc)

    def kv_rows_of(g, keep):
        cols = qkv3[:, seq - keep:, (g * 3 + 1) * GROUP_W:(g * 3 + 3) * GROUP_W]
        return cols.reshape(bsz, keep, 2, N_HEADS, HEAD_DIM)

    if is_prompt:
        a = _attn_prompt(qkv, bsz, seq)
        s, h_last = _ssd_prompt(xbc, z, dt, ssd_prm, bsz, seq, chunk=128)
        kv_rows = [kv_rows_of(g, min(w, seq)) for g, (w, _) in enumerate(ATTN_GROUPS)]
        new_conv = xbc3[:, -(CONV_W - 1):]
        moe_bm = 256
    else:
        a = _attn_sample(qkv, caches, bsz, seq)
        cs8 = jnp.pad(conv_state, ((0, 0), (SUBLANES - (CONV_W - 1), 0), (0, 0)))
        s, h_last = _ssd_sample(xbc, cs8, z, dt, ssm_state.reshape(bsz, d_inner, D_STATE), ssd_prm, bsz, seq)
        kv_rows = [kv_rows_of(g, seq) for g in range(N_GROUPS)]
        new_conv = jnp.concatenate([conv_state, xbc3], axis=1)[:, -(CONV_W - 1):]
        moe_bm = 128
    new_ssm = h_last.reshape(bsz, d_inner // SSM_HEAD_DIM, SSM_HEAD_DIM, D_STATE)

    tmx = 512
    x1, u2, logits = _mixer(a, s, gates, xn, per_tile(gt1, tmx)[0], per_tile(sc2, tmx)[0], per_tile(sh2, tmx)[0],
                            wts, tm=tmx, tiles_per_mod=per_tile(gt1, tmx)[1], name="mixer")
    yk, top_gate = _moe(u2, logits, wts, bm=moe_bm)
    y = _final(x1, yk, top_gate, per_tile(gt2, tm)[0], wts["ln_ffn_g"], wts["ln_ffn_b"], tm=tm, tiles_per_mod=tpm,
               name="final_ln")
    return y.reshape(bsz, seq, d), kv_rows, new_conv, new_ssm


def kernel(x_prompt, x_sample, c_prompt, c_sample, cache_kv_w128, cache_kv_w512, cache_kv_w2048, state_conv, state_ssm, ln_emb_g, ln_emb_b, w_ada, b_ada, w_in, conv_w, conv_b, dt_bias, a_log, d_skip, ssm_norm_g, w_attn_br, w_ssd_br, w_out, ln_mix_g, ln_mix_b, w_router, b_router, w_gate, b_gate, w_up, b_up, w_down, b_down, ln_ffn_g, ln_ffn_b):
    d = x_prompt.shape[-1]
    d_inner = ssm_norm_g.shape[-1]
    d_xbc = conv_w.shape[-1]
    n_heads = dt_bias.shape[-1]
    lyr = 0

    def rowv(v):
        return v.reshape(1, -1).astype(f32)

    def lane_pad(v):
        return jnp.pad(v.astype(f32), (0, LANES - n_heads)).reshape(1, LANES)

    wi = w_in[lyr]
    o0 = QKV_W
    o1 = o0 + d_inner
    o2 = o1 + d_xbc
    o3 = o2 + n_heads
    wr = jnp.pad(w_router[lyr], ((0, 0), (0, LANES - N_EXPERTS)))
    wr_hi = wr.astype(bf16)
    wts = {
        "ln_emb_g": rowv(ln_emb_g), "ln_emb_b": rowv(ln_emb_b),
        "w_ada": w_ada[lyr].astype(bf16), "b_ada": rowv(b_ada[lyr]),
        "w_qkv": wi[:, :o0].astype(bf16), "w_z": wi[:, o0:o1].astype(bf16), "w_xbc": wi[:, o1:o2].astype(bf16),
        "w_dt": jnp.pad(wi[:, o2:o3], ((0, 0), (0, LANES - n_heads))).astype(bf16),
        "w_gates": wi[:, o3:].astype(bf16),
        "zero_b": jnp.zeros((1, max(QKV_W, d_xbc, 2 * d)), f32),
        "w_attn_br": w_attn_br[lyr].astype(bf16), "w_ssd_br": w_ssd_br[lyr].astype(bf16),
        "w_out": w_out[lyr].astype(bf16),
        "ln_mix_g": rowv(ln_mix_g[lyr]), "ln_mix_b": rowv(ln_mix_b[lyr]),
        "w_router_hi": wr_hi, "w_router_lo": (wr - wr_hi.astype(f32)).astype(bf16),
        "b_router": jnp.pad(b_router[lyr], (0, LANES - N_EXPERTS)).reshape(1, LANES),
        "w_gate": w_gate[lyr].astype(bf16), "b_gate": b_gate[lyr][:, None, :],
        "w_up": w_up[lyr].astype(bf16), "b_up": b_up[lyr][:, None, :],
        "w_down": w_down[lyr].astype(bf16), "b_down": b_down[lyr][:, None, :],
        "ln_ffn_g": rowv(ln_ffn_g[lyr]), "ln_ffn_b": rowv(ln_ffn_b[lyr]),
    }
    head_of_lane = jnp.arange(d_inner, dtype=jnp.int32) // SSM_HEAD_DIM
    ssd_prm = {
        "conv_w": conv_w[lyr], "conv_b": rowv(conv_b[lyr]),
        "dtb_row": lane_pad(dt_bias[lyr]), "alog_row": lane_pad(a_log[lyr]),
        "dtb_t": jnp.broadcast_to(dt_bias[lyr][:, None], (n_heads, LANES)),
        "alog_t": jnp.broadcast_to(a_log[lyr][:, None], (n_heads, LANES)),
        "dskip_x": jnp.repeat(d_skip[lyr], SSM_HEAD_DIM).reshape(1, d_inner),
        "norm_g": rowv(ssm_norm_g[lyr]),
        "ex": (jnp.arange(LANES, dtype=jnp.int32)[:, None] == head_of_lane[None, :]).astype(bf16),
        "rsel": (head_of_lane[:, None] == jnp.arange(LANES, dtype=jnp.int32)[None, :]).astype(bf16),
    }

    yp, kv_p, conv_p, ssm_p = _stream(x_prompt, c_prompt, wts, ssd_prm)
    ys, kv_s, conv_s, ssm_s = _stream(
        x_sample, c_sample, wts, ssd_prm,
        caches=(cache_kv_w128[lyr], cache_kv_w512[lyr], cache_kv_w2048[lyr]),
        conv_state=state_conv[lyr], ssm_state=state_ssm[lyr])
    return (yp, ys, kv_p[0][None], kv_s[0][None], kv_p[1][None], kv_s[1][None], kv_p[2][None], kv_s[2][None],
            conv_p[None], conv_s[None], ssm_p[None], ssm_s[None])
```

```python
import functools
import math

import jax
import jax.numpy as jnp
from jax import lax
from jax.experimental import pallas as pl
from jax.experimental.pallas import tpu as pltpu

f32 = jnp.float32
bf16 = jnp.bfloat16

ATTN_GROUPS = ((128, 1), (512, 4), (2048, 16))
N_GROUPS = len(ATTN_GROUPS)
N_HEADS = 8
HEAD_DIM = 64
GROUP_W = N_HEADS * HEAD_DIM
QKV_W = N_GROUPS * 3 * GROUP_W
BAND = 128
ATTN_SCALE = HEAD_DIM ** -0.5
SSM_HEAD_DIM = 64
SSM_GROUPS = 4
D_STATE = 128
CONV_W = 4
N_EXPERTS = 32
TOP_K = 4
SWIGLU_ALPHA = 1.702
SWIGLU_LIMIT = 7.0
LN_EPS = 1e-5
RMS_EPS = 1e-5
DEPTH = 1
DN_ALPHA = (2 * DEPTH) ** 0.25
NEG = -1e30

LANES = 128
SUBLANES = 8
VMEM_LIMIT = 56 * 1024 * 1024
MOE_BLOCK_ROWS = 256
DMA_ISSUE_UNROLL = 32


def _params(*sem):
    return pltpu.CompilerParams(dimension_semantics=sem, vmem_limit_bytes=VMEM_LIMIT)


def _silu(x):
    return x * jax.nn.sigmoid(x)


def _softplus(x):
    return jnp.maximum(x, 0.0) + jnp.log(1.0 + jnp.exp(-jnp.abs(x)))


def _split3(x):
    hi = x.astype(bf16)
    r1 = x - hi.astype(f32)
    mid = r1.astype(bf16)
    lo = (r1 - mid.astype(f32)).astype(bf16)
    return hi, mid, lo


def _dot(a, b):
    return jnp.dot(a, b, preferred_element_type=f32)


def _dot_nt(a, b):
    return lax.dot_general(a, b, (((1,), (1,)), ((), ())), preferred_element_type=f32)


def _dot_tn(a, b):
    return lax.dot_general(a, b, (((0,), (0,)), ((), ())), preferred_element_type=f32)


def _dot3_lhs(x, w):
    hi, mid, lo = _split3(x)
    return _dot(hi, w) + _dot(mid, w) + _dot(lo, w)


def _dot3_rhs(w, x):
    hi, mid, lo = _split3(x)
    return _dot(w, hi) + _dot(w, mid) + _dot(w, lo)


def _layer_norm(x, g, b):
    mu = jnp.mean(x, axis=-1, keepdims=True)
    xc = x - mu
    var = jnp.mean(xc * xc, axis=-1, keepdims=True)
    return xc * lax.rsqrt(var + LN_EPS) * g + b


def _store_row_tiles(ref, x, lead=()):
    rows, d = x.shape
    nc = d // LANES
    for c in range(nc):
        ref[lead + (pl.ds(c, rows, stride=nc), slice(None))] = x[:, c * LANES:(c + 1) * LANES]


def _load_row_tiles(ref, rows, nc, lead=()):
    return jnp.concatenate([ref[lead + (pl.ds(c, rows, stride=nc), slice(None))] for c in range(nc)], axis=-1)


def _mm_body(a_ref, w_ref, b_ref, o_ref, *, act):
    a = a_ref[...]
    if act == "silu":
        a = _silu(a.astype(f32))
    o_ref[...] = (_dot(a.astype(bf16), w_ref[...]) + b_ref[...]).astype(o_ref.dtype)


def _mm(a, w, bias, *, tm, tn, act=None, out_dtype=f32, name):
    m, k = a.shape
    n = w.shape[1]
    return pl.pallas_call(
        functools.partial(_mm_body, act=act),
        grid=(m // tm, n // tn),
        in_specs=[pl.BlockSpec((tm, k), lambda i, j: (i, 0)),
                  pl.BlockSpec((k, tn), lambda i, j: (0, j)),
                  pl.BlockSpec((1, tn), lambda i, j: (0, j))],
        out_specs=pl.BlockSpec((tm, tn), lambda i, j: (i, j)),
        out_shape=jax.ShapeDtypeStruct((m, n), out_dtype),
        compiler_params=_params("parallel", "arbitrary"),
        name=name,
    )(a, w, bias)


def _ln_mod_body(x_ref, g_ref, b_ref, sc_ref, sh_ref, xn_ref, u_ref):
    xn = _layer_norm(x_ref[...], g_ref[...], b_ref[...])
    xn_ref[...] = xn
    u_ref[...] = (xn * (1.0 + sc_ref[0]) + sh_ref[0]).astype(bf16)


def _mod_spec(mod3, tm, tiles_per_mod):
    return pl.BlockSpec((1, mod3.shape[1], mod3.shape[2]), lambda i: (i // tiles_per_mod, 0, 0))


def _ln_mod(x, g, b, sc, sh, *, tm, tiles_per_mod, name):
    n, d = x.shape
    row = pl.BlockSpec((tm, d), lambda i: (i, 0))
    vec = pl.BlockSpec((1, d), lambda i: (0, 0))
    return pl.pallas_call(
        _ln_mod_body,
        grid=(n // tm,),
        in_specs=[row, vec, vec, _mod_spec(sc, tm, tiles_per_mod), _mod_spec(sh, tm, tiles_per_mod)],
        out_specs=(row, row),
        out_shape=(jax.ShapeDtypeStruct((n, d), f32), jax.ShapeDtypeStruct((n, d), bf16)),
        compiler_params=_params("parallel"),
        name=name,
    )(x, g, b, sc, sh)


ATTN_SPAN = BAND * max(dil for _, dil in ATTN_GROUPS)
ATTN_LANES = 2 * HEAD_DIM


def _attn_prompt_fused_body(*refs):
    q_refs = refs[0:N_GROUPS]
    k_refs = refs[N_GROUPS:2 * N_GROUPS]
    v_refs = refs[2 * N_GROUPS:3 * N_GROUPS]
    a_ref = refs[3 * N_GROUPS]
    o_scr, l_scr = refs[3 * N_GROUPS + 1:]
    span0 = pl.program_id(2) * ATTN_SPAN
    qi = lax.broadcasted_iota(jnp.int32, (BAND, 2 * BAND), 0)
    kj = lax.broadcasted_iota(jnp.int32, (BAND, 2 * BAND), 1)
    dist = qi + BAND - kj
    band = (dist >= 0) & (dist <= BAND)
    heads = ATTN_LANES // HEAD_DIM
    head_of_lane = lax.broadcasted_iota(jnp.int32, (BAND, ATTN_LANES), 1) // HEAD_DIM
    for g, (_, dil) in enumerate(ATTN_GROUPS):
        n_blocks = ATTN_SPAN // (BAND * dil)

        def block(idx, carry, g=g, dil=dil, n_blocks=n_blocks):
            r = idx % dil
            start = r + (idx // dil) * (BAND * dil)
            cur = span0 + start
            prev = cur - BAND * dil
            has_prev = prev >= 0
            prev = jnp.maximum(prev, 0)
            rows = pl.ds(start, BAND, stride=dil)
            q = (q_refs[g][0, rows, :] * ATTN_SCALE).astype(bf16)
            k = jnp.concatenate([k_refs[g][0, pl.ds(prev, BAND, stride=dil), :],
                                 k_refs[g][0, pl.ds(cur, BAND, stride=dil), :]], axis=0).astype(bf16)
            v = jnp.concatenate([v_refs[g][0, pl.ds(prev, BAND, stride=dil), :],
                                 v_refs[g][0, pl.ds(cur, BAND, stride=dil), :]], axis=0).astype(bf16)
            mask = band & ((kj >= BAND) | has_prev)
            v1 = jnp.concatenate([v, jnp.ones_like(v)], axis=-1)
            out = lse = None
            for h in range(heads):
                mine = head_of_lane == h
                s = jnp.where(mask, _dot_nt(jnp.where(mine, q, jnp.zeros_like(q)), k), NEG)
                m = jnp.max(s, axis=-1, keepdims=True)
                pv = _dot(jnp.exp(s - m).astype(bf16), v1)
                den = pv[:, ATTN_LANES:]
                o_h = pv[:, :ATTN_LANES] / den
                l_h = m + jnp.log(den)
                out = o_h if out is None else jnp.where(mine, o_h, out)
                lse = l_h if lse is None else jnp.where(mine, l_h, lse)
            o_scr[g, rows, :] = out
            l_scr[g, rows, :] = lse
            return carry

        lax.fori_loop(0, n_blocks * dil, block, 0, unroll=4)
    ls = [l_scr[g] for g in range(N_GROUPS)]
    m = functools.reduce(jnp.maximum, ls)
    ws = [jnp.exp(l - m) for l in ls]
    num = sum(ws[g] * o_scr[g] for g in range(N_GROUPS))
    a_ref[...] = num / sum(ws)


def _attn_prompt(qkv, bsz, seq):
    assert seq % ATTN_SPAN == 0
    qkv3 = qkv.reshape(bsz, seq, QKV_W)
    n_spans = seq // ATTN_SPAN
    lane_blocks = GROUP_W // ATTN_LANES

    def col(g, which):
        return lambda b, hp, sp: (g * 3 + which) * lane_blocks + hp

    q_specs = [pl.BlockSpec((1, ATTN_SPAN, ATTN_LANES),
                            functools.partial(lambda b, hp, sp, c: (b, sp, c(b, hp, sp)), c=col(g, 0)))
               for g in range(N_GROUPS)]
    kv_specs = [pl.BlockSpec((1, seq, ATTN_LANES),
                             functools.partial(lambda b, hp, sp, c: (b, 0, c(b, hp, sp)), c=col(g, which)))
                for which in (1, 2) for g in range(N_GROUPS)]
    a = pl.pallas_call(
        _attn_prompt_fused_body,
        grid=(bsz, lane_blocks, n_spans),
        in_specs=q_specs + kv_specs,
        out_specs=pl.BlockSpec((ATTN_SPAN, ATTN_LANES), lambda b, hp, sp: (b * n_spans + sp, hp)),
        out_shape=jax.ShapeDtypeStruct((bsz * seq, GROUP_W), f32),
        scratch_shapes=[pltpu.VMEM((N_GROUPS, ATTN_SPAN, ATTN_LANES), f32),
                        pltpu.VMEM((N_GROUPS, ATTN_SPAN, ATTN_LANES), f32)],
        compiler_params=_params("parallel", "parallel", "arbitrary"),
        name="attn_prompt",
    )(*([qkv3] * (3 * N_GROUPS)))
    return a


def _attn_sample_body(qkv_ref, c0_ref, c1_ref, c2_ref, a_ref, *, n_new):
    caches = (c0_ref, c1_ref, c2_ref)
    qkv = qkv_ref[...]
    masks = []
    for window, dil in ATTN_GROUPS:
        s_i = lax.broadcasted_iota(jnp.int32, (n_new, window), 0)
        p_i = lax.broadcasted_iota(jnp.int32, (n_new, window), 1)
        masks.append((p_i >= s_i) & (((s_i - p_i) & (dil - 1)) == 0))
    s_n = lax.broadcasted_iota(jnp.int32, (n_new, n_new), 0)
    k_n = lax.broadcasted_iota(jnp.int32, (n_new, n_new), 1)
    new_masks = [(k_n <= s_n) & (((s_n - k_n) & (dil - 1)) == 0) for _, dil in ATTN_GROUPS]
    for h in range(N_HEADS):
        hs = slice(h * HEAD_DIM, (h + 1) * HEAD_DIM)
        scores, new_scores, new_vals = [], [], []
        for g in range(N_GROUPS):
            base = g * 3 * GROUP_W
            q = (qkv[:, base + h * HEAD_DIM:base + (h + 1) * HEAD_DIM] * ATTN_SCALE).astype(bf16)
            kn = qkv[:, base + GROUP_W + h * HEAD_DIM:base + GROUP_W + (h + 1) * HEAD_DIM].astype(bf16)
            new_vals.append(qkv[:, base + 2 * GROUP_W + h * HEAD_DIM:base + 2 * GROUP_W + (h + 1) * HEAD_DIM]
                            .astype(bf16))
            scores.append(jnp.where(masks[g], _dot(q, caches[g][0, 0, h].astype(bf16)), NEG))
            new_scores.append(jnp.where(new_masks[g], _dot_nt(q, kn), NEG))
        m = scores[0].max(axis=-1, keepdims=True)
        for sc in scores[1:] + new_scores:
            m = jnp.maximum(m, sc.max(axis=-1, keepdims=True))
        den = jnp.zeros((n_new, 1), f32)
        o = jnp.zeros((n_new, HEAD_DIM), f32)
        for g in range(N_GROUPS):
            p = jnp.exp(scores[g] - m)
            pn = jnp.exp(new_scores[g] - m)
            den = den + jnp.sum(p, axis=-1, keepdims=True) + jnp.sum(pn, axis=-1, keepdims=True)
            o = o + _dot_nt(p.astype(bf16), caches[g][0, 1, h].astype(bf16)) + _dot(pn.astype(bf16), new_vals[g])
        a_ref[:, hs] = o / den


def _attn_sample(qkv, caches, bsz, n_new):
    views, specs = [], []
    for g, (window, dil) in enumerate(ATTN_GROUPS):
        assert caches[g].shape[1] == window == BAND * dil and dil & (dil - 1) == 0
        views.append(jnp.transpose(caches[g], (0, 2, 3, 4, 1)))
        specs.append(pl.BlockSpec((1, 2, N_HEADS, HEAD_DIM, window), lambda b: (b, 0, 0, 0, 0)))
    return pl.pallas_call(
        functools.partial(_attn_sample_body, n_new=n_new),
        grid=(bsz,),
        in_specs=[pl.BlockSpec((n_new, QKV_W), lambda b: (b, 0))] + specs,
        out_specs=pl.BlockSpec((n_new, GROUP_W), lambda b: (b, 0)),
        out_shape=jax.ShapeDtypeStruct((bsz * n_new, GROUP_W), f32),
        compiler_params=_params("parallel"),
        name="attn_sample",
    )(qkv, *views)


def _ssd_conv(ext_ref, n, cw_ref, cb_ref):
    y = cb_ref[...]
    for j in range(CONV_W):
        off = SUBLANES - (CONV_W - 1) + j
        y = y + ext_ref[off:off + n, :] * cw_ref[j:j + 1, :]
    return _silu(y)


def _gate_norm(y, z, g, d_inner):
    y = y * _silu(z.astype(f32))
    gw = d_inner // SSM_GROUPS
    parts = []
    for gi in range(SSM_GROUPS):
        yg = y[:, gi * gw:(gi + 1) * gw]
        parts.append(yg * lax.rsqrt(jnp.mean(yg * yg, axis=-1, keepdims=True) + RMS_EPS))
    return jnp.concatenate(parts, axis=-1) * g


def _ssd_prompt_body(xbc_ref, z_ref, dt_ref, cw_ref, cb_ref, dtb_ref, alog_ref, dtbt_ref, alogt_ref,
                     dskip_ref, ng_ref, ex_ref, s_ref, hl_ref, ext_ref, ht_ref, y_ref, *, d_inner, n_heads):
    c = pl.program_id(1)
    q = xbc_ref.shape[0]
    gw = d_inner // SSM_GROUPS
    hpg = n_heads // SSM_GROUPS

    @pl.when(c == 0)
    def _():
        ext_ref[0:SUBLANES, :] = jnp.zeros((SUBLANES, ext_ref.shape[1]), f32)
        ht_ref[...] = jnp.zeros_like(ht_ref)

    @pl.when(c > 0)
    def _():
        ext_ref[0:SUBLANES, :] = ext_ref[q:q + SUBLANES, :]

    ext_ref[SUBLANES:SUBLANES + q, :] = xbc_ref[...]
    xc = _ssd_conv(ext_ref, q, cw_ref, cb_ref)
    xs = xc[:, :d_inner]
    gn = SSM_GROUPS * D_STATE
    bm = xc[:, d_inner:d_inner + gn].astype(bf16)
    cm = xc[:, d_inner + gn:].astype(bf16)

    dtr = dt_ref[...]
    dt = _softplus(dtr + dtb_ref[...])
    da = dt * (-jnp.exp(alog_ref[...]))
    dt_t = _softplus(dtr.T[0:n_heads, :] + dtbt_ref[...])
    da_t = dt_t * (-jnp.exp(alogt_ref[...]))
    ii = lax.broadcasted_iota(jnp.int32, (q, q), 0)
    jj = lax.broadcasted_iota(jnp.int32, (q, q), 1)
    causal = ii >= jj
    lower = causal.astype(bf16)
    upper = (ii <= jj).astype(bf16)
    cum = _dot3_rhs(lower, da)
    cum_t = _dot3_lhs(da_t, upper)
    ex = ex_ref[...]
    cumx = _dot3_lhs(cum, ex)
    dtx = _dot3_lhs(dt, ex)
    clx = cumx[q - 1:q, :]
    ecum = jnp.exp(cumx)
    xd = (jnp.exp(clx - cumx) * dtx * xs).astype(bf16)
    xsb = xs.astype(bf16)

    for g in range(SSM_GROUPS):
        gl = slice(g * gw, (g + 1) * gw)
        cg = cm[:, g * D_STATE:(g + 1) * D_STATE]
        bg = bm[:, g * D_STATE:(g + 1) * D_STATE]
        cb = _dot_nt(cg, bg)
        h_old = ht_ref[g]
        y_ref[:, gl] = ecum[:, gl] * _dot(cg, h_old.astype(bf16))
        ht_ref[g] = jnp.exp(clx[:, gl]) * h_old + _dot_tn(bg, xd[:, gl])
        for e in range(hpg):
            hd = g * hpg + e
            hl = slice(hd * SSM_HEAD_DIM, (hd + 1) * SSM_HEAD_DIM)
            seg = jnp.broadcast_to(cum[:, hd:hd + 1], (q, q)) - jnp.broadcast_to(cum_t[hd:hd + 1, :], (q, q))
            w = cb * jnp.exp(jnp.where(causal, seg, NEG)) * jnp.broadcast_to(dt_t[hd:hd + 1, :], (q, q))
            y_ref[:, hl] += _dot(w.astype(bf16), xsb[:, hl])

    y = y_ref[...] + dskip_ref[...] * xs
    s_ref[...] = _gate_norm(y, z_ref[...], ng_ref[...], d_inner).astype(s_ref.dtype)

    @pl.when(c == pl.num_programs(1) - 1)
    def _():
        for g in range(SSM_GROUPS):
            hl_ref[0, g * gw:(g + 1) * gw, :] = ht_ref[g].T


def _ssd_prompt(xbc, z, dt, prm, bsz, seq, *, chunk):
    n, d_xbc = xbc.shape
    d_inner = z.shape[1]
    n_heads = d_inner // SSM_HEAD_DIM
    nc = seq // chunk
    gw = d_inner // SSM_GROUPS

    def row(w):
        return pl.BlockSpec((chunk, w), lambda b, c: (b * nc + c, 0))

    def full(a):
        return pl.BlockSpec(a.shape, lambda b, c: (0,) * a.ndim)

    consts = [prm["conv_w"], prm["conv_b"], prm["dtb_row"], prm["alog_row"], prm["dtb_t"], prm["alog_t"],
              prm["dskip_x"], prm["norm_g"], prm["ex"]]
    return pl.pallas_call(
        functools.partial(_ssd_prompt_body, d_inner=d_inner, n_heads=n_heads),
        grid=(bsz, nc),
        in_specs=[row(d_xbc), row(d_inner), row(LANES)] + [full(a) for a in consts],
        out_specs=(row(d_inner), pl.BlockSpec((1, d_inner, D_STATE), lambda b, c: (b, 0, 0))),
        out_shape=(jax.ShapeDtypeStruct((n, d_inner), bf16),
                   jax.ShapeDtypeStruct((bsz, d_inner, D_STATE), f32)),
        scratch_shapes=[pltpu.VMEM((SUBLANES + chunk + SUBLANES, d_xbc), f32),
                        pltpu.VMEM((SSM_GROUPS, D_STATE, gw), f32),
                        pltpu.VMEM((chunk, d_inner), f32)],
        compiler_params=_params("parallel", "arbitrary"),
        name="ssd_prompt",
    )(xbc, z, dt, *consts)


def _ssd_sample_body(xbc_ref, cs_ref, z_ref, dt_ref, h_ref, cw_ref, cb_ref, dtb_ref, alog_ref,
                     dskip_ref, ng_ref, ex_ref, rsel_ref, s_ref, hn_ref, ext_ref, *, d_inner, n_heads):
    q = xbc_ref.shape[0]
    gw = d_inner // SSM_GROUPS
    ext_ref[0:SUBLANES, :] = cs_ref[0]
    ext_ref[SUBLANES:SUBLANES + q, :] = xbc_ref[...]
    xc = _ssd_conv(ext_ref, q, cw_ref, cb_ref)
    xs = xc[:, :d_inner]
    gn = SSM_GROUPS * D_STATE
    bm = xc[:, d_inner:d_inner + gn]
    cm = xc[:, d_inner + gn:]

    dt = _softplus(dt_ref[...] + dtb_ref[...])
    da = dt * (-jnp.exp(alog_ref[...]))
    row = lax.broadcasted_iota(jnp.int32, (q, LANES), 0)
    cum = jnp.zeros((q, LANES), f32)
    for j in range(q):
        cum = cum + jnp.where(row >= j, da[j:j + 1, :], 0.0)
    ex = ex_ref[...]
    cumx = _dot3_lhs(cum, ex)
    dtx = _dot3_lhs(dt, ex)
    clx = cumx[q - 1:q, :]
    rowx = lax.broadcasted_iota(jnp.int32, (q, d_inner), 0)

    y = dskip_ref[...] * xs
    for j in range(q):
        prod = cm * bm[j:j + 1, :]
        cbx = jnp.concatenate(
            [jnp.broadcast_to(jnp.sum(prod[:, g * D_STATE:(g + 1) * D_STATE], axis=-1, keepdims=True), (q, gw))
             for g in range(SSM_GROUPS)], axis=-1)
        seg = jnp.where(rowx >= j, cumx - cumx[j:j + 1, :], NEG)
        y = y + cbx * jnp.exp(seg) * (dtx[j:j + 1, :] * xs[j:j + 1, :])

    xd = (jnp.exp(clx - cumx) * dtx * xs).astype(bf16)
    ones = jnp.ones((q, LANES), bf16)
    da_hi, da_mid, da_lo = _split3(da)
    cl_b = _dot_tn(da_hi, ones) + _dot_tn(da_mid, ones) + _dot_tn(da_lo, ones)
    decay = jnp.exp(_dot3_rhs(rsel_ref[...], cl_b))
    ecum = jnp.exp(cumx)
    cmb = cm.astype(bf16)
    bmb = bm.astype(bf16)
    ys = []
    for g in range(SSM_GROUPS):
        rows = slice(g * gw, (g + 1) * gw)
        hg = h_ref[0, rows, :]
        ys.append(_dot_nt(cmb[:, g * D_STATE:(g + 1) * D_STATE], hg.astype(bf16)))
        hn_ref[0, rows, :] = decay[rows, :] * hg + _dot_tn(xd[:, rows], bmb[:, g * D_STATE:(g + 1) * D_STATE])
    y = y + ecum * jnp.concatenate(ys, axis=-1)
    s_ref[...] = _gate_norm(y, z_ref[...], ng_ref[...], d_inner)


def _ssd_sample(xbc, conv_state8, z, dt, h0, prm, bsz, n_new):
    n, d_xbc = xbc.shape
    d_inner = z.shape[1]
    n_heads = d_inner // SSM_HEAD_DIM

    def row(w):
        return pl.BlockSpec((n_new, w), lambda b: (b, 0))

    def full(a):
        return pl.BlockSpec(a.shape, lambda b: (0,) * a.ndim)

    consts = [prm["conv_w"], prm["conv_b"], prm["dtb_row"], prm["alog_row"],
              prm["dskip_x"], prm["norm_g"], prm["ex"], prm["rsel"]]
    state = pl.BlockSpec((1, d_inner, D_STATE), lambda b: (b, 0, 0))
    return pl.pallas_call(
        functools.partial(_ssd_sample_body, d_inner=d_inner, n_heads=n_heads),
        grid=(bsz,),
        in_specs=[row(d_xbc), pl.BlockSpec((1, SUBLANES, d_xbc), lambda b: (b, 0, 0)), row(d_inner), row(LANES),
                  state] + [full(a) for a in consts],
        out_specs=(row(d_inner), state),
        out_shape=(jax.ShapeDtypeStruct((n, d_inner), f32),
                   jax.ShapeDtypeStruct((bsz, d_inner, D_STATE), f32)),
        scratch_shapes=[pltpu.VMEM((2 * SUBLANES, d_xbc), f32)],
        compiler_params=_params("parallel"),
        name="ssd_sample",
    )(xbc, conv_state8, z, dt, h0, *consts)


def _mixer_body(a_ref, s_ref, ga_ref, gs_ref, xn_ref, gt_ref, sc_ref, sh_ref, wa_ref, ws_ref, wo_ref,
                lg_ref, lb_ref, wrh_ref, wrl_ref, br_ref, x1_ref, u2_ref, lo_ref):
    m = (jax.nn.sigmoid(ga_ref[...].astype(f32)) * _dot(a_ref[...].astype(bf16), wa_ref[...])
         + jax.nn.sigmoid(gs_ref[...].astype(f32)) * _dot(s_ref[...].astype(bf16), ws_ref[...]))
    o = _dot(m.astype(bf16), wo_ref[...])
    x1 = _layer_norm(DN_ALPHA * xn_ref[...] + gt_ref[0] * o, lg_ref[...], lb_ref[...])
    x1_ref[...] = x1
    u2 = x1 * (1.0 + sc_ref[0]) + sh_ref[0]
    _store_row_tiles(u2_ref, u2)
    hi, mid, lo = _split3(u2)
    wrh = wrh_ref[...]
    wrl = wrl_ref[...]
    lo_ref[...] = (_dot(hi, wrh) + (_dot(hi, wrl) + _dot(mid, wrh)) + (_dot(mid, wrl) + _dot(lo, wrh))
                   + br_ref[...])


def _mixer(a, s, gates, xn, gt, sc, sh, wts, *, tm, tiles_per_mod, name):
    n, d = xn.shape

    def row(w, col=0):
        return pl.BlockSpec((tm, w), lambda i: (i, col))

    def full(arr):
        return pl.BlockSpec(arr.shape, lambda i: (0,) * arr.ndim)

    consts = [wts["w_attn_br"], wts["w_ssd_br"], wts["w_out"], wts["ln_mix_g"], wts["ln_mix_b"],
              wts["w_router_hi"], wts["w_router_lo"], wts["b_router"]]
    return pl.pallas_call(
        _mixer_body,
        grid=(n // tm,),
        in_specs=[row(a.shape[1]), row(s.shape[1]), row(d, 0), row(d, 1), row(d),
                  _mod_spec(gt, tm, tiles_per_mod), _mod_spec(sc, tm, tiles_per_mod),
                  _mod_spec(sh, tm, tiles_per_mod)] + [full(c) for c in consts],
        out_specs=(row(d), pl.BlockSpec((tm * (d // LANES), LANES), lambda i: (i, 0)), row(LANES)),
        out_shape=(jax.ShapeDtypeStruct((n, d), f32), jax.ShapeDtypeStruct((n * (d // LANES), LANES), f32),
                   jax.ShapeDtypeStruct((n, LANES), f32)),
        compiler_params=_params("parallel"),
        name=name,
    )(a, s, gates, gates, xn, gt, sc, sh, *consts)


def _moe_body(be_ref, nv_ref, nr_ref, tok_ref, tok_next_ref, slot_ref, x_hbm, wg_ref, bg_ref, wu_ref, bu_ref, wd_ref,
              bd_ref, y_hbm, xbuf, ybuf, wgb, wub, wdb, sem_in, sem_out, *, bm):
    i = pl.program_id(0)
    nv = nv_ref[0]
    nc = xbuf.shape[1] // bm
    cur = i % 2
    nxt = 1 - cur

    def start_gather(idx_ref, buf):
        def body(r, carry):
            src = pl.multiple_of(idx_ref[0, 0, r], nc)
            pltpu.make_async_copy(x_hbm.at[pl.ds(src, nc), :], xbuf.at[buf, pl.ds(r * nc, nc), :],
                                  sem_in.at[buf]).start()
            return carry
        lax.fori_loop(0, bm, body, 0, unroll=DMA_ISSUE_UNROLL)

    def wait_gather(buf):
        pltpu.make_async_copy(x_hbm.at[pl.ds(0, bm * nc), :], xbuf.at[buf], sem_in.at[buf]).wait()

    def scatter_row(buf, r, slot):
        dst = pl.multiple_of(slot, nc)
        return pltpu.make_async_copy(ybuf.at[buf, pl.ds(r * nc, nc), :], y_hbm.at[pl.ds(dst, nc), :], sem_out.at[buf])

    def start_scatter(buf, n_real):
        def body(r, carry):
            scatter_row(buf, r, slot_ref[0, 0, r]).start()
            return carry

        @pl.when(n_real == bm)
        def _():
            lax.fori_loop(0, bm, body, 0, unroll=DMA_ISSUE_UNROLL)

        @pl.when(n_real < bm)
        def _():
            lax.fori_loop(0, n_real, body, 0)

    def wait_scatter(buf, n_real):
        @pl.when(n_real == bm)
        def _():
            pltpu.make_async_copy(ybuf.at[buf], y_hbm.at[pl.ds(0, bm * nc), :], sem_out.at[buf]).wait()

        @pl.when(n_real < bm)
        def _():
            def body(r, carry):
                scatter_row(buf, 0, 0).wait()
                return carry
            lax.fori_loop(0, n_real, body, 0)

    @pl.when((i == 0) & (nv > 0))
    def _():
        start_gather(tok_ref, cur)

    @pl.when(i + 1 < nv)
    def _():
        start_gather(tok_next_ref, nxt)

    @pl.when(i < nv)
    def _():
        @pl.when((i == 0) | (be_ref[i] != be_ref[jnp.maximum(i - 1, 0)]))
        def _():
            wgb[...] = wg_ref[0].astype(bf16)
            wub[...] = wu_ref[0].astype(bf16)
            wdb[...] = wd_ref[0].astype(bf16)

        wait_gather(cur)
        xb = _load_row_tiles(xbuf, bm, nc, (cur,)).astype(bf16)
        gate = jnp.minimum(_dot(xb, wgb[...]) + bg_ref[0], SWIGLU_LIMIT)
        up = jnp.clip(_dot(xb, wub[...]) + bu_ref[0], -SWIGLU_LIMIT, SWIGLU_LIMIT)
        h = gate * jax.nn.sigmoid(SWIGLU_ALPHA * gate) * (up + 1.0)
        y = _dot(h.astype(bf16), wdb[...]) + bd_ref[0]

        @pl.when(i >= 1)
        def _():
            wait_scatter(nxt, nr_ref[jnp.maximum(i - 1, 0)])

        _store_row_tiles(ybuf, y, (cur,))
        start_scatter(cur, nr_ref[i])

        @pl.when(i == nv - 1)
        def _():
            wait_scatter(cur, nr_ref[i])


def _moe(u2, logits, wts, *, bm):
    d_ff, d = wts["w_down"].shape[1:]
    nc = d // LANES
    n_tok = u2.shape[0] // nc
    n_assign = n_tok * TOP_K
    i32 = jnp.int32
    top_logit, top_idx = lax.top_k(logits[:, :N_EXPERTS], TOP_K)
    top_gate = jax.nn.softmax(top_logit, axis=-1)
    e_flat = top_idx.reshape(-1)
    order = jnp.argsort(e_flat).astype(i32)
    experts = jnp.arange(N_EXPERTS, dtype=i32)
    counts = jnp.sum((e_flat[:, None] == experts[None, :]).astype(i32), axis=0)
    starts = jnp.cumsum(counts) - counts
    padded = (counts + bm - 1) // bm * bm
    pad_ends = jnp.cumsum(padded)
    pad_starts = pad_ends - padded
    n_blocks = -(-n_assign // bm) + N_EXPERTS
    blk_row = jnp.arange(n_blocks, dtype=i32) * bm
    block_expert = jnp.minimum(jnp.sum((pad_ends[None, :] <= blk_row[:, None]).astype(i32), axis=1), N_EXPERTS - 1)
    off = blk_row - pad_starts[block_expert]
    n_real = jnp.clip(counts[block_expert] - off, 0, bm)
    j = jnp.arange(bm, dtype=i32)[None, :]
    src = jnp.clip(starts[block_expert][:, None] + off[:, None] + j, 0, n_assign - 1)
    a_id = jnp.take(order, src.reshape(-1), axis=0).reshape(n_blocks, bm)
    real = j < n_real[:, None]
    row_tok = jnp.where(real, a_id // TOP_K, 0) * nc
    row_slot = jnp.where(real, (a_id % TOP_K) * n_tok + a_id // TOP_K, 0) * nc
    n_valid = (pad_ends[-1] // bm).astype(i32).reshape(1)

    def wspec(shape):
        return pl.BlockSpec((1,) + shape, lambda i, be, nv, nr: (be[i], 0, 0))

    idx_spec = pl.BlockSpec((1, 1, bm), lambda i, be, nv, nr: (i, 0, 0), memory_space=pltpu.SMEM)
    idx_next = pl.BlockSpec((1, 1, bm), lambda i, be, nv, nr: (jnp.minimum(i + 1, n_blocks - 1), 0, 0),
                            memory_space=pltpu.SMEM)
    grid_spec = pltpu.PrefetchScalarGridSpec(
        num_scalar_prefetch=3,
        grid=(n_blocks,),
        in_specs=[idx_spec, idx_next, idx_spec,
                  pl.BlockSpec(memory_space=pl.ANY),
                  wspec((d, d_ff)), wspec((1, d_ff)), wspec((d, d_ff)), wspec((1, d_ff)),
                  wspec((d_ff, d)), wspec((1, d))],
        out_specs=pl.BlockSpec(memory_space=pl.ANY),
        scratch_shapes=[pltpu.VMEM((2, bm * nc, LANES), f32), pltpu.VMEM((2, bm * nc, LANES), f32),
                        pltpu.VMEM((d, d_ff), bf16), pltpu.VMEM((d, d_ff), bf16), pltpu.VMEM((d_ff, d), bf16),
                        pltpu.SemaphoreType.DMA((2,)), pltpu.SemaphoreType.DMA((2,))],
    )
    row_tok3 = row_tok.reshape(n_blocks, 1, bm)
    y = pl.pallas_call(
        functools.partial(_moe_body, bm=bm),
        grid_spec=grid_spec,
        out_shape=jax.ShapeDtypeStruct((n_assign * nc, LANES), f32),
        compiler_params=_params("arbitrary"),
        name=f"moe_bm{bm}",
    )(block_expert, n_valid, n_real, row_tok3, row_tok3, row_slot.reshape(n_blocks, 1, bm), u2,
      wts["w_gate"], wts["b_gate"], wts["w_up"], wts["b_up"], wts["w_down"], wts["b_down"])
    return y, top_gate


def _final_body(x1_ref, *rest):
    yk_refs, (pg_ref, gt_ref, g_ref, b_ref, o_ref) = rest[:TOP_K], rest[TOP_K:]
    pg = pg_ref[...]
    tm, d = x1_ref.shape
    y = pg[:, 0:1] * _load_row_tiles(yk_refs[0], tm, d // LANES)
    for k in range(1, TOP_K):
        y = y + pg[:, k:k + 1] * _load_row_tiles(yk_refs[k], tm, d // LANES)
    o_ref[...] = _layer_norm(DN_ALPHA * x1_ref[...] + gt_ref[0] * y, g_ref[...], b_ref[...])


def _final(x1, yk, top_gate, gt, g, b, *, tm, tiles_per_mod, row0, n_all, name):
    n, d = x1.shape
    nt = n // tm
    assert row0 % tm == 0 and n_all % tm == 0
    nt_all, blk0 = n_all // tm, row0 // tm
    row = pl.BlockSpec((tm, d), lambda i: (i, 0))
    vec = pl.BlockSpec((1, d), lambda i: (0, 0))
    planes = [pl.BlockSpec((tm * (d // LANES), LANES),
                           functools.partial(lambda i, k: (k * nt_all + blk0 + i, 0), k=k)) for k in range(TOP_K)]
    return pl.pallas_call(
        _final_body,
        grid=(nt,),
        in_specs=[row] + planes + [pl.BlockSpec((tm, TOP_K), lambda i: (i, 0)),
                                   _mod_spec(gt, tm, tiles_per_mod), vec, vec],
        out_specs=row,
        out_shape=jax.ShapeDtypeStruct((n, d), f32),
        compiler_params=_params("parallel"),
        name=name,
    )(x1, *([yk] * TOP_K), top_gate, gt, g, b)


def _stream(x, c, wts, ssd_prm, *, caches=None, conv_state=None, ssm_state=None):
    bsz, seq, d = x.shape
    n = bsz * seq
    is_prompt = caches is None
    d_inner = wts["w_z"].shape[1]
    d_xbc = wts["w_xbc"].shape[1]

    c_pad = jnp.pad(c, ((0, (-bsz) % SUBLANES), (0, 0)))
    mod = _mm(c_pad, wts["w_ada"], wts["b_ada"], tm=c_pad.shape[0], tn=2 * d, act="silu", name="ada_mod")[:bsz]
    sh1, sc1, gt1, sh2, sc2, gt2 = jnp.split(mod, 6, axis=-1)

    def per_tile(p, tile):
        if seq % tile == 0:
            return p.reshape(bsz, 1, d), seq // tile
        return jnp.broadcast_to(p[:, None, :], (bsz, seq, d)).reshape(n // tile, tile, d), 1

    tm = 512
    tpm = per_tile(sc1, tm)[1]
    xn, u = _ln_mod(x.reshape(n, d), wts["ln_emb_g"], wts["ln_emb_b"], per_tile(sc1, tm)[0], per_tile(sh1, tm)[0],
                    tm=tm, tiles_per_mod=tpm, name="ln_mod")
    tmm = min(n, 2048)
    act_dtype = bf16 if seq % 16 == 0 else f32
    qkv = _mm(u, wts["w_qkv"], wts["zero_b"][:, :QKV_W], tm=tmm, tn=512, name="proj_qkv")
    z = _mm(u, wts["w_z"], wts["zero_b"][:, :d_inner], tm=tmm, tn=512, out_dtype=act_dtype, name="proj_z")
    xbc = _mm(u, wts["w_xbc"], wts["zero_b"][:, :d_xbc], tm=tmm, tn=512, name="proj_xbc")
    dt = _mm(u, wts["w_dt"], wts["zero_b"][:, :LANES], tm=tmm, tn=LANES, name="proj_dt")
    gates = _mm(u, wts["w_gates"], wts["zero_b"][:, :2 * d], tm=tmm, tn=512, out_dtype=bf16, name="proj_gates")

    qkv3 = qkv.reshape(bsz, seq, QKV_W)
    xbc3 = xbc.reshape(bsz, seq, d_xbc)

    def kv_rows_of(g, keep):
        cols = qkv3[:, seq - keep:, (g * 3 + 1) * GROUP_W:(g * 3 + 3) * GROUP_W]
        return cols.reshape(bsz, keep, 2, N_HEADS, HEAD_DIM)

    if is_prompt:
        a = _attn_prompt(qkv, bsz, seq)
        s, h_last = _ssd_prompt(xbc, z, dt, ssd_prm, bsz, seq, chunk=128)
        kv_rows = [kv_rows_of(g, min(w, seq)) for g, (w, _) in enumerate(ATTN_GROUPS)]
        new_conv = xbc3[:, -(CONV_W - 1):]
    else:
        a = _attn_sample(qkv, caches, bsz, seq)
        cs8 = jnp.pad(conv_state, ((0, 0), (SUBLANES - (CONV_W - 1), 0), (0, 0)))
        s, h_last = _ssd_sample(xbc, cs8, z, dt, ssm_state.reshape(bsz, d_inner, D_STATE), ssd_prm, bsz, seq)
        kv_rows = [kv_rows_of(g, seq) for g in range(N_GROUPS)]
        new_conv = jnp.concatenate([conv_state, xbc3], axis=1)[:, -(CONV_W - 1):]
    new_ssm = h_last.reshape(bsz, d_inner // SSM_HEAD_DIM, SSM_HEAD_DIM, D_STATE)

    tmx = 512
    x1, u2, logits = _mixer(a, s, gates, xn, per_tile(gt1, tmx)[0], per_tile(sc2, tmx)[0], per_tile(sh2, tmx)[0],
                            wts, tm=tmx, tiles_per_mod=per_tile(gt1, tmx)[1], name="mixer")

    def finish(yk, top_gate, row0, n_all):
        y = _final(x1, yk, top_gate[row0:row0 + n], per_tile(gt2, tm)[0], wts["ln_ffn_g"], wts["ln_ffn_b"], tm=tm,
                   tiles_per_mod=tpm, row0=row0, n_all=n_all, name="final_ln")
        return y.reshape(bsz, seq, d)

    return (u2, logits, finish), kv_rows, new_conv, new_ssm


def kernel(x_prompt, x_sample, c_prompt, c_sample, cache_kv_w128, cache_kv_w512, cache_kv_w2048, state_conv, state_ssm, ln_emb_g, ln_emb_b, w_ada, b_ada, w_in, conv_w, conv_b, dt_bias, a_log, d_skip, ssm_norm_g, w_attn_br, w_ssd_br, w_out, ln_mix_g, ln_mix_b, w_router, b_router, w_gate, b_gate, w_up, b_up, w_down, b_down, ln_ffn_g, ln_ffn_b):
    d = x_prompt.shape[-1]
    d_inner = ssm_norm_g.shape[-1]
    d_xbc = conv_w.shape[-1]
    n_heads = dt_bias.shape[-1]
    lyr = 0

    def rowv(v):
        return v.reshape(1, -1).astype(f32)

    def lane_pad(v):
        return jnp.pad(v.astype(f32), (0, LANES - n_heads)).reshape(1, LANES)

    wi = w_in[lyr]
    o0 = QKV_W
    o1 = o0 + d_inner
    o2 = o1 + d_xbc
    o3 = o2 + n_heads
    wr = jnp.pad(w_router[lyr], ((0, 0), (0, LANES - N_EXPERTS)))
    wr_hi = wr.astype(bf16)
    wts = {
        "ln_emb_g": rowv(ln_emb_g), "ln_emb_b": rowv(ln_emb_b),
        "w_ada": w_ada[lyr].astype(bf16), "b_ada": rowv(b_ada[lyr]),
        "w_qkv": wi[:, :o0].astype(bf16), "w_z": wi[:, o0:o1].astype(bf16), "w_xbc": wi[:, o1:o2].astype(bf16),
        "w_dt": jnp.pad(wi[:, o2:o3], ((0, 0), (0, LANES - n_heads))).astype(bf16),
        "w_gates": wi[:, o3:].astype(bf16),
        "zero_b": jnp.zeros((1, max(QKV_W, d_xbc, 2 * d)), f32),
        "w_attn_br": w_attn_br[lyr].astype(bf16), "w_ssd_br": w_ssd_br[lyr].astype(bf16),
        "w_out": w_out[lyr].astype(bf16),
        "ln_mix_g": rowv(ln_mix_g[lyr]), "ln_mix_b": rowv(ln_mix_b[lyr]),
        "w_router_hi": wr_hi, "w_router_lo": (wr - wr_hi.astype(f32)).astype(bf16),
        "b_router": jnp.pad(b_router[lyr], (0, LANES - N_EXPERTS)).reshape(1, LANES),
        "w_gate": w_gate[lyr], "b_gate": b_gate[lyr][:, None, :],
        "w_up": w_up[lyr], "b_up": b_up[lyr][:, None, :],
        "w_down": w_down[lyr], "b_down": b_down[lyr][:, None, :],
        "ln_ffn_g": rowv(ln_ffn_g[lyr]), "ln_ffn_b": rowv(ln_ffn_b[lyr]),
    }
    head_of_lane = jnp.arange(d_inner, dtype=jnp.int32) // SSM_HEAD_DIM
    ssd_prm = {
        "conv_w": conv_w[lyr], "conv_b": rowv(conv_b[lyr]),
        "dtb_row": lane_pad(dt_bias[lyr]), "alog_row": lane_pad(a_log[lyr]),
        "dtb_t": jnp.broadcast_to(dt_bias[lyr][:, None], (n_heads, LANES)),
        "alog_t": jnp.broadcast_to(a_log[lyr][:, None], (n_heads, LANES)),
        "dskip_x": jnp.repeat(d_skip[lyr], SSM_HEAD_DIM).reshape(1, d_inner),
        "norm_g": rowv(ssm_norm_g[lyr]),
        "ex": (jnp.arange(LANES, dtype=jnp.int32)[:, None] == head_of_lane[None, :]).astype(bf16),
        "rsel": (head_of_lane[:, None] == jnp.arange(LANES, dtype=jnp.int32)[None, :]).astype(bf16),
    }

    (u2_p, logits_p, finish_p), kv_p, conv_p, ssm_p = _stream(x_prompt, c_prompt, wts, ssd_prm)
    (u2_s, logits_s, finish_s), kv_s, conv_s, ssm_s = _stream(
        x_sample, c_sample, wts, ssd_prm,
        caches=(cache_kv_w128[lyr], cache_kv_w512[lyr], cache_kv_w2048[lyr]),
        conv_state=state_conv[lyr], ssm_state=state_ssm[lyr])
    n_p, n_s = logits_p.shape[0], logits_s.shape[0]
    yk, top_gate = _moe(jnp.concatenate([u2_p, u2_s], axis=0), jnp.concatenate([logits_p, logits_s], axis=0), wts,
                        bm=MOE_BLOCK_ROWS)
    yp = finish_p(yk, top_gate, 0, n_p + n_s)
    ys = finish_s(yk, top_gate, n_p, n_p + n_s)
    return (yp, ys, kv_p[0][None], kv_s[0][None], kv_p[1][None], kv_s[1][None], kv_p[2][None], kv_s[2][None],
            conv_p[None], conv_s[None], ssm_p[None], ssm_s[None])
```

```python
import functools
import math

import jax
import jax.numpy as jnp
from jax import lax
from jax.experimental import pallas as pl
from jax.experimental.pallas import tpu as pltpu

f32 = jnp.float32
bf16 = jnp.bfloat16

ATTN_GROUPS = ((128, 1), (512, 4), (2048, 16))
N_GROUPS = len(ATTN_GROUPS)
N_HEADS = 8
HEAD_DIM = 64
GROUP_W = N_HEADS * HEAD_DIM
QKV_W = N_GROUPS * 3 * GROUP_W
BAND = 128
ATTN_SCALE = HEAD_DIM ** -0.5
SSM_HEAD_DIM = 64
SSM_GROUPS = 4
D_STATE = 128
CONV_W = 4
N_EXPERTS = 32
TOP_K = 4
SWIGLU_ALPHA = 1.702
SWIGLU_LIMIT = 7.0
LN_EPS = 1e-5
RMS_EPS = 1e-5
DEPTH = 1
DN_ALPHA = (2 * DEPTH) ** 0.25
NEG = -1e30

LANES = 128
SUBLANES = 8
VMEM_LIMIT = 56 * 1024 * 1024
MOE_BLOCK_ROWS = 256
DMA_ISSUE_UNROLL = 32


def _params(*sem):
    return pltpu.CompilerParams(dimension_semantics=sem, vmem_limit_bytes=VMEM_LIMIT)


def _silu(x):
    return x * jax.nn.sigmoid(x)


def _softplus(x):
    return jnp.maximum(x, 0.0) + jnp.log(1.0 + jnp.exp(-jnp.abs(x)))


def _split3(x):
    hi = x.astype(bf16)
    r1 = x - hi.astype(f32)
    mid = r1.astype(bf16)
    lo = (r1 - mid.astype(f32)).astype(bf16)
    return hi, mid, lo


def _dot(a, b):
    return jnp.dot(a, b, preferred_element_type=f32)


def _dot_nt(a, b):
    return lax.dot_general(a, b, (((1,), (1,)), ((), ())), preferred_element_type=f32)


def _dot_tn(a, b):
    return lax.dot_general(a, b, (((0,), (0,)), ((), ())), preferred_element_type=f32)


def _dot3_lhs(x, w):
    hi, mid, lo = _split3(x)
    return _dot(hi, w) + _dot(mid, w) + _dot(lo, w)


def _dot3_rhs(w, x):
    hi, mid, lo = _split3(x)
    return _dot(w, hi) + _dot(w, mid) + _dot(w, lo)


def _layer_norm(x, g, b):
    mu = jnp.mean(x, axis=-1, keepdims=True)
    xc = x - mu
    var = jnp.mean(xc * xc, axis=-1, keepdims=True)
    return xc * lax.rsqrt(var + LN_EPS) * g + b


def _store_row_tiles(ref, x, lead=()):
    rows, d = x.shape
    nc = d // LANES
    for c in range(nc):
        ref[lead + (pl.ds(c, rows, stride=nc), slice(None))] = x[:, c * LANES:(c + 1) * LANES]


def _load_row_tiles(ref, rows, nc, lead=()):
    return jnp.concatenate([ref[lead + (pl.ds(c, rows, stride=nc), slice(None))] for c in range(nc)], axis=-1)


def _mm_body(a_ref, w_ref, b_ref, o_ref, *, act):
    a = a_ref[...]
    if act == "silu":
        a = _silu(a.astype(f32))
    o_ref[...] = (_dot(a.astype(bf16), w_ref[...]) + b_ref[...]).astype(o_ref.dtype)


def _mm(a, w, bias, *, tm, tn, act=None, out_dtype=f32, name):
    m, k = a.shape
    n = w.shape[1]
    return pl.pallas_call(
        functools.partial(_mm_body, act=act),
        grid=(m // tm, n // tn),
        in_specs=[pl.BlockSpec((tm, k), lambda i, j: (i, 0)),
                  pl.BlockSpec((k, tn), lambda i, j: (0, j)),
                  pl.BlockSpec((1, tn), lambda i, j: (0, j))],
        out_specs=pl.BlockSpec((tm, tn), lambda i, j: (i, j)),
        out_shape=jax.ShapeDtypeStruct((m, n), out_dtype),
        compiler_params=_params("parallel", "arbitrary"),
        name=name,
    )(a, w, bias)


def _ln_mod_body(x_ref, g_ref, b_ref, sc_ref, sh_ref, xn_ref, u_ref):
    xn = _layer_norm(x_ref[...], g_ref[...], b_ref[...])
    xn_ref[...] = xn
    u_ref[...] = (xn * (1.0 + sc_ref[0]) + sh_ref[0]).astype(bf16)


def _mod_spec(mod3, tm, tiles_per_mod):
    return pl.BlockSpec((1, mod3.shape[1], mod3.shape[2]), lambda i: (i // tiles_per_mod, 0, 0))


def _ln_mod(x, g, b, sc, sh, *, tm, tiles_per_mod, name):
    n, d = x.shape
    row = pl.BlockSpec((tm, d), lambda i: (i, 0))
    vec = pl.BlockSpec((1, d), lambda i: (0, 0))
    return pl.pallas_call(
        _ln_mod_body,
        grid=(n // tm,),
        in_specs=[row, vec, vec, _mod_spec(sc, tm, tiles_per_mod), _mod_spec(sh, tm, tiles_per_mod)],
        out_specs=(row, row),
        out_shape=(jax.ShapeDtypeStruct((n, d), f32), jax.ShapeDtypeStruct((n, d), bf16)),
        compiler_params=_params("parallel"),
        name=name,
    )(x, g, b, sc, sh)


ATTN_SPAN = BAND * max(dil for _, dil in ATTN_GROUPS)
ATTN_LANES = 2 * HEAD_DIM


def _attn_prompt_fused_body(*refs):
    q_refs = refs[0:N_GROUPS]
    k_refs = refs[N_GROUPS:2 * N_GROUPS]
    v_refs = refs[2 * N_GROUPS:3 * N_GROUPS]
    a_ref = refs[3 * N_GROUPS]
    o_scr, l_scr = refs[3 * N_GROUPS + 1:]
    span0 = pl.program_id(2) * ATTN_SPAN
    qi = lax.broadcasted_iota(jnp.int32, (BAND, 2 * BAND), 0)
    kj = lax.broadcasted_iota(jnp.int32, (BAND, 2 * BAND), 1)
    dist = qi + BAND - kj
    band = (dist >= 0) & (dist <= BAND)
    heads = ATTN_LANES // HEAD_DIM
    head_of_lane = lax.broadcasted_iota(jnp.int32, (BAND, ATTN_LANES), 1) // HEAD_DIM
    for g, (_, dil) in enumerate(ATTN_GROUPS):
        n_blocks = ATTN_SPAN // (BAND * dil)

        def block(idx, carry, g=g, dil=dil, n_blocks=n_blocks):
            r = idx % dil
            start = r + (idx // dil) * (BAND * dil)
            cur = span0 + start
            prev = cur - BAND * dil
            has_prev = prev >= 0
            prev = jnp.maximum(prev, 0)
            rows = pl.ds(start, BAND, stride=dil)
            q = (q_refs[g][0, rows, :] * ATTN_SCALE).astype(bf16)
            k = jnp.concatenate([k_refs[g][0, pl.ds(prev, BAND, stride=dil), :],
                                 k_refs[g][0, pl.ds(cur, BAND, stride=dil), :]], axis=0).astype(bf16)
            v = jnp.concatenate([v_refs[g][0, pl.ds(prev, BAND, stride=dil), :],
                                 v_refs[g][0, pl.ds(cur, BAND, stride=dil), :]], axis=0).astype(bf16)
            mask = band & ((kj >= BAND) | has_prev)
            v1 = jnp.concatenate([v, jnp.ones_like(v)], axis=-1)
            out = lse = None
            for h in range(heads):
                mine = head_of_lane == h
                s = jnp.where(mask, _dot_nt(jnp.where(mine, q, jnp.zeros_like(q)), k), NEG)
                m = jnp.max(s, axis=-1, keepdims=True)
                pv = _dot(jnp.exp(s - m).astype(bf16), v1)
                den = pv[:, ATTN_LANES:]
                o_h = pv[:, :ATTN_LANES] / den
                l_h = m + jnp.log(den)
                out = o_h if out is None else jnp.where(mine, o_h, out)
                lse = l_h if lse is None else jnp.where(mine, l_h, lse)
            o_scr[g, rows, :] = out
            l_scr[g, rows, :] = lse
            return carry

        lax.fori_loop(0, n_blocks * dil, block, 0, unroll=4)
    ls = [l_scr[g] for g in range(N_GROUPS)]
    m = functools.reduce(jnp.maximum, ls)
    ws = [jnp.exp(l - m) for l in ls]
    num = sum(ws[g] * o_scr[g] for g in range(N_GROUPS))
    a_ref[...] = num / sum(ws)


def _attn_prompt(qkv, bsz, seq):
    assert seq % ATTN_SPAN == 0
    qkv3 = qkv.reshape(bsz, seq, QKV_W)
    n_spans = seq // ATTN_SPAN
    lane_blocks = GROUP_W // ATTN_LANES

    def col(g, which):
        return lambda b, hp, sp: (g * 3 + which) * lane_blocks + hp

    q_specs = [pl.BlockSpec((1, ATTN_SPAN, ATTN_LANES),
                            functools.partial(lambda b, hp, sp, c: (b, sp, c(b, hp, sp)), c=col(g, 0)))
               for g in range(N_GROUPS)]
    kv_specs = [pl.BlockSpec((1, seq, ATTN_LANES),
                             functools.partial(lambda b, hp, sp, c: (b, 0, c(b, hp, sp)), c=col(g, which)))
                for which in (1, 2) for g in range(N_GROUPS)]
    a = pl.pallas_call(
        _attn_prompt_fused_body,
        grid=(bsz, lane_blocks, n_spans),
        in_specs=q_specs + kv_specs,
        out_specs=pl.BlockSpec((ATTN_SPAN, ATTN_LANES), lambda b, hp, sp: (b * n_spans + sp, hp)),
        out_shape=jax.ShapeDtypeStruct((bsz * seq, GROUP_W), f32),
        scratch_shapes=[pltpu.VMEM((N_GROUPS, ATTN_SPAN, ATTN_LANES), f32),
                        pltpu.VMEM((N_GROUPS, ATTN_SPAN, ATTN_LANES), f32)],
        compiler_params=_params("parallel", "parallel", "arbitrary"),
        name="attn_prompt",
    )(*([qkv3] * (3 * N_GROUPS)))
    return a


def _attn_sample_body(qkv_ref, c0_ref, c1_ref, c2_ref, a_ref, *, n_new):
    caches = (c0_ref, c1_ref, c2_ref)
    qkv = qkv_ref[...]
    masks = []
    for window, dil in ATTN_GROUPS:
        s_i = lax.broadcasted_iota(jnp.int32, (n_new, window), 0)
        p_i = lax.broadcasted_iota(jnp.int32, (n_new, window), 1)
        masks.append((p_i >= s_i) & (((s_i - p_i) & (dil - 1)) == 0))
    s_n = lax.broadcasted_iota(jnp.int32, (n_new, n_new), 0)
    k_n = lax.broadcasted_iota(jnp.int32, (n_new, n_new), 1)
    new_masks = [(k_n <= s_n) & (((s_n - k_n) & (dil - 1)) == 0) for _, dil in ATTN_GROUPS]
    for h in range(N_HEADS):
        hs = slice(h * HEAD_DIM, (h + 1) * HEAD_DIM)
        scores, new_scores, new_vals = [], [], []
        for g in range(N_GROUPS):
            base = g * 3 * GROUP_W
            q = (qkv[:, base + h * HEAD_DIM:base + (h + 1) * HEAD_DIM] * ATTN_SCALE).astype(bf16)
            kn = qkv[:, base + GROUP_W + h * HEAD_DIM:base + GROUP_W + (h + 1) * HEAD_DIM].astype(bf16)
            new_vals.append(qkv[:, base + 2 * GROUP_W + h * HEAD_DIM:base + 2 * GROUP_W + (h + 1) * HEAD_DIM]
                            .astype(bf16))
            scores.append(jnp.where(masks[g], _dot(q, caches[g][0, 0, h].astype(bf16)), NEG))
            new_scores.append(jnp.where(new_masks[g], _dot_nt(q, kn), NEG))
        m = scores[0].max(axis=-1, keepdims=True)
        for sc in scores[1:] + new_scores:
            m = jnp.maximum(m, sc.max(axis=-1, keepdims=True))
        den = jnp.zeros((n_new, 1), f32)
        o = jnp.zeros((n_new, HEAD_DIM), f32)
        for g in range(N_GROUPS):
            p = jnp.exp(scores[g] - m)
            pn = jnp.exp(new_scores[g] - m)
            den = den + jnp.sum(p, axis=-1, keepdims=True) + jnp.sum(pn, axis=-1, keepdims=True)
            o = o + _dot_nt(p.astype(bf16), caches[g][0, 1, h].astype(bf16)) + _dot(pn.astype(bf16), new_vals[g])
        a_ref[:, hs] = o / den


def _attn_sample(qkv, caches, bsz, n_new):
    views, specs = [], []
    for g, (window, dil) in enumerate(ATTN_GROUPS):
        assert caches[g].shape[1] == window == BAND * dil and dil & (dil - 1) == 0
        views.append(jnp.transpose(caches[g], (0, 2, 3, 4, 1)))
        specs.append(pl.BlockSpec((1, 2, N_HEADS, HEAD_DIM, window), lambda b: (b, 0, 0, 0, 0)))
    return pl.pallas_call(
        functools.partial(_attn_sample_body, n_new=n_new),
        grid=(bsz,),
        in_specs=[pl.BlockSpec((n_new, QKV_W), lambda b: (b, 0))] + specs,
        out_specs=pl.BlockSpec((n_new, GROUP_W), lambda b: (b, 0)),
        out_shape=jax.ShapeDtypeStruct((bsz * n_new, GROUP_W), f32),
        compiler_params=_params("parallel"),
        name="attn_sample",
    )(qkv, *views)


def _ssd_conv(ext_ref, n, cw_ref, cb_ref):
    y = cb_ref[...]
    for j in range(CONV_W):
        off = SUBLANES - (CONV_W - 1) + j
        y = y + ext_ref[off:off + n, :] * cw_ref[j:j + 1, :]
    return _silu(y)


def _gate_norm(y, z, g, d_inner):
    y = y * _silu(z.astype(f32))
    gw = d_inner // SSM_GROUPS
    parts = []
    for gi in range(SSM_GROUPS):
        yg = y[:, gi * gw:(gi + 1) * gw]
        parts.append(yg * lax.rsqrt(jnp.mean(yg * yg, axis=-1, keepdims=True) + RMS_EPS))
    return jnp.concatenate(parts, axis=-1) * g


def _ssd_prompt_body(xbc_ref, z_ref, dt_ref, cw_ref, cb_ref, dtb_ref, alog_ref, dtbt_ref, alogt_ref,
                     dskip_ref, ng_ref, ex_ref, s_ref, hl_ref, ext_ref, ht_ref, y_ref, *, d_inner, n_heads):
    c = pl.program_id(1)
    q = xbc_ref.shape[0]
    gw = d_inner // SSM_GROUPS
    hpg = n_heads // SSM_GROUPS

    @pl.when(c == 0)
    def _():
        ext_ref[0:SUBLANES, :] = jnp.zeros((SUBLANES, ext_ref.shape[1]), f32)
        ht_ref[...] = jnp.zeros_like(ht_ref)

    @pl.when(c > 0)
    def _():
        ext_ref[0:SUBLANES, :] = ext_ref[q:q + SUBLANES, :]

    ext_ref[SUBLANES:SUBLANES + q, :] = xbc_ref[...]
    xc = _ssd_conv(ext_ref, q, cw_ref, cb_ref)
    xs = xc[:, :d_inner]
    gn = SSM_GROUPS * D_STATE
    bm = xc[:, d_inner:d_inner + gn].astype(bf16)
    cm = xc[:, d_inner + gn:].astype(bf16)

    dtr = dt_ref[...]
    dt = _softplus(dtr + dtb_ref[...])
    da = dt * (-jnp.exp(alog_ref[...]))
    dt_t = _softplus(dtr.T[0:n_heads, :] + dtbt_ref[...])
    da_t = dt_t * (-jnp.exp(alogt_ref[...]))
    ii = lax.broadcasted_iota(jnp.int32, (q, q), 0)
    jj = lax.broadcasted_iota(jnp.int32, (q, q), 1)
    causal = ii >= jj
    lower = causal.astype(bf16)
    upper = (ii <= jj).astype(bf16)
    cum = _dot3_rhs(lower, da)
    cum_t = _dot3_lhs(da_t, upper)
    ex = ex_ref[...]
    cumx = _dot3_lhs(cum, ex)
    dtx = _dot3_lhs(dt, ex)
    clx = cumx[q - 1:q, :]
    ecum = jnp.exp(cumx)
    xd = (jnp.exp(clx - cumx) * dtx * xs).astype(bf16)
    xsb = xs.astype(bf16)

    for g in range(SSM_GROUPS):
        gl = slice(g * gw, (g + 1) * gw)
        cg = cm[:, g * D_STATE:(g + 1) * D_STATE]
        bg = bm[:, g * D_STATE:(g + 1) * D_STATE]
        cb = _dot_nt(cg, bg)
        h_old = ht_ref[g]
        y_ref[:, gl] = ecum[:, gl] * _dot(cg, h_old.astype(bf16))
        ht_ref[g] = jnp.exp(clx[:, gl]) * h_old + _dot_tn(bg, xd[:, gl])
        for e in range(hpg):
            hd = g * hpg + e
            hl = slice(hd * SSM_HEAD_DIM, (hd + 1) * SSM_HEAD_DIM)
            seg = jnp.broadcast_to(cum[:, hd:hd + 1], (q, q)) - jnp.broadcast_to(cum_t[hd:hd + 1, :], (q, q))
            w = cb * jnp.exp(jnp.where(causal, seg, NEG)) * jnp.broadcast_to(dt_t[hd:hd + 1, :], (q, q))
            y_ref[:, hl] += _dot(w.astype(bf16), xsb[:, hl])

    y = y_ref[...] + dskip_ref[...] * xs
    s_ref[...] = _gate_norm(y, z_ref[...], ng_ref[...], d_inner).astype(s_ref.dtype)

    @pl.when(c == pl.num_programs(1) - 1)
    def _():
        for g in range(SSM_GROUPS):
            hl_ref[0, g * gw:(g + 1) * gw, :] = ht_ref[g].T


def _ssd_prompt(xbc, z, dt, prm, bsz, seq, *, chunk):
    n, d_xbc = xbc.shape
    d_inner = z.shape[1]
    n_heads = d_inner // SSM_HEAD_DIM
    nc = seq // chunk
    gw = d_inner // SSM_GROUPS

    def row(w):
        return pl.BlockSpec((chunk, w), lambda b, c: (b * nc + c, 0))

    def full(a):
        return pl.BlockSpec(a.shape, lambda b, c: (0,) * a.ndim)

    consts = [prm["conv_w"], prm["conv_b"], prm["dtb_row"], prm["alog_row"], prm["dtb_t"], prm["alog_t"],
              prm["dskip_x"], prm["norm_g"], prm["ex"]]
    return pl.pallas_call(
        functools.partial(_ssd_prompt_body, d_inner=d_inner, n_heads=n_heads),
        grid=(bsz, nc),
        in_specs=[row(d_xbc), row(d_inner), row(LANES)] + [full(a) for a in consts],
        out_specs=(row(d_inner), pl.BlockSpec((1, d_inner, D_STATE), lambda b, c: (b, 0, 0))),
        out_shape=(jax.ShapeDtypeStruct((n, d_inner), bf16),
                   jax.ShapeDtypeStruct((bsz, d_inner, D_STATE), f32)),
        scratch_shapes=[pltpu.VMEM((SUBLANES + chunk + SUBLANES, d_xbc), f32),
                        pltpu.VMEM((SSM_GROUPS, D_STATE, gw), f32),
                        pltpu.VMEM((chunk, d_inner), f32)],
        compiler_params=_params("parallel", "arbitrary"),
        name="ssd_prompt",
    )(xbc, z, dt, *consts)


def _ssd_sample_body(xbc_ref, cs_ref, z_ref, dt_ref, h_ref, cw_ref, cb_ref, dtb_ref, alog_ref,
                     dskip_ref, ng_ref, ex_ref, rsel_ref, s_ref, hn_ref, ext_ref, *, d_inner, n_heads):
    q = xbc_ref.shape[0]
    gw = d_inner // SSM_GROUPS
    ext_ref[0:SUBLANES, :] = cs_ref[0]
    ext_ref[SUBLANES:SUBLANES + q, :] = xbc_ref[...]
    xc = _ssd_conv(ext_ref, q, cw_ref, cb_ref)
    xs = xc[:, :d_inner]
    gn = SSM_GROUPS * D_STATE
    bm = xc[:, d_inner:d_inner + gn]
    cm = xc[:, d_inner + gn:]

    dt = _softplus(dt_ref[...] + dtb_ref[...])
    da = dt * (-jnp.exp(alog_ref[...]))
    row = lax.broadcasted_iota(jnp.int32, (q, LANES), 0)
    cum = jnp.zeros((q, LANES), f32)
    for j in range(q):
        cum = cum + jnp.where(row >= j, da[j:j + 1, :], 0.0)
    ex = ex_ref[...]
    cumx = _dot3_lhs(cum, ex)
    dtx = _dot3_lhs(dt, ex)
    clx = cumx[q - 1:q, :]
    rowx = lax.broadcasted_iota(jnp.int32, (q, d_inner), 0)

    y = dskip_ref[...] * xs
    for j in range(q):
        prod = cm * bm[j:j + 1, :]
        cbx = jnp.concatenate(
            [jnp.broadcast_to(jnp.sum(prod[:, g * D_STATE:(g + 1) * D_STATE], axis=-1, keepdims=True), (q, gw))
             for g in range(SSM_GROUPS)], axis=-1)
        seg = jnp.where(rowx >= j, cumx - cumx[j:j + 1, :], NEG)
        y = y + cbx * jnp.exp(seg) * (dtx[j:j + 1, :] * xs[j:j + 1, :])

    xd = (jnp.exp(clx - cumx) * dtx * xs).astype(bf16)
    ones = jnp.ones((q, LANES), bf16)
    da_hi, da_mid, da_lo = _split3(da)
    cl_b = _dot_tn(da_hi, ones) + _dot_tn(da_mid, ones) + _dot_tn(da_lo, ones)
    decay = jnp.exp(_dot3_rhs(rsel_ref[...], cl_b))
    ecum = jnp.exp(cumx)
    cmb = cm.astype(bf16)
    bmb = bm.astype(bf16)
    ys = []
    for g in range(SSM_GROUPS):
        rows = slice(g * gw, (g + 1) * gw)
        hg = h_ref[0, rows, :]
        ys.append(_dot_nt(cmb[:, g * D_STATE:(g + 1) * D_STATE], hg.astype(bf16)))
        hn_ref[0, rows, :] = decay[rows, :] * hg + _dot_tn(xd[:, rows], bmb[:, g * D_STATE:(g + 1) * D_STATE])
    y = y + ecum * jnp.concatenate(ys, axis=-1)
    s_ref[...] = _gate_norm(y, z_ref[...], ng_ref[...], d_inner)


def _ssd_sample(xbc, conv_state8, z, dt, h0, prm, bsz, n_new):
    n, d_xbc = xbc.shape
    d_inner = z.shape[1]
    n_heads = d_inner // SSM_HEAD_DIM

    def row(w):
        return pl.BlockSpec((n_new, w), lambda b: (b, 0))

    def full(a):
        return pl.BlockSpec(a.shape, lambda b: (0,) * a.ndim)

    consts = [prm["conv_w"], prm["conv_b"], prm["dtb_row"], prm["alog_row"],
              prm["dskip_x"], prm["norm_g"], prm["ex"], prm["rsel"]]
    state = pl.BlockSpec((1, d_inner, D_STATE), lambda b: (b, 0, 0))
    return pl.pallas_call(
        functools.partial(_ssd_sample_body, d_inner=d_inner, n_heads=n_heads),
        grid=(bsz,),
        in_specs=[row(d_xbc), pl.BlockSpec((1, SUBLANES, d_xbc), lambda b: (b, 0, 0)), row(d_inner), row(LANES),
                  state] + [full(a) for a in consts],
        out_specs=(row(d_inner), state),
        out_shape=(jax.ShapeDtypeStruct((n, d_inner), f32),
                   jax.ShapeDtypeStruct((bsz, d_inner, D_STATE), f32)),
        scratch_shapes=[pltpu.VMEM((2 * SUBLANES, d_xbc), f32)],
        compiler_params=_params("parallel"),
        name="ssd_sample",
    )(xbc, conv_state8, z, dt, h0, *consts)


def _mixer_body(a_ref, s_ref, ga_ref, gs_ref, xn_ref, gt_ref, sc_ref, sh_ref, wa_ref, ws_ref, wo_ref,
                lg_ref, lb_ref, wrh_ref, wrl_ref, br_ref, x1_ref, u2_ref, lo_ref):
    m = (jax.nn.sigmoid(ga_ref[...].astype(f32)) * _dot(a_ref[...].astype(bf16), wa_ref[...])
         + jax.nn.sigmoid(gs_ref[...].astype(f32)) * _dot(s_ref[...].astype(bf16), ws_ref[...]))
    o = _dot(m.astype(bf16), wo_ref[...])
    x1 = _layer_norm(DN_ALPHA * xn_ref[...] + gt_ref[0] * o, lg_ref[...], lb_ref[...])
    x1_ref[...] = x1
    u2 = x1 * (1.0 + sc_ref[0]) + sh_ref[0]
    _store_row_tiles(u2_ref, u2)
    hi, mid, lo = _split3(u2)
    wrh = wrh_ref[...]
    wrl = wrl_ref[...]
    lo_ref[...] = (_dot(hi, wrh) + (_dot(hi, wrl) + _dot(mid, wrh)) + (_dot(mid, wrl) + _dot(lo, wrh))
                   + br_ref[...])


def _mixer(a, s, gates, xn, gt, sc, sh, wts, *, tm, tiles_per_mod, name):
    n, d = xn.shape

    def row(w, col=0):
        return pl.BlockSpec((tm, w), lambda i: (i, col))

    def full(arr):
        return pl.BlockSpec(arr.shape, lambda i: (0,) * arr.ndim)

    consts = [wts["w_attn_br"], wts["w_ssd_br"], wts["w_out"], wts["ln_mix_g"], wts["ln_mix_b"],
              wts["w_router_hi"], wts["w_router_lo"], wts["b_router"]]
    return pl.pallas_call(
        _mixer_body,
        grid=(n // tm,),
        in_specs=[row(a.shape[1]), row(s.shape[1]), row(d, 0), row(d, 1), row(d),
                  _mod_spec(gt, tm, tiles_per_mod), _mod_spec(sc, tm, tiles_per_mod),
                  _mod_spec(sh, tm, tiles_per_mod)] + [full(c) for c in consts],
        out_specs=(row(d), pl.BlockSpec((tm * (d // LANES), LANES), lambda i: (i, 0)), row(LANES)),
        out_shape=(jax.ShapeDtypeStruct((n, d), f32), jax.ShapeDtypeStruct((n * (d // LANES), LANES), f32),
                   jax.ShapeDtypeStruct((n, LANES), f32)),
        compiler_params=_params("parallel"),
        name=name,
    )(a, s, gates, gates, xn, gt, sc, sh, *consts)


def _moe_body(be_ref, nv_ref, nr_ref, tok_ref, tok_next_ref, slot_ref, x_hbm, wg_ref, bg_ref, wu_ref, bu_ref, wd_ref,
              bd_ref, y_hbm, xbuf, ybuf, wgb, wub, wdb, sem_in, sem_out, *, bm):
    i = pl.program_id(0)
    nv = nv_ref[0]
    nc = xbuf.shape[1] // bm
    cur = i % 2
    nxt = 1 - cur

    def start_gather(idx_ref, buf):
        def body(r, carry):
            src = pl.multiple_of(idx_ref[0, 0, r], nc)
            pltpu.make_async_copy(x_hbm.at[pl.ds(src, nc), :], xbuf.at[buf, pl.ds(r * nc, nc), :],
                                  sem_in.at[buf]).start(priority=1)
            return carry
        lax.fori_loop(0, bm, body, 0, unroll=DMA_ISSUE_UNROLL)

    def wait_gather(buf):
        pltpu.make_async_copy(x_hbm.at[pl.ds(0, bm * nc), :], xbuf.at[buf], sem_in.at[buf]).wait()

    def scatter_row(buf, r, slot):
        dst = pl.multiple_of(slot, nc)
        return pltpu.make_async_copy(ybuf.at[buf, pl.ds(r * nc, nc), :], y_hbm.at[pl.ds(dst, nc), :], sem_out.at[buf])

    def start_scatter(buf, n_real):
        def body(r, carry):
            scatter_row(buf, r, slot_ref[0, 0, r]).start()
            return carry

        @pl.when(n_real == bm)
        def _():
            lax.fori_loop(0, bm, body, 0, unroll=DMA_ISSUE_UNROLL)

        @pl.when(n_real < bm)
        def _():
            lax.fori_loop(0, n_real, body, 0)

    def wait_scatter(buf, n_real):
        @pl.when(n_real == bm)
        def _():
            pltpu.make_async_copy(ybuf.at[buf], y_hbm.at[pl.ds(0, bm * nc), :], sem_out.at[buf]).wait()

        @pl.when(n_real < bm)
        def _():
            def body(r, carry):
                scatter_row(buf, 0, 0).wait()
                return carry
            lax.fori_loop(0, n_real, body, 0)

    @pl.when((i == 0) & (nv > 0))
    def _():
        start_gather(tok_ref, cur)

    @pl.when(i + 1 < nv)
    def _():
        start_gather(tok_next_ref, nxt)

    @pl.when(i < nv)
    def _():
        @pl.when((i == 0) | (be_ref[i] != be_ref[jnp.maximum(i - 1, 0)]))
        def _():
            wgb[...] = wg_ref[0].astype(bf16)
            wub[...] = wu_ref[0].astype(bf16)
            wdb[...] = wd_ref[0].astype(bf16)

        wait_gather(cur)
        xb = _load_row_tiles(xbuf, bm, nc, (cur,)).astype(bf16)
        gate = jnp.minimum(_dot(xb, wgb[...]) + bg_ref[0], SWIGLU_LIMIT)
        up = jnp.clip(_dot(xb, wub[...]) + bu_ref[0], -SWIGLU_LIMIT, SWIGLU_LIMIT)
        h = gate * jax.nn.sigmoid(SWIGLU_ALPHA * gate) * (up + 1.0)
        y = _dot(h.astype(bf16), wdb[...]) + bd_ref[0]

        @pl.when(i >= 1)
        def _():
            wait_scatter(nxt, nr_ref[jnp.maximum(i - 1, 0)])

        _store_row_tiles(ybuf, y, (cur,))
        start_scatter(cur, nr_ref[i])

        @pl.when(i == nv - 1)
        def _():
            wait_scatter(cur, nr_ref[i])


def _moe(u2, logits, wts, *, bm):
    d_ff, d = wts["w_down"].shape[1:]
    nc = d // LANES
    n_tok = u2.shape[0] // nc
    n_assign = n_tok * TOP_K
    i32 = jnp.int32
    top_logit, top_idx = lax.top_k(logits[:, :N_EXPERTS], TOP_K)
    top_gate = jax.nn.softmax(top_logit, axis=-1)
    e_flat = top_idx.reshape(-1)
    order = jnp.argsort(e_flat).astype(i32)
    experts = jnp.arange(N_EXPERTS, dtype=i32)
    counts = jnp.sum((e_flat[:, None] == experts[None, :]).astype(i32), axis=0)
    starts = jnp.cumsum(counts) - counts
    padded = (counts + bm - 1) // bm * bm
    pad_ends = jnp.cumsum(padded)
    pad_starts = pad_ends - padded
    n_blocks = -(-n_assign // bm) + N_EXPERTS
    blk_row = jnp.arange(n_blocks, dtype=i32) * bm
    block_expert = jnp.minimum(jnp.sum((pad_ends[None, :] <= blk_row[:, None]).astype(i32), axis=1), N_EXPERTS - 1)
    off = blk_row - pad_starts[block_expert]
    n_real = jnp.clip(counts[block_expert] - off, 0, bm)
    j = jnp.arange(bm, dtype=i32)[None, :]
    src = jnp.clip(starts[block_expert][:, None] + off[:, None] + j, 0, n_assign - 1)
    a_id = jnp.take(order, src.reshape(-1), axis=0).reshape(n_blocks, bm)
    real = j < n_real[:, None]
    row_tok = jnp.where(real, a_id // TOP_K, 0) * nc
    row_slot = jnp.where(real, (a_id % TOP_K) * n_tok + a_id // TOP_K, 0) * nc
    n_valid = (pad_ends[-1] // bm).astype(i32).reshape(1)

    def wspec(shape):
        return pl.BlockSpec((1,) + shape, lambda i, be, nv, nr: (be[i], 0, 0))

    idx_spec = pl.BlockSpec((1, 1, bm), lambda i, be, nv, nr: (i, 0, 0), memory_space=pltpu.SMEM)
    idx_next = pl.BlockSpec((1, 1, bm), lambda i, be, nv, nr: (jnp.minimum(i + 1, n_blocks - 1), 0, 0),
                            memory_space=pltpu.SMEM)
    grid_spec = pltpu.PrefetchScalarGridSpec(
        num_scalar_prefetch=3,
        grid=(n_blocks,),
        in_specs=[idx_spec, idx_next, idx_spec,
                  pl.BlockSpec(memory_space=pl.ANY),
                  wspec((d, d_ff)), wspec((1, d_ff)), wspec((d, d_ff)), wspec((1, d_ff)),
                  wspec((d_ff, d)), wspec((1, d))],
        out_specs=pl.BlockSpec(memory_space=pl.ANY),
        scratch_shapes=[pltpu.VMEM((2, bm * nc, LANES), f32), pltpu.VMEM((2, bm * nc, LANES), f32),
                        pltpu.VMEM((d, d_ff), bf16), pltpu.VMEM((d, d_ff), bf16), pltpu.VMEM((d_ff, d), bf16),
                        pltpu.SemaphoreType.DMA((2,)), pltpu.SemaphoreType.DMA((2,))],
    )
    row_tok3 = row_tok.reshape(n_blocks, 1, bm)
    y = pl.pallas_call(
        functools.partial(_moe_body, bm=bm),
        grid_spec=grid_spec,
        out_shape=jax.ShapeDtypeStruct((n_assign * nc, LANES), f32),
        compiler_params=_params("arbitrary"),
        name=f"moe_bm{bm}",
    )(block_expert, n_valid, n_real, row_tok3, row_tok3, row_slot.reshape(n_blocks, 1, bm), u2,
      wts["w_gate"], wts["b_gate"], wts["w_up"], wts["b_up"], wts["w_down"], wts["b_down"])
    return y, top_gate


def _final_body(x1_ref, *rest):
    yk_refs, (pg_ref, gt_ref, g_ref, b_ref, o_ref) = rest[:TOP_K], rest[TOP_K:]
    pg = pg_ref[...]
    tm, d = x1_ref.shape
    y = pg[:, 0:1] * _load_row_tiles(yk_refs[0], tm, d // LANES)
    for k in range(1, TOP_K):
        y = y + pg[:, k:k + 1] * _load_row_tiles(yk_refs[k], tm, d // LANES)
    o_ref[...] = _layer_norm(DN_ALPHA * x1_ref[...] + gt_ref[0] * y, g_ref[...], b_ref[...])


def _final(x1, yk, top_gate, gt, g, b, *, tm, tiles_per_mod, row0, n_all, name):
    n, d = x1.shape
    nt = n // tm
    assert row0 % tm == 0 and n_all % tm == 0
    nt_all, blk0 = n_all // tm, row0 // tm
    row = pl.BlockSpec((tm, d), lambda i: (i, 0))
    vec = pl.BlockSpec((1, d), lambda i: (0, 0))
    planes = [pl.BlockSpec((tm * (d // LANES), LANES),
                           functools.partial(lambda i, k: (k * nt_all + blk0 + i, 0), k=k)) for k in range(TOP_K)]
    return pl.pallas_call(
        _final_body,
        grid=(nt,),
        in_specs=[row] + planes + [pl.BlockSpec((tm, TOP_K), lambda i: (i, 0)),
                                   _mod_spec(gt, tm, tiles_per_mod), vec, vec],
        out_specs=row,
        out_shape=jax.ShapeDtypeStruct((n, d), f32),
        compiler_params=_params("parallel"),
        name=name,
    )(x1, *([yk] * TOP_K), top_gate, gt, g, b)


def _stream(x, c, wts, ssd_prm, *, caches=None, conv_state=None, ssm_state=None):
    bsz, seq, d = x.shape
    n = bsz * seq
    is_prompt = caches is None
    d_inner = wts["w_z"].shape[1]
    d_xbc = wts["w_xbc"].shape[1]

    c_pad = jnp.pad(c, ((0, (-bsz) % SUBLANES), (0, 0)))
    mod = _mm(c_pad, wts["w_ada"], wts["b_ada"], tm=c_pad.shape[0], tn=2 * d, act="silu", name="ada_mod")[:bsz]
    sh1, sc1, gt1, sh2, sc2, gt2 = jnp.split(mod, 6, axis=-1)

    def per_tile(p, tile):
        if seq % tile == 0:
            return p.reshape(bsz, 1, d), seq // tile
        return jnp.broadcast_to(p[:, None, :], (bsz, seq, d)).reshape(n // tile, tile, d), 1

    tm = 512
    tpm = per_tile(sc1, tm)[1]
    xn, u = _ln_mod(x.reshape(n, d), wts["ln_emb_g"], wts["ln_emb_b"], per_tile(sc1, tm)[0], per_tile(sh1, tm)[0],
                    tm=tm, tiles_per_mod=tpm, name="ln_mod")
    tmm = min(n, 2048)
    act_dtype = bf16 if seq % 16 == 0 else f32
    qkv = _mm(u, wts["w_qkv"], wts["zero_b"][:, :QKV_W], tm=tmm, tn=512, name="proj_qkv")
    z = _mm(u, wts["w_z"], wts["zero_b"][:, :d_inner], tm=tmm, tn=512, out_dtype=act_dtype, name="proj_z")
    xbc = _mm(u, wts["w_xbc"], wts["zero_b"][:, :d_xbc], tm=tmm, tn=512, name="proj_xbc")
    dt = _mm(u, wts["w_dt"], wts["zero_b"][:, :LANES], tm=tmm, tn=LANES, name="proj_dt")
    gates = _mm(u, wts["w_gates"], wts["zero_b"][:, :2 * d], tm=tmm, tn=512, out_dtype=bf16, name="proj_gates")

    qkv3 = qkv.reshape(bsz, seq, QKV_W)
    xbc3 = xbc.reshape(bsz, seq, d_xbc)

    def kv_rows_of(g, keep):
        cols = qkv3[:, seq - keep:, (g * 3 + 1) * GROUP_W:(g * 3 + 3) * GROUP_W]
        return cols.reshape(bsz, keep, 2, N_HEADS, HEAD_DIM)

    if is_prompt:
        a = _attn_prompt(qkv, bsz, seq)
        s, h_last = _ssd_prompt(xbc, z, dt, ssd_prm, bsz, seq, chunk=128)
        kv_rows = [kv_rows_of(g, min(w, seq)) for g, (w, _) in enumerate(ATTN_GROUPS)]
        new_conv = xbc3[:, -(CONV_W - 1):]
    else:
        a = _attn_sample(qkv, caches, bsz, seq)
        cs8 = jnp.pad(conv_state, ((0, 0), (SUBLANES - (CONV_W - 1), 0), (0, 0)))
        s, h_last = _ssd_sample(xbc, cs8, z, dt, ssm_state.reshape(bsz, d_inner, D_STATE), ssd_prm, bsz, seq)
        kv_rows = [kv_rows_of(g, seq) for g in range(N_GROUPS)]
        new_conv = jnp.concatenate([conv_state, xbc3], axis=1)[:, -(CONV_W - 1):]
    new_ssm = h_last.reshape(bsz, d_inner // SSM_HEAD_DIM, SSM_HEAD_DIM, D_STATE)

    tmx = 512
    x1, u2, logits = _mixer(a, s, gates, xn, per_tile(gt1, tmx)[0], per_tile(sc2, tmx)[0], per_tile(sh2, tmx)[0],
                            wts, tm=tmx, tiles_per_mod=per_tile(gt1, tmx)[1], name="mixer")

    def finish(yk, top_gate, row0, n_all):
        y = _final(x1, yk, top_gate[row0:row0 + n], per_tile(gt2, tm)[0], wts["ln_ffn_g"], wts["ln_ffn_b"], tm=tm,
                   tiles_per_mod=tpm, row0=row0, n_all=n_all, name="final_ln")
        return y.reshape(bsz, seq, d)

    return (u2, logits, finish), kv_rows, new_conv, new_ssm


def kernel(x_prompt, x_sample, c_prompt, c_sample, cache_kv_w128, cache_kv_w512, cache_kv_w2048, state_conv, state_ssm, ln_emb_g, ln_emb_b, w_ada, b_ada, w_in, conv_w, conv_b, dt_bias, a_log, d_skip, ssm_norm_g, w_attn_br, w_ssd_br, w_out, ln_mix_g, ln_mix_b, w_router, b_router, w_gate, b_gate, w_up, b_up, w_down, b_down, ln_ffn_g, ln_ffn_b):
    d = x_prompt.shape[-1]
    d_inner = ssm_norm_g.shape[-1]
    d_xbc = conv_w.shape[-1]
    n_heads = dt_bias.shape[-1]
    lyr = 0

    def rowv(v):
        return v.reshape(1, -1).astype(f32)

    def lane_pad(v):
        return jnp.pad(v.astype(f32), (0, LANES - n_heads)).reshape(1, LANES)

    wi = w_in[lyr]
    o0 = QKV_W
    o1 = o0 + d_inner
    o2 = o1 + d_xbc
    o3 = o2 + n_heads
    wr = jnp.pad(w_router[lyr], ((0, 0), (0, LANES - N_EXPERTS)))
    wr_hi = wr.astype(bf16)
    wts = {
        "ln_emb_g": rowv(ln_emb_g), "ln_emb_b": rowv(ln_emb_b),
        "w_ada": w_ada[lyr].astype(bf16), "b_ada": rowv(b_ada[lyr]),
        "w_qkv": wi[:, :o0].astype(bf16), "w_z": wi[:, o0:o1].astype(bf16), "w_xbc": wi[:, o1:o2].astype(bf16),
        "w_dt": jnp.pad(wi[:, o2:o3], ((0, 0), (0, LANES - n_heads))).astype(bf16),
        "w_gates": wi[:, o3:].astype(bf16),
        "zero_b": jnp.zeros((1, max(QKV_W, d_xbc, 2 * d)), f32),
        "w_attn_br": w_attn_br[lyr].astype(bf16), "w_ssd_br": w_ssd_br[lyr].astype(bf16),
        "w_out": w_out[lyr].astype(bf16),
        "ln_mix_g": rowv(ln_mix_g[lyr]), "ln_mix_b": rowv(ln_mix_b[lyr]),
        "w_router_hi": wr_hi, "w_router_lo": (wr - wr_hi.astype(f32)).astype(bf16),
        "b_router": jnp.pad(b_router[lyr], (0, LANES - N_EXPERTS)).reshape(1, LANES),
        "w_gate": w_gate[lyr], "b_gate": b_gate[lyr][:, None, :],
        "w_up": w_up[lyr], "b_up": b_up[lyr][:, None, :],
        "w_down": w_down[lyr], "b_down": b_down[lyr][:, None, :],
        "ln_ffn_g": rowv(ln_ffn_g[lyr]), "ln_ffn_b": rowv(ln_ffn_b[lyr]),
    }
    head_of_lane = jnp.arange(d_inner, dtype=jnp.int32) // SSM_HEAD_DIM
    ssd_prm = {
        "conv_w": conv_w[lyr], "conv_b": rowv(conv_b[lyr]),
        "dtb_row": lane_pad(dt_bias[lyr]), "alog_row": lane_pad(a_log[lyr]),
        "dtb_t": jnp.broadcast_to(dt_bias[lyr][:, None], (n_heads, LANES)),
        "alog_t": jnp.broadcast_to(a_log[lyr][:, None], (n_heads, LANES)),
        "dskip_x": jnp.repeat(d_skip[lyr], SSM_HEAD_DIM).reshape(1, d_inner),
        "norm_g": rowv(ssm_norm_g[lyr]),
        "ex": (jnp.arange(LANES, dtype=jnp.int32)[:, None] == head_of_lane[None, :]).astype(bf16),
        "rsel": (head_of_lane[:, None] == jnp.arange(LANES, dtype=jnp.int32)[None, :]).astype(bf16),
    }

    (u2_p, logits_p, finish_p), kv_p, conv_p, ssm_p = _stream(x_prompt, c_prompt, wts, ssd_prm)
    (u2_s, logits_s, finish_s), kv_s, conv_s, ssm_s = _stream(
        x_sample, c_sample, wts, ssd_prm,
        caches=(cache_kv_w128[lyr], cache_kv_w512[lyr], cache_kv_w2048[lyr]),
        conv_state=state_conv[lyr], ssm_state=state_ssm[lyr])
    n_p, n_s = logits_p.shape[0], logits_s.shape[0]
    yk, top_gate = _moe(jnp.concatenate([u2_p, u2_s], axis=0), jnp.concatenate([logits_p, logits_s], axis=0), wts,
                        bm=MOE_BLOCK_ROWS)
    yp = finish_p(yk, top_gate, 0, n_p + n_s)
    ys = finish_s(yk, top_gate, n_p, n_p + n_s)
    return (yp, ys, kv_p[0][None], kv_s[0][None], kv_p[1][None], kv_s[1][None], kv_p[2][None], kv_s[2][None],
            conv_p[None], conv_s[None], ssm_p[None], ssm_s[None])
```

```python
import functools
import math

import jax
import jax.numpy as jnp
from jax import lax
from jax.experimental import pallas as pl
from jax.experimental.pallas import tpu as pltpu

f32 = jnp.float32
bf16 = jnp.bfloat16

ATTN_GROUPS = ((128, 1), (512, 4), (2048, 16))
N_GROUPS = len(ATTN_GROUPS)
N_HEADS = 8
HEAD_DIM = 64
GROUP_W = N_HEADS * HEAD_DIM
QKV_W = N_GROUPS * 3 * GROUP_W
BAND = 128
ATTN_SCALE = HEAD_DIM ** -0.5
SSM_HEAD_DIM = 64
SSM_GROUPS = 4
D_STATE = 128
CONV_W = 4
N_EXPERTS = 32
TOP_K = 4
SWIGLU_ALPHA = 1.702
SWIGLU_LIMIT = 7.0
LN_EPS = 1e-5
RMS_EPS = 1e-5
DEPTH = 1
DN_ALPHA = (2 * DEPTH) ** 0.25
NEG = -1e30

LANES = 128
SUBLANES = 8
VMEM_LIMIT = 56 * 1024 * 1024
MOE_BLOCK_ROWS = 256
MOE_FF_CHUNK = 256
DMA_ISSUE_UNROLL = 32


def _params(*sem):
    return pltpu.CompilerParams(dimension_semantics=sem, vmem_limit_bytes=VMEM_LIMIT)


def _silu(x):
    return x * jax.nn.sigmoid(x)


def _softplus(x):
    return jnp.maximum(x, 0.0) + jnp.log(1.0 + jnp.exp(-jnp.abs(x)))


def _split3(x):
    hi = x.astype(bf16)
    r1 = x - hi.astype(f32)
    mid = r1.astype(bf16)
    lo = (r1 - mid.astype(f32)).astype(bf16)
    return hi, mid, lo


def _dot(a, b):
    return jnp.dot(a, b, preferred_element_type=f32)


def _dot_nt(a, b):
    return lax.dot_general(a, b, (((1,), (1,)), ((), ())), preferred_element_type=f32)


def _dot_tn(a, b):
    return lax.dot_general(a, b, (((0,), (0,)), ((), ())), preferred_element_type=f32)


def _dot3_lhs(x, w):
    hi, mid, lo = _split3(x)
    return _dot(hi, w) + _dot(mid, w) + _dot(lo, w)


def _dot3_rhs(w, x):
    hi, mid, lo = _split3(x)
    return _dot(w, hi) + _dot(w, mid) + _dot(w, lo)


def _layer_norm(x, g, b):
    mu = jnp.mean(x, axis=-1, keepdims=True)
    xc = x - mu
    var = jnp.mean(xc * xc, axis=-1, keepdims=True)
    return xc * lax.rsqrt(var + LN_EPS) * g + b


def _store_row_tiles(ref, x, lead=()):
    rows, d = x.shape
    nc = d // LANES
    for c in range(nc):
        ref[lead + (pl.ds(c, rows, stride=nc), slice(None))] = x[:, c * LANES:(c + 1) * LANES]


def _load_row_tiles(ref, rows, nc, lead=()):
    return jnp.concatenate([ref[lead + (pl.ds(c, rows, stride=nc), slice(None))] for c in range(nc)], axis=-1)


def _mm_body(a_ref, w_ref, b_ref, o_ref, *, act):
    a = a_ref[...]
    if act == "silu":
        a = _silu(a.astype(f32))
    o_ref[...] = (_dot(a.astype(bf16), w_ref[...]) + b_ref[...]).astype(o_ref.dtype)


def _mm(a, w, bias, *, tm, tn, act=None, out_dtype=f32, name):
    m, k = a.shape
    n = w.shape[1]
    return pl.pallas_call(
        functools.partial(_mm_body, act=act),
        grid=(m // tm, n // tn),
        in_specs=[pl.BlockSpec((tm, k), lambda i, j: (i, 0)),
                  pl.BlockSpec((k, tn), lambda i, j: (0, j)),
                  pl.BlockSpec((1, tn), lambda i, j: (0, j))],
        out_specs=pl.BlockSpec((tm, tn), lambda i, j: (i, j)),
        out_shape=jax.ShapeDtypeStruct((m, n), out_dtype),
        compiler_params=_params("parallel", "arbitrary"),
        name=name,
    )(a, w, bias)


def _ln_mod_body(x_ref, g_ref, b_ref, sc_ref, sh_ref, xn_ref, u_ref):
    xn = _layer_norm(x_ref[...], g_ref[...], b_ref[...])
    xn_ref[...] = xn
    u_ref[...] = (xn * (1.0 + sc_ref[0]) + sh_ref[0]).astype(bf16)


def _mod_spec(mod3, tm, tiles_per_mod):
    return pl.BlockSpec((1, mod3.shape[1], mod3.shape[2]), lambda i: (i // tiles_per_mod, 0, 0))


def _ln_mod(x, g, b, sc, sh, *, tm, tiles_per_mod, name):
    n, d = x.shape
    row = pl.BlockSpec((tm, d), lambda i: (i, 0))
    vec = pl.BlockSpec((1, d), lambda i: (0, 0))
    return pl.pallas_call(
        _ln_mod_body,
        grid=(n // tm,),
        in_specs=[row, vec, vec, _mod_spec(sc, tm, tiles_per_mod), _mod_spec(sh, tm, tiles_per_mod)],
        out_specs=(row, row),
        out_shape=(jax.ShapeDtypeStruct((n, d), f32), jax.ShapeDtypeStruct((n, d), bf16)),
        compiler_params=_params("parallel"),
        name=name,
    )(x, g, b, sc, sh)


ATTN_SPAN = BAND * max(dil for _, dil in ATTN_GROUPS)
ATTN_LANES = 2 * HEAD_DIM


def _attn_prompt_fused_body(*refs):
    q_refs = refs[0:N_GROUPS]
    k_refs = refs[N_GROUPS:2 * N_GROUPS]
    v_refs = refs[2 * N_GROUPS:3 * N_GROUPS]
    a_ref = refs[3 * N_GROUPS]
    o_scr, l_scr = refs[3 * N_GROUPS + 1:]
    span0 = pl.program_id(2) * ATTN_SPAN
    qi = lax.broadcasted_iota(jnp.int32, (BAND, 2 * BAND), 0)
    kj = lax.broadcasted_iota(jnp.int32, (BAND, 2 * BAND), 1)
    dist = qi + BAND - kj
    band = (dist >= 0) & (dist <= BAND)
    heads = ATTN_LANES // HEAD_DIM
    head_of_lane = lax.broadcasted_iota(jnp.int32, (BAND, ATTN_LANES), 1) // HEAD_DIM
    for g, (_, dil) in enumerate(ATTN_GROUPS):
        n_blocks = ATTN_SPAN // (BAND * dil)

        def block(idx, carry, g=g, dil=dil, n_blocks=n_blocks):
            r = idx % dil
            start = r + (idx // dil) * (BAND * dil)
            cur = span0 + start
            prev = cur - BAND * dil
            has_prev = prev >= 0
            prev = jnp.maximum(prev, 0)
            rows = pl.ds(start, BAND, stride=dil)
            q = (q_refs[g][0, rows, :] * ATTN_SCALE).astype(bf16)
            k = jnp.concatenate([k_refs[g][0, pl.ds(prev, BAND, stride=dil), :],
                                 k_refs[g][0, pl.ds(cur, BAND, stride=dil), :]], axis=0).astype(bf16)
            v = jnp.concatenate([v_refs[g][0, pl.ds(prev, BAND, stride=dil), :],
                                 v_refs[g][0, pl.ds(cur, BAND, stride=dil), :]], axis=0).astype(bf16)
            mask = band & ((kj >= BAND) | has_prev)
            v1 = jnp.concatenate([v, jnp.ones_like(v)], axis=-1)
            out = lse = None
            for h in range(heads):
                mine = head_of_lane == h
                s = jnp.where(mask, _dot_nt(jnp.where(mine, q, jnp.zeros_like(q)), k), NEG)
                m = jnp.max(s, axis=-1, keepdims=True)
                pv = _dot(jnp.exp(s - m).astype(bf16), v1)
                den = pv[:, ATTN_LANES:]
                o_h = pv[:, :ATTN_LANES] / den
                l_h = m + jnp.log(den)
                out = o_h if out is None else jnp.where(mine, o_h, out)
                lse = l_h if lse is None else jnp.where(mine, l_h, lse)
            o_scr[g, rows, :] = out
            l_scr[g, rows, :] = lse
            return carry

        lax.fori_loop(0, n_blocks * dil, block, 0, unroll=4)
    ls = [l_scr[g] for g in range(N_GROUPS)]
    m = functools.reduce(jnp.maximum, ls)
    ws = [jnp.exp(l - m) for l in ls]
    num = sum(ws[g] * o_scr[g] for g in range(N_GROUPS))
    a_ref[...] = num / sum(ws)


def _attn_prompt(qkv, bsz, seq):
    assert seq % ATTN_SPAN == 0
    qkv3 = qkv.reshape(bsz, seq, QKV_W)
    n_spans = seq // ATTN_SPAN
    lane_blocks = GROUP_W // ATTN_LANES

    def col(g, which):
        return lambda b, hp, sp: (g * 3 + which) * lane_blocks + hp

    q_specs = [pl.BlockSpec((1, ATTN_SPAN, ATTN_LANES),
                            functools.partial(lambda b, hp, sp, c: (b, sp, c(b, hp, sp)), c=col(g, 0)))
               for g in range(N_GROUPS)]
    kv_specs = [pl.BlockSpec((1, seq, ATTN_LANES),
                             functools.partial(lambda b, hp, sp, c: (b, 0, c(b, hp, sp)), c=col(g, which)))
                for which in (1, 2) for g in range(N_GROUPS)]
    a = pl.pallas_call(
        _attn_prompt_fused_body,
        grid=(bsz, lane_blocks, n_spans),
        in_specs=q_specs + kv_specs,
        out_specs=pl.BlockSpec((ATTN_SPAN, ATTN_LANES), lambda b, hp, sp: (b * n_spans + sp, hp)),
        out_shape=jax.ShapeDtypeStruct((bsz * seq, GROUP_W), f32),
        scratch_shapes=[pltpu.VMEM((N_GROUPS, ATTN_SPAN, ATTN_LANES), f32),
                        pltpu.VMEM((N_GROUPS, ATTN_SPAN, ATTN_LANES), f32)],
        compiler_params=_params("parallel", "parallel", "arbitrary"),
        name="attn_prompt",
    )(*([qkv3] * (3 * N_GROUPS)))
    return a


def _attn_sample_body(qkv_ref, c0_ref, c1_ref, c2_ref, a_ref, *, n_new):
    caches = (c0_ref, c1_ref, c2_ref)
    qkv = qkv_ref[...]
    masks = []
    for window, dil in ATTN_GROUPS:
        s_i = lax.broadcasted_iota(jnp.int32, (n_new, window), 0)
        p_i = lax.broadcasted_iota(jnp.int32, (n_new, window), 1)
        masks.append((p_i >= s_i) & (((s_i - p_i) & (dil - 1)) == 0))
    s_n = lax.broadcasted_iota(jnp.int32, (n_new, n_new), 0)
    k_n = lax.broadcasted_iota(jnp.int32, (n_new, n_new), 1)
    new_masks = [(k_n <= s_n) & (((s_n - k_n) & (dil - 1)) == 0) for _, dil in ATTN_GROUPS]
    for h in range(N_HEADS):
        hs = slice(h * HEAD_DIM, (h + 1) * HEAD_DIM)
        scores, new_scores, new_vals = [], [], []
        for g in range(N_GROUPS):
            base = g * 3 * GROUP_W
            q = (qkv[:, base + h * HEAD_DIM:base + (h + 1) * HEAD_DIM] * ATTN_SCALE).astype(bf16)
            kn = qkv[:, base + GROUP_W + h * HEAD_DIM:base + GROUP_W + (h + 1) * HEAD_DIM].astype(bf16)
            new_vals.append(qkv[:, base + 2 * GROUP_W + h * HEAD_DIM:base + 2 * GROUP_W + (h + 1) * HEAD_DIM]
                            .astype(bf16))
            scores.append(jnp.where(masks[g], _dot(q, caches[g][0, 0, h].astype(bf16)), NEG))
            new_scores.append(jnp.where(new_masks[g], _dot_nt(q, kn), NEG))
        m = scores[0].max(axis=-1, keepdims=True)
        for sc in scores[1:] + new_scores:
            m = jnp.maximum(m, sc.max(axis=-1, keepdims=True))
        den = jnp.zeros((n_new, 1), f32)
        o = jnp.zeros((n_new, HEAD_DIM), f32)
        for g in range(N_GROUPS):
            p = jnp.exp(scores[g] - m)
            pn = jnp.exp(new_scores[g] - m)
            den = den + jnp.sum(p, axis=-1, keepdims=True) + jnp.sum(pn, axis=-1, keepdims=True)
            o = o + _dot_nt(p.astype(bf16), caches[g][0, 1, h].astype(bf16)) + _dot(pn.astype(bf16), new_vals[g])
        a_ref[:, hs] = o / den


def _attn_sample(qkv, caches, bsz, n_new):
    views, specs = [], []
    for g, (window, dil) in enumerate(ATTN_GROUPS):
        assert caches[g].shape[1] == window == BAND * dil and dil & (dil - 1) == 0
        views.append(jnp.transpose(caches[g], (0, 2, 3, 4, 1)))
        specs.append(pl.BlockSpec((1, 2, N_HEADS, HEAD_DIM, window), lambda b: (b, 0, 0, 0, 0)))
    return pl.pallas_call(
        functools.partial(_attn_sample_body, n_new=n_new),
        grid=(bsz,),
        in_specs=[pl.BlockSpec((n_new, QKV_W), lambda b: (b, 0))] + specs,
        out_specs=pl.BlockSpec((n_new, GROUP_W), lambda b: (b, 0)),
        out_shape=jax.ShapeDtypeStruct((bsz * n_new, GROUP_W), f32),
        compiler_params=_params("parallel"),
        name="attn_sample",
    )(qkv, *views)


def _ssd_conv(ext_ref, n, cw_ref, cb_ref):
    y = cb_ref[...]
    for j in range(CONV_W):
        off = SUBLANES - (CONV_W - 1) + j
        y = y + ext_ref[off:off + n, :] * cw_ref[j:j + 1, :]
    return _silu(y)


def _gate_norm(y, z, g, d_inner):
    y = y * _silu(z.astype(f32))
    gw = d_inner // SSM_GROUPS
    parts = []
    for gi in range(SSM_GROUPS):
        yg = y[:, gi * gw:(gi + 1) * gw]
        parts.append(yg * lax.rsqrt(jnp.mean(yg * yg, axis=-1, keepdims=True) + RMS_EPS))
    return jnp.concatenate(parts, axis=-1) * g


def _ssd_prompt_body(xbc_ref, z_ref, dt_ref, cw_ref, cb_ref, dtb_ref, alog_ref, dtbt_ref, alogt_ref,
                     dskip_ref, ng_ref, ex_ref, s_ref, hl_ref, ext_ref, ht_ref, y_ref, *, d_inner, n_heads):
    c = pl.program_id(1)
    q = xbc_ref.shape[0]
    gw = d_inner // SSM_GROUPS
    hpg = n_heads // SSM_GROUPS

    @pl.when(c == 0)
    def _():
        ext_ref[0:SUBLANES, :] = jnp.zeros((SUBLANES, ext_ref.shape[1]), f32)
        ht_ref[...] = jnp.zeros_like(ht_ref)

    @pl.when(c > 0)
    def _():
        ext_ref[0:SUBLANES, :] = ext_ref[q:q + SUBLANES, :]

    ext_ref[SUBLANES:SUBLANES + q, :] = xbc_ref[...]
    xc = _ssd_conv(ext_ref, q, cw_ref, cb_ref)
    xs = xc[:, :d_inner]
    gn = SSM_GROUPS * D_STATE
    bm = xc[:, d_inner:d_inner + gn].astype(bf16)
    cm = xc[:, d_inner + gn:].astype(bf16)

    dtr = dt_ref[...]
    dt = _softplus(dtr + dtb_ref[...])
    da = dt * (-jnp.exp(alog_ref[...]))
    dt_t = _softplus(dtr.T[0:n_heads, :] + dtbt_ref[...])
    da_t = dt_t * (-jnp.exp(alogt_ref[...]))
    ii = lax.broadcasted_iota(jnp.int32, (q, q), 0)
    jj = lax.broadcasted_iota(jnp.int32, (q, q), 1)
    causal = ii >= jj
    lower = causal.astype(bf16)
    upper = (ii <= jj).astype(bf16)
    cum = _dot3_rhs(lower, da)
    cum_t = _dot3_lhs(da_t, upper)
    ex = ex_ref[...]
    cumx = _dot3_lhs(cum, ex)
    dtx = _dot3_lhs(dt, ex)
    clx = cumx[q - 1:q, :]
    ecum = jnp.exp(cumx)
    xd = (jnp.exp(clx - cumx) * dtx * xs).astype(bf16)
    xsb = xs.astype(bf16)

    for g in range(SSM_GROUPS):
        gl = slice(g * gw, (g + 1) * gw)
        cg = cm[:, g * D_STATE:(g + 1) * D_STATE]
        bg = bm[:, g * D_STATE:(g + 1) * D_STATE]
        cb = _dot_nt(cg, bg)
        h_old = ht_ref[g]
        y_ref[:, gl] = ecum[:, gl] * _dot(cg, h_old.astype(bf16))
        ht_ref[g] = jnp.exp(clx[:, gl]) * h_old + _dot_tn(bg, xd[:, gl])
        for e in range(hpg):
            hd = g * hpg + e
            hl = slice(hd * SSM_HEAD_DIM, (hd + 1) * SSM_HEAD_DIM)
            seg = jnp.broadcast_to(cum[:, hd:hd + 1], (q, q)) - jnp.broadcast_to(cum_t[hd:hd + 1, :], (q, q))
            w = cb * jnp.exp(jnp.where(causal, seg, NEG)) * jnp.broadcast_to(dt_t[hd:hd + 1, :], (q, q))
            y_ref[:, hl] += _dot(w.astype(bf16), xsb[:, hl])

    y = y_ref[...] + dskip_ref[...] * xs
    s_ref[...] = _gate_norm(y, z_ref[...], ng_ref[...], d_inner).astype(s_ref.dtype)

    @pl.when(c == pl.num_programs(1) - 1)
    def _():
        for g in range(SSM_GROUPS):
            hl_ref[0, g * gw:(g + 1) * gw, :] = ht_ref[g].T


def _ssd_prompt(xbc, z, dt, prm, bsz, seq, *, chunk):
    n, d_xbc = xbc.shape
    d_inner = z.shape[1]
    n_heads = d_inner // SSM_HEAD_DIM
    nc = seq // chunk
    gw = d_inner // SSM_GROUPS

    def row(w):
        return pl.BlockSpec((chunk, w), lambda b, c: (b * nc + c, 0))

    def full(a):
        return pl.BlockSpec(a.shape, lambda b, c: (0,) * a.ndim)

    consts = [prm["conv_w"], prm["conv_b"], prm["dtb_row"], prm["alog_row"], prm["dtb_t"], prm["alog_t"],
              prm["dskip_x"], prm["norm_g"], prm["ex"]]
    return pl.pallas_call(
        functools.partial(_ssd_prompt_body, d_inner=d_inner, n_heads=n_heads),
        grid=(bsz, nc),
        in_specs=[row(d_xbc), row(d_inner), row(LANES)] + [full(a) for a in consts],
        out_specs=(row(d_inner), pl.BlockSpec((1, d_inner, D_STATE), lambda b, c: (b, 0, 0))),
        out_shape=(jax.ShapeDtypeStruct((n, d_inner), bf16),
                   jax.ShapeDtypeStruct((bsz, d_inner, D_STATE), f32)),
        scratch_shapes=[pltpu.VMEM((SUBLANES + chunk + SUBLANES, d_xbc), f32),
                        pltpu.VMEM((SSM_GROUPS, D_STATE, gw), f32),
                        pltpu.VMEM((chunk, d_inner), f32)],
        compiler_params=_params("parallel", "arbitrary"),
        name="ssd_prompt",
    )(xbc, z, dt, *consts)


def _ssd_sample_body(xbc_ref, cs_ref, z_ref, dt_ref, h_ref, cw_ref, cb_ref, dtb_ref, alog_ref,
                     dskip_ref, ng_ref, ex_ref, rsel_ref, s_ref, hn_ref, ext_ref, *, d_inner, n_heads):
    q = xbc_ref.shape[0]
    gw = d_inner // SSM_GROUPS
    ext_ref[0:SUBLANES, :] = cs_ref[0]
    ext_ref[SUBLANES:SUBLANES + q, :] = xbc_ref[...]
    xc = _ssd_conv(ext_ref, q, cw_ref, cb_ref)
    xs = xc[:, :d_inner]
    gn = SSM_GROUPS * D_STATE
    bm = xc[:, d_inner:d_inner + gn]
    cm = xc[:, d_inner + gn:]

    dt = _softplus(dt_ref[...] + dtb_ref[...])
    da = dt * (-jnp.exp(alog_ref[...]))
    row = lax.broadcasted_iota(jnp.int32, (q, LANES), 0)
    cum = jnp.zeros((q, LANES), f32)
    for j in range(q):
        cum = cum + jnp.where(row >= j, da[j:j + 1, :], 0.0)
    ex = ex_ref[...]
    cumx = _dot3_lhs(cum, ex)
    dtx = _dot3_lhs(dt, ex)
    clx = cumx[q - 1:q, :]
    rowx = lax.broadcasted_iota(jnp.int32, (q, d_inner), 0)

    y = dskip_ref[...] * xs
    for j in range(q):
        prod = cm * bm[j:j + 1, :]
        cbx = jnp.concatenate(
            [jnp.broadcast_to(jnp.sum(prod[:, g * D_STATE:(g + 1) * D_STATE], axis=-1, keepdims=True), (q, gw))
             for g in range(SSM_GROUPS)], axis=-1)
        seg = jnp.where(rowx >= j, cumx - cumx[j:j + 1, :], NEG)
        y = y + cbx * jnp.exp(seg) * (dtx[j:j + 1, :] * xs[j:j + 1, :])

    xd = (jnp.exp(clx - cumx) * dtx * xs).astype(bf16)
    ones = jnp.ones((q, LANES), bf16)
    da_hi, da_mid, da_lo = _split3(da)
    cl_b = _dot_tn(da_hi, ones) + _dot_tn(da_mid, ones) + _dot_tn(da_lo, ones)
    decay = jnp.exp(_dot3_rhs(rsel_ref[...], cl_b))
    ecum = jnp.exp(cumx)
    cmb = cm.astype(bf16)
    bmb = bm.astype(bf16)
    ys = []
    for g in range(SSM_GROUPS):
        rows = slice(g * gw, (g + 1) * gw)
        hg = h_ref[0, rows, :]
        ys.append(_dot_nt(cmb[:, g * D_STATE:(g + 1) * D_STATE], hg.astype(bf16)))
        hn_ref[0, rows, :] = decay[rows, :] * hg + _dot_tn(xd[:, rows], bmb[:, g * D_STATE:(g + 1) * D_STATE])
    y = y + ecum * jnp.concatenate(ys, axis=-1)
    s_ref[...] = _gate_norm(y, z_ref[...], ng_ref[...], d_inner)


def _ssd_sample(xbc, conv_state8, z, dt, h0, prm, bsz, n_new):
    n, d_xbc = xbc.shape
    d_inner = z.shape[1]
    n_heads = d_inner // SSM_HEAD_DIM

    def row(w):
        return pl.BlockSpec((n_new, w), lambda b: (b, 0))

    def full(a):
        return pl.BlockSpec(a.shape, lambda b: (0,) * a.ndim)

    consts = [prm["conv_w"], prm["conv_b"], prm["dtb_row"], prm["alog_row"],
              prm["dskip_x"], prm["norm_g"], prm["ex"], prm["rsel"]]
    state = pl.BlockSpec((1, d_inner, D_STATE), lambda b: (b, 0, 0))
    return pl.pallas_call(
        functools.partial(_ssd_sample_body, d_inner=d_inner, n_heads=n_heads),
        grid=(bsz,),
        in_specs=[row(d_xbc), pl.BlockSpec((1, SUBLANES, d_xbc), lambda b: (b, 0, 0)), row(d_inner), row(LANES),
                  state] + [full(a) for a in consts],
        out_specs=(row(d_inner), state),
        out_shape=(jax.ShapeDtypeStruct((n, d_inner), f32),
                   jax.ShapeDtypeStruct((bsz, d_inner, D_STATE), f32)),
        scratch_shapes=[pltpu.VMEM((2 * SUBLANES, d_xbc), f32)],
        compiler_params=_params("parallel"),
        name="ssd_sample",
    )(xbc, conv_state8, z, dt, h0, *consts)


def _mixer_body(a_ref, s_ref, ga_ref, gs_ref, xn_ref, gt_ref, sc_ref, sh_ref, wa_ref, ws_ref, wo_ref,
                lg_ref, lb_ref, wrh_ref, wrl_ref, br_ref, x1_ref, u2_ref, lo_ref):
    m = (jax.nn.sigmoid(ga_ref[...].astype(f32)) * _dot(a_ref[...].astype(bf16), wa_ref[...])
         + jax.nn.sigmoid(gs_ref[...].astype(f32)) * _dot(s_ref[...].astype(bf16), ws_ref[...]))
    o = _dot(m.astype(bf16), wo_ref[...])
    x1 = _layer_norm(DN_ALPHA * xn_ref[...] + gt_ref[0] * o, lg_ref[...], lb_ref[...])
    x1_ref[...] = x1
    u2 = x1 * (1.0 + sc_ref[0]) + sh_ref[0]
    _store_row_tiles(u2_ref, u2)
    hi, mid, lo = _split3(u2)
    wrh = wrh_ref[...]
    wrl = wrl_ref[...]
    lo_ref[...] = (_dot(hi, wrh) + (_dot(hi, wrl) + _dot(mid, wrh)) + (_dot(mid, wrl) + _dot(lo, wrh))
                   + br_ref[...])


def _mixer(a, s, gates, xn, gt, sc, sh, wts, *, tm, tiles_per_mod, name):
    n, d = xn.shape

    def row(w, col=0):
        return pl.BlockSpec((tm, w), lambda i: (i, col))

    def full(arr):
        return pl.BlockSpec(arr.shape, lambda i: (0,) * arr.ndim)

    consts = [wts["w_attn_br"], wts["w_ssd_br"], wts["w_out"], wts["ln_mix_g"], wts["ln_mix_b"],
              wts["w_router_hi"], wts["w_router_lo"], wts["b_router"]]
    return pl.pallas_call(
        _mixer_body,
        grid=(n // tm,),
        in_specs=[row(a.shape[1]), row(s.shape[1]), row(d, 0), row(d, 1), row(d),
                  _mod_spec(gt, tm, tiles_per_mod), _mod_spec(sc, tm, tiles_per_mod),
                  _mod_spec(sh, tm, tiles_per_mod)] + [full(c) for c in consts],
        out_specs=(row(d), pl.BlockSpec((tm * (d // LANES), LANES), lambda i: (i, 0)), row(LANES)),
        out_shape=(jax.ShapeDtypeStruct((n, d), f32), jax.ShapeDtypeStruct((n * (d // LANES), LANES), f32),
                   jax.ShapeDtypeStruct((n, LANES), f32)),
        compiler_params=_params("parallel"),
        name=name,
    )(a, s, gates, gates, xn, gt, sc, sh, *consts)


def _moe_body(be_ref, nv_ref, nr_ref, tok_ref, tok_next_ref, slot_prev_ref, slot_ref, x_hbm, wg_ref, bg_ref, wu_ref,
              bu_ref, wd_ref, bd_ref, y_hbm, xbuf, ybuf, wgb, wub, wdb, xbb, acc, sem_in, sem_out, *, bm):
    i = pl.program_id(0)
    nv = nv_ref[0]
    nc = xbuf.shape[1] // bm
    n_chunks = wgb.shape[1] // MOE_FF_CHUNK
    share = bm // n_chunks
    cur = i % 2
    nxt = 1 - cur

    def gather_row(idx_ref, buf, r, priority):
        src = pl.multiple_of(idx_ref[0, 0, r], nc)
        pltpu.make_async_copy(x_hbm.at[pl.ds(src, nc), :], xbuf.at[buf, pl.ds(pl.multiple_of(r * nc, nc), nc), :],
                              sem_in.at[buf]).start(priority=priority)

    def scatter_row(sref, buf, r):
        dst = pl.multiple_of(sref[0, 0, r], nc)
        return pltpu.make_async_copy(ybuf.at[buf, pl.ds(pl.multiple_of(r * nc, nc), nc), :],
                                     y_hbm.at[pl.ds(dst, nc), :], sem_out.at[buf])

    def wait_gather(buf):
        pltpu.make_async_copy(x_hbm.at[pl.ds(0, bm * nc), :], xbuf.at[buf], sem_in.at[buf]).wait()

    def start_scatter_rows(sref, buf, n_rows):
        def body(r, carry):
            scatter_row(sref, buf, r).start()
            return carry
        lax.fori_loop(0, n_rows, body, 0)

    def wait_scatter(buf, n_real):
        @pl.when(n_real == bm)
        def _():
            pltpu.make_async_copy(ybuf.at[buf], y_hbm.at[pl.ds(0, bm * nc), :], sem_out.at[buf]).wait()

        @pl.when(n_real < bm)
        def _():
            def body(r, carry):
                scatter_row(slot_ref, buf, 0).wait()
                return carry
            lax.fori_loop(0, n_real, body, 0)

    def ffn(with_prev_scatter):
        acc[...] = jnp.broadcast_to(bd_ref[0], acc.shape)

        def chunk(j, carry):
            cols = pl.ds(pl.multiple_of(j * MOE_FF_CHUNK, MOE_FF_CHUNK), MOE_FF_CHUNK)
            xb = xbb[...]
            gate = jnp.minimum(_dot(xb, wgb[:, cols]) + bg_ref[0, :, cols], SWIGLU_LIMIT)
            up = jnp.clip(_dot(xb, wub[:, cols]) + bu_ref[0, :, cols], -SWIGLU_LIMIT, SWIGLU_LIMIT)
            h = gate * jax.nn.sigmoid(SWIGLU_ALPHA * gate) * (up + 1.0)
            acc[...] += _dot(h.astype(bf16), wdb[cols, :])
            for k in range(share):
                r = j * share + k
                gather_row(tok_next_ref, nxt, r, k % 2)
                if with_prev_scatter:
                    scatter_row(slot_prev_ref, nxt, r).start()
            return carry

        lax.fori_loop(0, n_chunks, chunk, 0)

    @pl.when((i == 0) & (nv > 0))
    def _():
        def body(r2, carry):
            gather_row(tok_ref, cur, 2 * r2, 0)
            gather_row(tok_ref, cur, 2 * r2 + 1, 1)
            return carry
        lax.fori_loop(0, bm // 2, body, 0, unroll=DMA_ISSUE_UNROLL // 2)

    @pl.when(i < nv)
    def _():
        @pl.when((i == 0) | (be_ref[i] != be_ref[jnp.maximum(i - 1, 0)]))
        def _():
            wgb[...] = wg_ref[0].astype(bf16)
            wub[...] = wu_ref[0].astype(bf16)
            wdb[...] = wd_ref[0].astype(bf16)

        wait_gather(cur)

        @pl.when(i >= 2)
        def _():
            wait_scatter(cur, nr_ref[jnp.maximum(i - 2, 0)])

        xbb[...] = _load_row_tiles(xbuf, bm, nc, (cur,)).astype(bf16)
        n_prev = jnp.where(i >= 1, nr_ref[jnp.maximum(i - 1, 0)], 0)

        @pl.when(n_prev == bm)
        def _():
            ffn(True)

        @pl.when(n_prev < bm)
        def _():
            start_scatter_rows(slot_prev_ref, nxt, n_prev)
            ffn(False)

        _store_row_tiles(ybuf, acc[...], (cur,))

        @pl.when(i == nv - 1)
        def _():
            wait_gather(nxt)
            wait_scatter(nxt, n_prev)
            start_scatter_rows(slot_ref, cur, nr_ref[i])
            wait_scatter(cur, nr_ref[i])


def _moe(u2, logits, wts, *, bm):
    d_ff, d = wts["w_down"].shape[1:]
    nc = d // LANES
    n_tok = u2.shape[0] // nc
    n_assign = n_tok * TOP_K
    i32 = jnp.int32
    top_logit, top_idx = lax.top_k(logits[:, :N_EXPERTS], TOP_K)
    top_gate = jax.nn.softmax(top_logit, axis=-1)
    e_flat = top_idx.reshape(-1)
    order = jnp.argsort(e_flat).astype(i32)
    experts = jnp.arange(N_EXPERTS, dtype=i32)
    counts = jnp.sum((e_flat[:, None] == experts[None, :]).astype(i32), axis=0)
    starts = jnp.cumsum(counts) - counts
    padded = (counts + bm - 1) // bm * bm
    pad_ends = jnp.cumsum(padded)
    pad_starts = pad_ends - padded
    n_blocks = -(-n_assign // bm) + N_EXPERTS
    blk_row = jnp.arange(n_blocks, dtype=i32) * bm
    block_expert = jnp.minimum(jnp.sum((pad_ends[None, :] <= blk_row[:, None]).astype(i32), axis=1), N_EXPERTS - 1)
    off = blk_row - pad_starts[block_expert]
    n_real = jnp.clip(counts[block_expert] - off, 0, bm)
    j = jnp.arange(bm, dtype=i32)[None, :]
    src = jnp.clip(starts[block_expert][:, None] + off[:, None] + j, 0, n_assign - 1)
    a_id = jnp.take(order, src.reshape(-1), axis=0).reshape(n_blocks, bm)
    real = j < n_real[:, None]
    row_tok = jnp.where(real, a_id // TOP_K, 0) * nc
    row_slot = jnp.where(real, (a_id % TOP_K) * n_tok + a_id // TOP_K, 0) * nc
    n_valid = (pad_ends[-1] // bm).astype(i32).reshape(1)

    def wspec(shape):
        return pl.BlockSpec((1,) + shape, lambda i, be, nv, nr: (be[i], 0, 0))

    idx_spec = pl.BlockSpec((1, 1, bm), lambda i, be, nv, nr: (i, 0, 0), memory_space=pltpu.SMEM)
    idx_next = pl.BlockSpec((1, 1, bm), lambda i, be, nv, nr: (jnp.minimum(i + 1, n_blocks - 1), 0, 0),
                            memory_space=pltpu.SMEM)
    idx_prev = pl.BlockSpec((1, 1, bm), lambda i, be, nv, nr: (jnp.maximum(i - 1, 0), 0, 0), memory_space=pltpu.SMEM)
    grid_spec = pltpu.PrefetchScalarGridSpec(
        num_scalar_prefetch=3,
        grid=(n_blocks,),
        in_specs=[idx_spec, idx_next, idx_prev, idx_spec,
                  pl.BlockSpec(memory_space=pl.ANY),
                  wspec((d, d_ff)), wspec((1, d_ff)), wspec((d, d_ff)), wspec((1, d_ff)),
                  wspec((d_ff, d)), wspec((1, d))],
        out_specs=pl.BlockSpec(memory_space=pl.ANY),
        scratch_shapes=[pltpu.VMEM((2, bm * nc, LANES), f32), pltpu.VMEM((2, bm * nc, LANES), f32),
                        pltpu.VMEM((d, d_ff), bf16), pltpu.VMEM((d, d_ff), bf16), pltpu.VMEM((d_ff, d), bf16),
                        pltpu.VMEM((bm, d), bf16), pltpu.VMEM((bm, d), f32),
                        pltpu.SemaphoreType.DMA((2,)), pltpu.SemaphoreType.DMA((2,))],
    )
    row_tok3 = row_tok.reshape(n_blocks, 1, bm)
    row_slot3 = row_slot.reshape(n_blocks, 1, bm)
    y = pl.pallas_call(
        functools.partial(_moe_body, bm=bm),
        grid_spec=grid_spec,
        out_shape=jax.ShapeDtypeStruct((n_assign * nc, LANES), f32),
        compiler_params=_params("arbitrary"),
        name=f"moe_bm{bm}",
    )(block_expert, n_valid, n_real, row_tok3, row_tok3, row_slot3, row_slot3, u2,
      wts["w_gate"], wts["b_gate"], wts["w_up"], wts["b_up"], wts["w_down"], wts["b_down"])
    return y, top_gate


def _final_body(x1_ref, *rest):
    yk_refs, (pg_ref, gt_ref, g_ref, b_ref, o_ref) = rest[:TOP_K], rest[TOP_K:]
    pg = pg_ref[...]
    tm, d = x1_ref.shape
    y = pg[:, 0:1] * _load_row_tiles(yk_refs[0], tm, d // LANES)
    for k in range(1, TOP_K):
        y = y + pg[:, k:k + 1] * _load_row_tiles(yk_refs[k], tm, d // LANES)
    o_ref[...] = _layer_norm(DN_ALPHA * x1_ref[...] + gt_ref[0] * y, g_ref[...], b_ref[...])


def _final(x1, yk, top_gate, gt, g, b, *, tm, tiles_per_mod, row0, n_all, name):
    n, d = x1.shape
    nt = n // tm
    assert row0 % tm == 0 and n_all % tm == 0
    nt_all, blk0 = n_all // tm, row0 // tm
    row = pl.BlockSpec((tm, d), lambda i: (i, 0))
    vec = pl.BlockSpec((1, d), lambda i: (0, 0))
    planes = [pl.BlockSpec((tm * (d // LANES), LANES),
                           functools.partial(lambda i, k: (k * nt_all + blk0 + i, 0), k=k)) for k in range(TOP_K)]
    return pl.pallas_call(
        _final_body,
        grid=(nt,),
        in_specs=[row] + planes + [pl.BlockSpec((tm, TOP_K), lambda i: (i, 0)),
                                   _mod_spec(gt, tm, tiles_per_mod), vec, vec],
        out_specs=row,
        out_shape=jax.ShapeDtypeStruct((n, d), f32),
        compiler_params=_params("parallel"),
        name=name,
    )(x1, *([yk] * TOP_K), top_gate, gt, g, b)


def _stream(x, c, wts, ssd_prm, *, caches=None, conv_state=None, ssm_state=None):
    bsz, seq, d = x.shape
    n = bsz * seq
    is_prompt = caches is None
    d_inner = wts["w_z"].shape[1]
    d_xbc = wts["w_xbc"].shape[1]

    c_pad = jnp.pad(c, ((0, (-bsz) % SUBLANES), (0, 0)))
    mod = _mm(c_pad, wts["w_ada"], wts["b_ada"], tm=c_pad.shape[0], tn=2 * d, act="silu", name="ada_mod")[:bsz]
    sh1, sc1, gt1, sh2, sc2, gt2 = jnp.split(mod, 6, axis=-1)

    def per_tile(p, tile):
        if seq % tile == 0:
            return p.reshape(bsz, 1, d), seq // tile
        return jnp.broadcast_to(p[:, None, :], (bsz, seq, d)).reshape(n // tile, tile, d), 1

    tm = 512
    tpm = per_tile(sc1, tm)[1]
    xn, u = _ln_mod(x.reshape(n, d), wts["ln_emb_g"], wts["ln_emb_b"], per_tile(sc1, tm)[0], per_tile(sh1, tm)[0],
                    tm=tm, tiles_per_mod=tpm, name="ln_mod")
    tmm = min(n, 2048)
    act_dtype = bf16 if seq % 16 == 0 else f32
    qkv = _mm(u, wts["w_qkv"], wts["zero_b"][:, :QKV_W], tm=tmm, tn=512, name="proj_qkv")
    z = _mm(u, wts["w_z"], wts["zero_b"][:, :d_inner], tm=tmm, tn=512, out_dtype=act_dtype, name="proj_z")
    xbc = _mm(u, wts["w_xbc"], wts["zero_b"][:, :d_xbc], tm=tmm, tn=512, name="proj_xbc")
    dt = _mm(u, wts["w_dt"], wts["zero_b"][:, :LANES], tm=tmm, tn=LANES, name="proj_dt")
    gates = _mm(u, wts["w_gates"], wts["zero_b"][:, :2 * d], tm=tmm, tn=512, out_dtype=bf16, name="proj_gates")

    qkv3 = qkv.reshape(bsz, seq, QKV_W)
    xbc3 = xbc.reshape(bsz, seq, d_xbc)

    def kv_rows_of(g, keep):
        cols = qkv3[:, seq - keep:, (g * 3 + 1) * GROUP_W:(g * 3 + 3) * GROUP_W]
        return cols.reshape(bsz, keep, 2, N_HEADS, HEAD_DIM)

    if is_prompt:
        a = _attn_prompt(qkv, bsz, seq)
        s, h_last = _ssd_prompt(xbc, z, dt, ssd_prm, bsz, seq, chunk=128)
        kv_rows = [kv_rows_of(g, min(w, seq)) for g, (w, _) in enumerate(ATTN_GROUPS)]
        new_conv = xbc3[:, -(CONV_W - 1):]
    else:
        a = _attn_sample(qkv, caches, bsz, seq)
        cs8 = jnp.pad(conv_state, ((0, 0), (SUBLANES - (CONV_W - 1), 0), (0, 0)))
        s, h_last = _ssd_sample(xbc, cs8, z, dt, ssm_state.reshape(bsz, d_inner, D_STATE), ssd_prm, bsz, seq)
        kv_rows = [kv_rows_of(g, seq) for g in range(N_GROUPS)]
        new_conv = jnp.concatenate([conv_state, xbc3], axis=1)[:, -(CONV_W - 1):]
    new_ssm = h_last.reshape(bsz, d_inner // SSM_HEAD_DIM, SSM_HEAD_DIM, D_STATE)

    tmx = 512
    x1, u2, logits = _mixer(a, s, gates, xn, per_tile(gt1, tmx)[0], per_tile(sc2, tmx)[0], per_tile(sh2, tmx)[0],
                            wts, tm=tmx, tiles_per_mod=per_tile(gt1, tmx)[1], name="mixer")

    def finish(yk, top_gate, row0, n_all):
        y = _final(x1, yk, top_gate[row0:row0 + n], per_tile(gt2, tm)[0], wts["ln_ffn_g"], wts["ln_ffn_b"], tm=tm,
                   tiles_per_mod=tpm, row0=row0, n_all=n_all, name="final_ln")
        return y.reshape(bsz, seq, d)

    return (u2, logits, finish), kv_rows, new_conv, new_ssm


def kernel(x_prompt, x_sample, c_prompt, c_sample, cache_kv_w128, cache_kv_w512, cache_kv_w2048, state_conv, state_ssm, ln_emb_g, ln_emb_b, w_ada, b_ada, w_in, conv_w, conv_b, dt_bias, a_log, d_skip, ssm_norm_g, w_attn_br, w_ssd_br, w_out, ln_mix_g, ln_mix_b, w_router, b_router, w_gate, b_gate, w_up, b_up, w_down, b_down, ln_ffn_g, ln_ffn_b):
    d = x_prompt.shape[-1]
    d_inner = ssm_norm_g.shape[-1]
    d_xbc = conv_w.shape[-1]
    n_heads = dt_bias.shape[-1]
    lyr = 0

    def rowv(v):
        return v.reshape(1, -1).astype(f32)

    def lane_pad(v):
        return jnp.pad(v.astype(f32), (0, LANES - n_heads)).reshape(1, LANES)

    wi = w_in[lyr]
    o0 = QKV_W
    o1 = o0 + d_inner
    o2 = o1 + d_xbc
    o3 = o2 + n_heads
    wr = jnp.pad(w_router[lyr], ((0, 0), (0, LANES - N_EXPERTS)))
    wr_hi = wr.astype(bf16)
    wts = {
        "ln_emb_g": rowv(ln_emb_g), "ln_emb_b": rowv(ln_emb_b),
        "w_ada": w_ada[lyr].astype(bf16), "b_ada": rowv(b_ada[lyr]),
        "w_qkv": wi[:, :o0].astype(bf16), "w_z": wi[:, o0:o1].astype(bf16), "w_xbc": wi[:, o1:o2].astype(bf16),
        "w_dt": jnp.pad(wi[:, o2:o3], ((0, 0), (0, LANES - n_heads))).astype(bf16),
        "w_gates": wi[:, o3:].astype(bf16),
        "zero_b": jnp.zeros((1, max(QKV_W, d_xbc, 2 * d)), f32),
        "w_attn_br": w_attn_br[lyr].astype(bf16), "w_ssd_br": w_ssd_br[lyr].astype(bf16),
        "w_out": w_out[lyr].astype(bf16),
        "ln_mix_g": rowv(ln_mix_g[lyr]), "ln_mix_b": rowv(ln_mix_b[lyr]),
        "w_router_hi": wr_hi, "w_router_lo": (wr - wr_hi.astype(f32)).astype(bf16),
        "b_router": jnp.pad(b_router[lyr], (0, LANES - N_EXPERTS)).reshape(1, LANES),
        "w_gate": w_gate[lyr], "b_gate": b_gate[lyr][:, None, :],
        "w_up": w_up[lyr], "b_up": b_up[lyr][:, None, :],
        "w_down": w_down[lyr], "b_down": b_down[lyr][:, None, :],
        "ln_ffn_g": rowv(ln_ffn_g[lyr]), "ln_ffn_b": rowv(ln_ffn_b[lyr]),
    }
    head_of_lane = jnp.arange(d_inner, dtype=jnp.int32) // SSM_HEAD_DIM
    ssd_prm = {
        "conv_w": conv_w[lyr], "conv_b": rowv(conv_b[lyr]),
        "dtb_row": lane_pad(dt_bias[lyr]), "alog_row": lane_pad(a_log[lyr]),
        "dtb_t": jnp.broadcast_to(dt_bias[lyr][:, None], (n_heads, LANES)),
        "alog_t": jnp.broadcast_to(a_log[lyr][:, None], (n_heads, LANES)),
        "dskip_x": jnp.repeat(d_skip[lyr], SSM_HEAD_DIM).reshape(1, d_inner),
        "norm_g": rowv(ssm_norm_g[lyr]),
        "ex": (jnp.arange(LANES, dtype=jnp.int32)[:, None] == head_of_lane[None, :]).astype(bf16),
        "rsel": (head_of_lane[:, None] == jnp.arange(LANES, dtype=jnp.int32)[None, :]).astype(bf16),
    }

    (u2_p, logits_p, finish_p), kv_p, conv_p, ssm_p = _stream(x_prompt, c_prompt, wts, ssd_prm)
    (u2_s, logits_s, finish_s), kv_s, conv_s, ssm_s = _stream(
        x_sample, c_sample, wts, ssd_prm,
        caches=(cache_kv_w128[lyr], cache_kv_w512[lyr], cache_kv_w2048[lyr]),
        conv_state=state_conv[lyr], ssm_state=state_ssm[lyr])
    n_p, n_s = logits_p.shape[0], logits_s.shape[0]
    yk, top_gate = _moe(jnp.concatenate([u2_p, u2_s], axis=0), jnp.concatenate([logits_p, logits_s], axis=0), wts,
                        bm=MOE_BLOCK_ROWS)
    yp = finish_p(yk, top_gate, 0, n_p + n_s)
    ys = finish_s(yk, top_gate, n_p, n_p + n_s)
    return (yp, ys, kv_p[0][None], kv_s[0][None], kv_p[1][None], kv_s[1][None], kv_p[2][None], kv_s[2][None],
            conv_p[None], conv_s[None], ssm_p[None], ssm_s[None])
```

```python
import functools
import math

import jax
import jax.numpy as jnp
from jax import lax
from jax.experimental import pallas as pl
from jax.experimental.pallas import tpu as pltpu

f32 = jnp.float32
bf16 = jnp.bfloat16

ATTN_GROUPS = ((128, 1), (512, 4), (2048, 16))
N_GROUPS = len(ATTN_GROUPS)
N_HEADS = 8
HEAD_DIM = 64
GROUP_W = N_HEADS * HEAD_DIM
QKV_W = N_GROUPS * 3 * GROUP_W
BAND = 128
ATTN_SCALE = HEAD_DIM ** -0.5
SSM_HEAD_DIM = 64
SSM_GROUPS = 4
D_STATE = 128
CONV_W = 4
N_EXPERTS = 32
TOP_K = 4
SWIGLU_ALPHA = 1.702
SWIGLU_LIMIT = 7.0
LN_EPS = 1e-5
RMS_EPS = 1e-5
DEPTH = 1
DN_ALPHA = (2 * DEPTH) ** 0.25
NEG = -1e30

LANES = 128
SUBLANES = 8
VMEM_LIMIT = 56 * 1024 * 1024
MOE_BLOCK_ROWS = 256
DMA_ISSUE_UNROLL = 32


def _params(*sem):
    return pltpu.CompilerParams(dimension_semantics=sem, vmem_limit_bytes=VMEM_LIMIT)


def _silu(x):
    return x * jax.nn.sigmoid(x)


def _softplus(x):
    return jnp.maximum(x, 0.0) + jnp.log(1.0 + jnp.exp(-jnp.abs(x)))


def _split3(x):
    hi = x.astype(bf16)
    r1 = x - hi.astype(f32)
    mid = r1.astype(bf16)
    lo = (r1 - mid.astype(f32)).astype(bf16)
    return hi, mid, lo


def _dot(a, b):
    return jnp.dot(a, b, preferred_element_type=f32)


def _dot_nt(a, b):
    return lax.dot_general(a, b, (((1,), (1,)), ((), ())), preferred_element_type=f32)


def _dot_tn(a, b):
    return lax.dot_general(a, b, (((0,), (0,)), ((), ())), preferred_element_type=f32)


def _dot3_lhs(x, w):
    hi, mid, lo = _split3(x)
    return _dot(hi, w) + _dot(mid, w) + _dot(lo, w)


def _dot3_rhs(w, x):
    hi, mid, lo = _split3(x)
    return _dot(w, hi) + _dot(w, mid) + _dot(w, lo)


def _layer_norm(x, g, b):
    mu = jnp.mean(x, axis=-1, keepdims=True)
    xc = x - mu
    var = jnp.mean(xc * xc, axis=-1, keepdims=True)
    return xc * lax.rsqrt(var + LN_EPS) * g + b


def _store_row_tiles(ref, x, lead=()):
    rows, d = x.shape
    nc = d // LANES
    for c in range(nc):
        ref[lead + (pl.ds(c, rows, stride=nc), slice(None))] = x[:, c * LANES:(c + 1) * LANES]


def _load_row_tiles(ref, rows, nc, lead=()):
    return jnp.concatenate([ref[lead + (pl.ds(c, rows, stride=nc), slice(None))] for c in range(nc)], axis=-1)


def _mm_body(a_ref, w_ref, b_ref, o_ref, *, act):
    a = a_ref[...]
    if act == "silu":
        a = _silu(a.astype(f32))
    o_ref[...] = (_dot(a.astype(bf16), w_ref[...]) + b_ref[...]).astype(o_ref.dtype)


def _mm(a, w, bias, *, tm, tn, act=None, out_dtype=f32, name):
    m, k = a.shape
    n = w.shape[1]
    return pl.pallas_call(
        functools.partial(_mm_body, act=act),
        grid=(m // tm, n // tn),
        in_specs=[pl.BlockSpec((tm, k), lambda i, j: (i, 0)),
                  pl.BlockSpec((k, tn), lambda i, j: (0, j)),
                  pl.BlockSpec((1, tn), lambda i, j: (0, j))],
        out_specs=pl.BlockSpec((tm, tn), lambda i, j: (i, j)),
        out_shape=jax.ShapeDtypeStruct((m, n), out_dtype),
        compiler_params=_params("parallel", "arbitrary"),
        name=name,
    )(a, w, bias)


def _ln_mod_body(x_ref, g_ref, b_ref, sc_ref, sh_ref, xn_ref, u_ref):
    xn = _layer_norm(x_ref[...], g_ref[...], b_ref[...])
    xn_ref[...] = xn
    u_ref[...] = (xn * (1.0 + sc_ref[0]) + sh_ref[0]).astype(bf16)


def _mod_spec(mod3, tm, tiles_per_mod):
    return pl.BlockSpec((1, mod3.shape[1], mod3.shape[2]), lambda i: (i // tiles_per_mod, 0, 0))


def _ln_mod(x, g, b, sc, sh, *, tm, tiles_per_mod, name):
    n, d = x.shape
    row = pl.BlockSpec((tm, d), lambda i: (i, 0))
    vec = pl.BlockSpec((1, d), lambda i: (0, 0))
    return pl.pallas_call(
        _ln_mod_body,
        grid=(n // tm,),
        in_specs=[row, vec, vec, _mod_spec(sc, tm, tiles_per_mod), _mod_spec(sh, tm, tiles_per_mod)],
        out_specs=(row, row),
        out_shape=(jax.ShapeDtypeStruct((n, d), f32), jax.ShapeDtypeStruct((n, d), bf16)),
        compiler_params=_params("parallel"),
        name=name,
    )(x, g, b, sc, sh)


ATTN_SPAN = BAND * max(dil for _, dil in ATTN_GROUPS)
ATTN_LANES = 2 * HEAD_DIM


def _attn_prompt_fused_body(*refs):
    q_refs = refs[0:N_GROUPS]
    k_refs = refs[N_GROUPS:2 * N_GROUPS]
    v_refs = refs[2 * N_GROUPS:3 * N_GROUPS]
    a_ref = refs[3 * N_GROUPS]
    o_scr, l_scr = refs[3 * N_GROUPS + 1:]
    span0 = pl.program_id(2) * ATTN_SPAN
    qi = lax.broadcasted_iota(jnp.int32, (BAND, 2 * BAND), 0)
    kj = lax.broadcasted_iota(jnp.int32, (BAND, 2 * BAND), 1)
    dist = qi + BAND - kj
    band = (dist >= 0) & (dist <= BAND)
    heads = ATTN_LANES // HEAD_DIM
    head_of_lane = lax.broadcasted_iota(jnp.int32, (BAND, ATTN_LANES), 1) // HEAD_DIM
    for g, (_, dil) in enumerate(ATTN_GROUPS):
        n_blocks = ATTN_SPAN // (BAND * dil)

        def block(idx, carry, g=g, dil=dil, n_blocks=n_blocks):
            r = idx % dil
            start = r + (idx // dil) * (BAND * dil)
            cur = span0 + start
            prev = cur - BAND * dil
            has_prev = prev >= 0
            prev = jnp.maximum(prev, 0)
            rows = pl.ds(start, BAND, stride=dil)
            q = (q_refs[g][0, rows, :] * ATTN_SCALE).astype(bf16)
            k = jnp.concatenate([k_refs[g][0, pl.ds(prev, BAND, stride=dil), :],
                                 k_refs[g][0, pl.ds(cur, BAND, stride=dil), :]], axis=0).astype(bf16)
            v = jnp.concatenate([v_refs[g][0, pl.ds(prev, BAND, stride=dil), :],
                                 v_refs[g][0, pl.ds(cur, BAND, stride=dil), :]], axis=0).astype(bf16)
            mask = band & ((kj >= BAND) | has_prev)
            v1 = jnp.concatenate([v, jnp.ones_like(v)], axis=-1)
            out = lse = None
            for h in range(heads):
                mine = head_of_lane == h
                s = jnp.where(mask, _dot_nt(jnp.where(mine, q, jnp.zeros_like(q)), k), NEG)
                m = jnp.max(s, axis=-1, keepdims=True)
                pv = _dot(jnp.exp(s - m).astype(bf16), v1)
                den = pv[:, ATTN_LANES:]
                o_h = pv[:, :ATTN_LANES] / den
                l_h = m + jnp.log(den)
                out = o_h if out is None else jnp.where(mine, o_h, out)
                lse = l_h if lse is None else jnp.where(mine, l_h, lse)
            o_scr[g, rows, :] = out
            l_scr[g, rows, :] = lse
            return carry

        lax.fori_loop(0, n_blocks * dil, block, 0, unroll=4)
    ls = [l_scr[g] for g in range(N_GROUPS)]
    m = functools.reduce(jnp.maximum, ls)
    ws = [jnp.exp(l - m) for l in ls]
    num = sum(ws[g] * o_scr[g] for g in range(N_GROUPS))
    a_ref[...] = num / sum(ws)


def _attn_prompt(qkv, bsz, seq):
    assert seq % ATTN_SPAN == 0
    qkv3 = qkv.reshape(bsz, seq, QKV_W)
    n_spans = seq // ATTN_SPAN
    lane_blocks = GROUP_W // ATTN_LANES

    def col(g, which):
        return lambda b, hp, sp: (g * 3 + which) * lane_blocks + hp

    q_specs = [pl.BlockSpec((1, ATTN_SPAN, ATTN_LANES),
                            functools.partial(lambda b, hp, sp, c: (b, sp, c(b, hp, sp)), c=col(g, 0)))
               for g in range(N_GROUPS)]
    kv_specs = [pl.BlockSpec((1, seq, ATTN_LANES),
                             functools.partial(lambda b, hp, sp, c: (b, 0, c(b, hp, sp)), c=col(g, which)))
                for which in (1, 2) for g in range(N_GROUPS)]
    a = pl.pallas_call(
        _attn_prompt_fused_body,
        grid=(bsz, lane_blocks, n_spans),
        in_specs=q_specs + kv_specs,
        out_specs=pl.BlockSpec((ATTN_SPAN, ATTN_LANES), lambda b, hp, sp: (b * n_spans + sp, hp)),
        out_shape=jax.ShapeDtypeStruct((bsz * seq, GROUP_W), f32),
        scratch_shapes=[pltpu.VMEM((N_GROUPS, ATTN_SPAN, ATTN_LANES), f32),
                        pltpu.VMEM((N_GROUPS, ATTN_SPAN, ATTN_LANES), f32)],
        compiler_params=_params("parallel", "parallel", "arbitrary"),
        name="attn_prompt",
    )(*([qkv3] * (3 * N_GROUPS)))
    return a


def _attn_sample_body(qkv_ref, c0_ref, c1_ref, c2_ref, a_ref, *, n_new):
    caches = (c0_ref, c1_ref, c2_ref)
    qkv = qkv_ref[...]
    masks = []
    for window, dil in ATTN_GROUPS:
        s_i = lax.broadcasted_iota(jnp.int32, (n_new, window), 0)
        p_i = lax.broadcasted_iota(jnp.int32, (n_new, window), 1)
        masks.append((p_i >= s_i) & (((s_i - p_i) & (dil - 1)) == 0))
    s_n = lax.broadcasted_iota(jnp.int32, (n_new, n_new), 0)
    k_n = lax.broadcasted_iota(jnp.int32, (n_new, n_new), 1)
    new_masks = [(k_n <= s_n) & (((s_n - k_n) & (dil - 1)) == 0) for _, dil in ATTN_GROUPS]
    for h in range(N_HEADS):
        hs = slice(h * HEAD_DIM, (h + 1) * HEAD_DIM)
        scores, new_scores, new_vals = [], [], []
        for g in range(N_GROUPS):
            base = g * 3 * GROUP_W
            q = (qkv[:, base + h * HEAD_DIM:base + (h + 1) * HEAD_DIM] * ATTN_SCALE).astype(bf16)
            kn = qkv[:, base + GROUP_W + h * HEAD_DIM:base + GROUP_W + (h + 1) * HEAD_DIM].astype(bf16)
            new_vals.append(qkv[:, base + 2 * GROUP_W + h * HEAD_DIM:base + 2 * GROUP_W + (h + 1) * HEAD_DIM]
                            .astype(bf16))
            scores.append(jnp.where(masks[g], _dot(q, caches[g][0, 0, h].astype(bf16)), NEG))
            new_scores.append(jnp.where(new_masks[g], _dot_nt(q, kn), NEG))
        m = scores[0].max(axis=-1, keepdims=True)
        for sc in scores[1:] + new_scores:
            m = jnp.maximum(m, sc.max(axis=-1, keepdims=True))
        den = jnp.zeros((n_new, 1), f32)
        o = jnp.zeros((n_new, HEAD_DIM), f32)
        for g in range(N_GROUPS):
            p = jnp.exp(scores[g] - m)
            pn = jnp.exp(new_scores[g] - m)
            den = den + jnp.sum(p, axis=-1, keepdims=True) + jnp.sum(pn, axis=-1, keepdims=True)
            o = o + _dot_nt(p.astype(bf16), caches[g][0, 1, h].astype(bf16)) + _dot(pn.astype(bf16), new_vals[g])
        a_ref[:, hs] = o / den


def _attn_sample(qkv, caches, bsz, n_new):
    views, specs = [], []
    for g, (window, dil) in enumerate(ATTN_GROUPS):
        assert caches[g].shape[1] == window == BAND * dil and dil & (dil - 1) == 0
        views.append(jnp.transpose(caches[g], (0, 2, 3, 4, 1)))
        specs.append(pl.BlockSpec((1, 2, N_HEADS, HEAD_DIM, window), lambda b: (b, 0, 0, 0, 0)))
    return pl.pallas_call(
        functools.partial(_attn_sample_body, n_new=n_new),
        grid=(bsz,),
        in_specs=[pl.BlockSpec((n_new, QKV_W), lambda b: (b, 0))] + specs,
        out_specs=pl.BlockSpec((n_new, GROUP_W), lambda b: (b, 0)),
        out_shape=jax.ShapeDtypeStruct((bsz * n_new, GROUP_W), f32),
        compiler_params=_params("parallel"),
        name="attn_sample",
    )(qkv, *views)


def _ssd_conv(ext_ref, n, cw_ref, cb_ref):
    y = cb_ref[...]
    for j in range(CONV_W):
        off = SUBLANES - (CONV_W - 1) + j
        y = y + ext_ref[off:off + n, :] * cw_ref[j:j + 1, :]
    return _silu(y)


def _gate_norm(y, z, g, d_inner):
    y = y * _silu(z.astype(f32))
    gw = d_inner // SSM_GROUPS
    parts = []
    for gi in range(SSM_GROUPS):
        yg = y[:, gi * gw:(gi + 1) * gw]
        parts.append(yg * lax.rsqrt(jnp.mean(yg * yg, axis=-1, keepdims=True) + RMS_EPS))
    return jnp.concatenate(parts, axis=-1) * g


def _ssd_prompt_body(xbc_ref, z_ref, dt_ref, cw_ref, cb_ref, dtb_ref, alog_ref, dtbt_ref, alogt_ref,
                     dskip_ref, ng_ref, ex_ref, s_ref, hl_ref, ext_ref, ht_ref, y_ref, *, d_inner, n_heads):
    c = pl.program_id(1)
    q = xbc_ref.shape[0]
    gw = d_inner // SSM_GROUPS
    hpg = n_heads // SSM_GROUPS

    @pl.when(c == 0)
    def _():
        ext_ref[0:SUBLANES, :] = jnp.zeros((SUBLANES, ext_ref.shape[1]), f32)
        ht_ref[...] = jnp.zeros_like(ht_ref)

    @pl.when(c > 0)
    def _():
        ext_ref[0:SUBLANES, :] = ext_ref[q:q + SUBLANES, :]

    ext_ref[SUBLANES:SUBLANES + q, :] = xbc_ref[...]
    xc = _ssd_conv(ext_ref, q, cw_ref, cb_ref)
    xs = xc[:, :d_inner]
    gn = SSM_GROUPS * D_STATE
    bm = xc[:, d_inner:d_inner + gn].astype(bf16)
    cm = xc[:, d_inner + gn:].astype(bf16)

    dtr = dt_ref[...]
    dt = _softplus(dtr + dtb_ref[...])
    da = dt * (-jnp.exp(alog_ref[...]))
    dt_t = _softplus(dtr.T[0:n_heads, :] + dtbt_ref[...])
    da_t = dt_t * (-jnp.exp(alogt_ref[...]))
    ii = lax.broadcasted_iota(jnp.int32, (q, q), 0)
    jj = lax.broadcasted_iota(jnp.int32, (q, q), 1)
    causal = ii >= jj
    lower = causal.astype(bf16)
    upper = (ii <= jj).astype(bf16)
    cum = _dot3_rhs(lower, da)
    cum_t = _dot3_lhs(da_t, upper)
    ex = ex_ref[...]
    cumx = _dot3_lhs(cum, ex)
    dtx = _dot3_lhs(dt, ex)
    clx = cumx[q - 1:q, :]
    ecum = jnp.exp(cumx)
    xd = (jnp.exp(clx - cumx) * dtx * xs).astype(bf16)
    xsb = xs.astype(bf16)

    for g in range(SSM_GROUPS):
        gl = slice(g * gw, (g + 1) * gw)
        cg = cm[:, g * D_STATE:(g + 1) * D_STATE]
        bg = bm[:, g * D_STATE:(g + 1) * D_STATE]
        cb = _dot_nt(cg, bg)
        h_old = ht_ref[g]
        y_ref[:, gl] = ecum[:, gl] * _dot(cg, h_old.astype(bf16))
        ht_ref[g] = jnp.exp(clx[:, gl]) * h_old + _dot_tn(bg, xd[:, gl])
        for e in range(hpg):
            hd = g * hpg + e
            hl = slice(hd * SSM_HEAD_DIM, (hd + 1) * SSM_HEAD_DIM)
            seg = jnp.broadcast_to(cum[:, hd:hd + 1], (q, q)) - jnp.broadcast_to(cum_t[hd:hd + 1, :], (q, q))
            w = cb * jnp.exp(jnp.where(causal, seg, NEG)) * jnp.broadcast_to(dt_t[hd:hd + 1, :], (q, q))
            y_ref[:, hl] += _dot(w.astype(bf16), xsb[:, hl])

    y = y_ref[...] + dskip_ref[...] * xs
    s_ref[...] = _gate_norm(y, z_ref[...], ng_ref[...], d_inner).astype(s_ref.dtype)

    @pl.when(c == pl.num_programs(1) - 1)
    def _():
        for g in range(SSM_GROUPS):
            hl_ref[0, g * gw:(g + 1) * gw, :] = ht_ref[g].T


def _ssd_prompt(xbc, z, dt, prm, bsz, seq, *, chunk):
    n, d_xbc = xbc.shape
    d_inner = z.shape[1]
    n_heads = d_inner // SSM_HEAD_DIM
    nc = seq // chunk
    gw = d_inner // SSM_GROUPS

    def row(w):
        return pl.BlockSpec((chunk, w), lambda b, c: (b * nc + c, 0))

    def full(a):
        return pl.BlockSpec(a.shape, lambda b, c: (0,) * a.ndim)

    consts = [prm["conv_w"], prm["conv_b"], prm["dtb_row"], prm["alog_row"], prm["dtb_t"], prm["alog_t"],
              prm["dskip_x"], prm["norm_g"], prm["ex"]]
    return pl.pallas_call(
        functools.partial(_ssd_prompt_body, d_inner=d_inner, n_heads=n_heads),
        grid=(bsz, nc),
        in_specs=[row(d_xbc), row(d_inner), row(LANES)] + [full(a) for a in consts],
        out_specs=(row(d_inner), pl.BlockSpec((1, d_inner, D_STATE), lambda b, c: (b, 0, 0))),
        out_shape=(jax.ShapeDtypeStruct((n, d_inner), bf16),
                   jax.ShapeDtypeStruct((bsz, d_inner, D_STATE), f32)),
        scratch_shapes=[pltpu.VMEM((SUBLANES + chunk + SUBLANES, d_xbc), f32),
                        pltpu.VMEM((SSM_GROUPS, D_STATE, gw), f32),
                        pltpu.VMEM((chunk, d_inner), f32)],
        compiler_params=_params("parallel", "arbitrary"),
        name="ssd_prompt",
    )(xbc, z, dt, *consts)


def _ssd_sample_body(xbc_ref, cs_ref, z_ref, dt_ref, h_ref, cw_ref, cb_ref, dtb_ref, alog_ref,
                     dskip_ref, ng_ref, ex_ref, rsel_ref, s_ref, hn_ref, ext_ref, *, d_inner, n_heads):
    q = xbc_ref.shape[0]
    gw = d_inner // SSM_GROUPS
    ext_ref[0:SUBLANES, :] = cs_ref[0]
    ext_ref[SUBLANES:SUBLANES + q, :] = xbc_ref[...]
    xc = _ssd_conv(ext_ref, q, cw_ref, cb_ref)
    xs = xc[:, :d_inner]
    gn = SSM_GROUPS * D_STATE
    bm = xc[:, d_inner:d_inner + gn]
    cm = xc[:, d_inner + gn:]

    dt = _softplus(dt_ref[...] + dtb_ref[...])
    da = dt * (-jnp.exp(alog_ref[...]))
    row = lax.broadcasted_iota(jnp.int32, (q, LANES), 0)
    cum = jnp.zeros((q, LANES), f32)
    for j in range(q):
        cum = cum + jnp.where(row >= j, da[j:j + 1, :], 0.0)
    ex = ex_ref[...]
    cumx = _dot3_lhs(cum, ex)
    dtx = _dot3_lhs(dt, ex)
    clx = cumx[q - 1:q, :]
    rowx = lax.broadcasted_iota(jnp.int32, (q, d_inner), 0)

    y = dskip_ref[...] * xs
    for j in range(q):
        prod = cm * bm[j:j + 1, :]
        cbx = jnp.concatenate(
            [jnp.broadcast_to(jnp.sum(prod[:, g * D_STATE:(g + 1) * D_STATE], axis=-1, keepdims=True), (q, gw))
             for g in range(SSM_GROUPS)], axis=-1)
        seg = jnp.where(rowx >= j, cumx - cumx[j:j + 1, :], NEG)
        y = y + cbx * jnp.exp(seg) * (dtx[j:j + 1, :] * xs[j:j + 1, :])

    xd = (jnp.exp(clx - cumx) * dtx * xs).astype(bf16)
    ones = jnp.ones((q, LANES), bf16)
    da_hi, da_mid, da_lo = _split3(da)
    cl_b = _dot_tn(da_hi, ones) + _dot_tn(da_mid, ones) + _dot_tn(da_lo, ones)
    decay = jnp.exp(_dot3_rhs(rsel_ref[...], cl_b))
    ecum = jnp.exp(cumx)
    cmb = cm.astype(bf16)
    bmb = bm.astype(bf16)
    ys = []
    for g in range(SSM_GROUPS):
        rows = slice(g * gw, (g + 1) * gw)
        hg = h_ref[0, rows, :]
        ys.append(_dot_nt(cmb[:, g * D_STATE:(g + 1) * D_STATE], hg.astype(bf16)))
        hn_ref[0, rows, :] = decay[rows, :] * hg + _dot_tn(xd[:, rows], bmb[:, g * D_STATE:(g + 1) * D_STATE])
    y = y + ecum * jnp.concatenate(ys, axis=-1)
    s_ref[...] = _gate_norm(y, z_ref[...], ng_ref[...], d_inner)


def _ssd_sample(xbc, conv_state8, z, dt, h0, prm, bsz, n_new):
    n, d_xbc = xbc.shape
    d_inner = z.shape[1]
    n_heads = d_inner // SSM_HEAD_DIM

    def row(w):
        return pl.BlockSpec((n_new, w), lambda b: (b, 0))

    def full(a):
        return pl.BlockSpec(a.shape, lambda b: (0,) * a.ndim)

    consts = [prm["conv_w"], prm["conv_b"], prm["dtb_row"], prm["alog_row"],
              prm["dskip_x"], prm["norm_g"], prm["ex"], prm["rsel"]]
    state = pl.BlockSpec((1, d_inner, D_STATE), lambda b: (b, 0, 0))
    return pl.pallas_call(
        functools.partial(_ssd_sample_body, d_inner=d_inner, n_heads=n_heads),
        grid=(bsz,),
        in_specs=[row(d_xbc), pl.BlockSpec((1, SUBLANES, d_xbc), lambda b: (b, 0, 0)), row(d_inner), row(LANES),
                  state] + [full(a) for a in consts],
        out_specs=(row(d_inner), state),
        out_shape=(jax.ShapeDtypeStruct((n, d_inner), f32),
                   jax.ShapeDtypeStruct((bsz, d_inner, D_STATE), f32)),
        scratch_shapes=[pltpu.VMEM((2 * SUBLANES, d_xbc), f32)],
        compiler_params=_params("parallel"),
        name="ssd_sample",
    )(xbc, conv_state8, z, dt, h0, *consts)


def _mixer_body(a_ref, s_ref, ga_ref, gs_ref, xn_ref, gt_ref, sc_ref, sh_ref, wa_ref, ws_ref, wo_ref,
                lg_ref, lb_ref, wrh_ref, wrl_ref, br_ref, x1_ref, u2_ref, lo_ref):
    m = (jax.nn.sigmoid(ga_ref[...].astype(f32)) * _dot(a_ref[...].astype(bf16), wa_ref[...])
         + jax.nn.sigmoid(gs_ref[...].astype(f32)) * _dot(s_ref[...].astype(bf16), ws_ref[...]))
    o = _dot(m.astype(bf16), wo_ref[...])
    x1 = _layer_norm(DN_ALPHA * xn_ref[...] + gt_ref[0] * o, lg_ref[...], lb_ref[...])
    x1_ref[...] = x1
    u2 = x1 * (1.0 + sc_ref[0]) + sh_ref[0]
    _store_row_tiles(u2_ref, u2)
    hi, mid, lo = _split3(u2)
    wrh = wrh_ref[...]
    wrl = wrl_ref[...]
    lo_ref[...] = (_dot(hi, wrh) + (_dot(hi, wrl) + _dot(mid, wrh)) + (_dot(mid, wrl) + _dot(lo, wrh))
                   + br_ref[...])


def _mixer(a, s, gates, xn, gt, sc, sh, wts, *, tm, tiles_per_mod, name):
    n, d = xn.shape

    def row(w, col=0):
        return pl.BlockSpec((tm, w), lambda i: (i, col))

    def full(arr):
        return pl.BlockSpec(arr.shape, lambda i: (0,) * arr.ndim)

    consts = [wts["w_attn_br"], wts["w_ssd_br"], wts["w_out"], wts["ln_mix_g"], wts["ln_mix_b"],
              wts["w_router_hi"], wts["w_router_lo"], wts["b_router"]]
    return pl.pallas_call(
        _mixer_body,
        grid=(n // tm,),
        in_specs=[row(a.shape[1]), row(s.shape[1]), row(d, 0), row(d, 1), row(d),
                  _mod_spec(gt, tm, tiles_per_mod), _mod_spec(sc, tm, tiles_per_mod),
                  _mod_spec(sh, tm, tiles_per_mod)] + [full(c) for c in consts],
        out_specs=(row(d), pl.BlockSpec((tm * (d // LANES), LANES), lambda i: (i, 0)), row(LANES)),
        out_shape=(jax.ShapeDtypeStruct((n, d), f32), jax.ShapeDtypeStruct((n * (d // LANES), LANES), f32),
                   jax.ShapeDtypeStruct((n, LANES), f32)),
        compiler_params=_params("parallel"),
        name=name,
    )(a, s, gates, gates, xn, gt, sc, sh, *consts)


def _moe_body(be_ref, nv_ref, nr_ref, tok_ref, tok_next_ref, slot_ref, x_hbm, wg_ref, bg_ref, wu_ref, bu_ref, wd_ref,
              bd_ref, y_hbm, xbuf, ybuf, wgb, wub, wdb, sem_in, sem_out, *, bm):
    i = pl.program_id(0)
    nv = nv_ref[0]
    nc = xbuf.shape[1] // bm
    cur = i % 2
    nxt = 1 - cur

    def gather_row(idx_ref, buf, r, priority):
        src = pl.multiple_of(idx_ref[0, 0, r], nc)
        pltpu.make_async_copy(x_hbm.at[pl.ds(src, nc), :], xbuf.at[buf, pl.ds(pl.multiple_of(r * nc, nc), nc), :],
                              sem_in.at[buf]).start(priority=priority)

    def start_gather(idx_ref, buf):
        def body(r2, carry):
            gather_row(idx_ref, buf, 2 * r2, 0)
            gather_row(idx_ref, buf, 2 * r2 + 1, 1)
            return carry
        lax.fori_loop(0, bm // 2, body, 0, unroll=DMA_ISSUE_UNROLL // 2)

    def wait_gather(buf):
        pltpu.make_async_copy(x_hbm.at[pl.ds(0, bm * nc), :], xbuf.at[buf], sem_in.at[buf]).wait()

    def scatter_row(buf, r, slot):
        dst = pl.multiple_of(slot, nc)
        return pltpu.make_async_copy(ybuf.at[buf, pl.ds(pl.multiple_of(r * nc, nc), nc), :],
                                     y_hbm.at[pl.ds(dst, nc), :], sem_out.at[buf])

    def start_scatter(buf, n_real):
        @pl.when(n_real == bm)
        def _():
            def body(r2, carry):
                scatter_row(buf, 2 * r2, slot_ref[0, 0, 2 * r2]).start(priority=0)
                scatter_row(buf, 2 * r2 + 1, slot_ref[0, 0, 2 * r2 + 1]).start(priority=1)
                return carry
            lax.fori_loop(0, bm // 2, body, 0, unroll=DMA_ISSUE_UNROLL // 2)

        @pl.when(n_real < bm)
        def _():
            def body(r, carry):
                scatter_row(buf, r, slot_ref[0, 0, r]).start()
                return carry
            lax.fori_loop(0, n_real, body, 0)

    def wait_scatter(buf, n_real):
        @pl.when(n_real == bm)
        def _():
            pltpu.make_async_copy(ybuf.at[buf], y_hbm.at[pl.ds(0, bm * nc), :], sem_out.at[buf]).wait()

        @pl.when(n_real < bm)
        def _():
            def body(r, carry):
                scatter_row(buf, 0, 0).wait()
                return carry
            lax.fori_loop(0, n_real, body, 0)

    @pl.when((i == 0) & (nv > 0))
    def _():
        start_gather(tok_ref, cur)

    @pl.when(i + 1 < nv)
    def _():
        start_gather(tok_next_ref, nxt)

    @pl.when(i < nv)
    def _():
        @pl.when((i == 0) | (be_ref[i] != be_ref[jnp.maximum(i - 1, 0)]))
        def _():
            wgb[...] = wg_ref[0].astype(bf16)
            wub[...] = wu_ref[0].astype(bf16)
            wdb[...] = wd_ref[0].astype(bf16)

        wait_gather(cur)
        xb = _load_row_tiles(xbuf, bm, nc, (cur,)).astype(bf16)
        gate = jnp.minimum(_dot(xb, wgb[...]) + bg_ref[0], SWIGLU_LIMIT)
        up = jnp.clip(_dot(xb, wub[...]) + bu_ref[0], -SWIGLU_LIMIT, SWIGLU_LIMIT)
        h = gate * jax.nn.sigmoid(SWIGLU_ALPHA * gate) * (up + 1.0)
        y = _dot(h.astype(bf16), wdb[...]) + bd_ref[0]

        @pl.when(i >= 1)
        def _():
            wait_scatter(nxt, nr_ref[jnp.maximum(i - 1, 0)])

        _store_row_tiles(ybuf, y, (cur,))
        start_scatter(cur, nr_ref[i])

        @pl.when(i == nv - 1)
        def _():
            wait_scatter(cur, nr_ref[i])


def _moe(u2, logits, wts, *, bm):
    d_ff, d = wts["w_down"].shape[1:]
    nc = d // LANES
    n_tok = u2.shape[0] // nc
    n_assign = n_tok * TOP_K
    i32 = jnp.int32
    top_logit, top_idx = lax.top_k(logits[:, :N_EXPERTS], TOP_K)
    top_gate = jax.nn.softmax(top_logit, axis=-1)
    e_flat = top_idx.reshape(-1)
    order = jnp.argsort(e_flat).astype(i32)
    experts = jnp.arange(N_EXPERTS, dtype=i32)
    counts = jnp.sum((e_flat[:, None] == experts[None, :]).astype(i32), axis=0)
    starts = jnp.cumsum(counts) - counts
    padded = (counts + bm - 1) // bm * bm
    pad_ends = jnp.cumsum(padded)
    pad_starts = pad_ends - padded
    n_blocks = -(-n_assign // bm) + N_EXPERTS
    blk_row = jnp.arange(n_blocks, dtype=i32) * bm
    block_expert = jnp.minimum(jnp.sum((pad_ends[None, :] <= blk_row[:, None]).astype(i32), axis=1), N_EXPERTS - 1)
    off = blk_row - pad_starts[block_expert]
    n_real = jnp.clip(counts[block_expert] - off, 0, bm)
    j = jnp.arange(bm, dtype=i32)[None, :]
    src = jnp.clip(starts[block_expert][:, None] + off[:, None] + j, 0, n_assign - 1)
    a_id = jnp.take(order, src.reshape(-1), axis=0).reshape(n_blocks, bm)
    real = j < n_real[:, None]
    row_tok = jnp.where(real, a_id // TOP_K, 0) * nc
    row_slot = jnp.where(real, (a_id % TOP_K) * n_tok + a_id // TOP_K, 0) * nc
    n_valid = (pad_ends[-1] // bm).astype(i32).reshape(1)

    def wspec(shape):
        return pl.BlockSpec((1,) + shape, lambda i, be, nv, nr: (be[i], 0, 0))

    idx_spec = pl.BlockSpec((1, 1, bm), lambda i, be, nv, nr: (i, 0, 0), memory_space=pltpu.SMEM)
    idx_next = pl.BlockSpec((1, 1, bm), lambda i, be, nv, nr: (jnp.minimum(i + 1, n_blocks - 1), 0, 0),
                            memory_space=pltpu.SMEM)
    grid_spec = pltpu.PrefetchScalarGridSpec(
        num_scalar_prefetch=3,
        grid=(n_blocks,),
        in_specs=[idx_spec, idx_next, idx_spec,
                  pl.BlockSpec(memory_space=pl.ANY),
                  wspec((d, d_ff)), wspec((1, d_ff)), wspec((d, d_ff)), wspec((1, d_ff)),
                  wspec((d_ff, d)), wspec((1, d))],
        out_specs=pl.BlockSpec(memory_space=pl.ANY),
        scratch_shapes=[pltpu.VMEM((2, bm * nc, LANES), f32), pltpu.VMEM((2, bm * nc, LANES), f32),
                        pltpu.VMEM((d, d_ff), bf16), pltpu.VMEM((d, d_ff), bf16), pltpu.VMEM((d_ff, d), bf16),
                        pltpu.SemaphoreType.DMA((2,)), pltpu.SemaphoreType.DMA((2,))],
    )
    row_tok3 = row_tok.reshape(n_blocks, 1, bm)
    y = pl.pallas_call(
        functools.partial(_moe_body, bm=bm),
        grid_spec=grid_spec,
        out_shape=jax.ShapeDtypeStruct((n_assign * nc, LANES), f32),
        compiler_params=_params("arbitrary"),
        name=f"moe_bm{bm}",
    )(block_expert, n_valid, n_real, row_tok3, row_tok3, row_slot.reshape(n_blocks, 1, bm), u2,
      wts["w_gate"], wts["b_gate"], wts["w_up"], wts["b_up"], wts["w_down"], wts["b_down"])
    return y, top_gate


def _final_body(x1_ref, *rest):
    yk_refs, (pg_ref, gt_ref, g_ref, b_ref, o_ref) = rest[:TOP_K], rest[TOP_K:]
    pg = pg_ref[...]
    tm, d = x1_ref.shape
    y = pg[:, 0:1] * _load_row_tiles(yk_refs[0], tm, d // LANES)
    for k in range(1, TOP_K):
        y = y + pg[:, k:k + 1] * _load_row_tiles(yk_refs[k], tm, d // LANES)
    o_ref[...] = _layer_norm(DN_ALPHA * x1_ref[...] + gt_ref[0] * y, g_ref[...], b_ref[...])


def _final(x1, yk, top_gate, gt, g, b, *, tm, tiles_per_mod, row0, n_all, name):
    n, d = x1.shape
    nt = n // tm
    assert row0 % tm == 0 and n_all % tm == 0
    nt_all, blk0 = n_all // tm, row0 // tm
    row = pl.BlockSpec((tm, d), lambda i: (i, 0))
    vec = pl.BlockSpec((1, d), lambda i: (0, 0))
    planes = [pl.BlockSpec((tm * (d // LANES), LANES),
                           functools.partial(lambda i, k: (k * nt_all + blk0 + i, 0), k=k)) for k in range(TOP_K)]
    return pl.pallas_call(
        _final_body,
        grid=(nt,),
        in_specs=[row] + planes + [pl.BlockSpec((tm, TOP_K), lambda i: (i, 0)),
                                   _mod_spec(gt, tm, tiles_per_mod), vec, vec],
        out_specs=row,
        out_shape=jax.ShapeDtypeStruct((n, d), f32),
        compiler_params=_params("parallel"),
        name=name,
    )(x1, *([yk] * TOP_K), top_gate, gt, g, b)


def _stream(x, c, wts, ssd_prm, *, caches=None, conv_state=None, ssm_state=None):
    bsz, seq, d = x.shape
    n = bsz * seq
    is_prompt = caches is None
    d_inner = wts["w_z"].shape[1]
    d_xbc = wts["w_xbc"].shape[1]

    c_pad = jnp.pad(c, ((0, (-bsz) % SUBLANES), (0, 0)))
    mod = _mm(c_pad, wts["w_ada"], wts["b_ada"], tm=c_pad.shape[0], tn=2 * d, act="silu", name="ada_mod")[:bsz]
    sh1, sc1, gt1, sh2, sc2, gt2 = jnp.split(mod, 6, axis=-1)

    def per_tile(p, tile):
        if seq % tile == 0:
            return p.reshape(bsz, 1, d), seq // tile
        return jnp.broadcast_to(p[:, None, :], (bsz, seq, d)).reshape(n // tile, tile, d), 1

    tm = 512
    tpm = per_tile(sc1, tm)[1]
    xn, u = _ln_mod(x.reshape(n, d), wts["ln_emb_g"], wts["ln_emb_b"], per_tile(sc1, tm)[0], per_tile(sh1, tm)[0],
                    tm=tm, tiles_per_mod=tpm, name="ln_mod")
    tmm = min(n, 2048)
    act_dtype = bf16 if seq % 16 == 0 else f32
    qkv = _mm(u, wts["w_qkv"], wts["zero_b"][:, :QKV_W], tm=tmm, tn=512, name="proj_qkv")
    z = _mm(u, wts["w_z"], wts["zero_b"][:, :d_inner], tm=tmm, tn=512, out_dtype=act_dtype, name="proj_z")
    xbc = _mm(u, wts["w_xbc"], wts["zero_b"][:, :d_xbc], tm=tmm, tn=512, name="proj_xbc")
    dt = _mm(u, wts["w_dt"], wts["zero_b"][:, :LANES], tm=tmm, tn=LANES, name="proj_dt")
    gates = _mm(u, wts["w_gates"], wts["zero_b"][:, :2 * d], tm=tmm, tn=512, out_dtype=bf16, name="proj_gates")

    qkv3 = qkv.reshape(bsz, seq, QKV_W)
    xbc3 = xbc.reshape(bsz, seq, d_xbc)

    def kv_rows_of(g, keep):
        cols = qkv3[:, seq - keep:, (g * 3 + 1) * GROUP_W:(g * 3 + 3) * GROUP_W]
        return cols.reshape(bsz, keep, 2, N_HEADS, HEAD_DIM)

    if is_prompt:
        a = _attn_prompt(qkv, bsz, seq)
        s, h_last = _ssd_prompt(xbc, z, dt, ssd_prm, bsz, seq, chunk=128)
        kv_rows = [kv_rows_of(g, min(w, seq)) for g, (w, _) in enumerate(ATTN_GROUPS)]
        new_conv = xbc3[:, -(CONV_W - 1):]
    else:
        a = _attn_sample(qkv, caches, bsz, seq)
        cs8 = jnp.pad(conv_state, ((0, 0), (SUBLANES - (CONV_W - 1), 0), (0, 0)))
        s, h_last = _ssd_sample(xbc, cs8, z, dt, ssm_state.reshape(bsz, d_inner, D_STATE), ssd_prm, bsz, seq)
        kv_rows = [kv_rows_of(g, seq) for g in range(N_GROUPS)]
        new_conv = jnp.concatenate([conv_state, xbc3], axis=1)[:, -(CONV_W - 1):]
    new_ssm = h_last.reshape(bsz, d_inner // SSM_HEAD_DIM, SSM_HEAD_DIM, D_STATE)

    tmx = 512
    x1, u2, logits = _mixer(a, s, gates, xn, per_tile(gt1, tmx)[0], per_tile(sc2, tmx)[0], per_tile(sh2, tmx)[0],
                            wts, tm=tmx, tiles_per_mod=per_tile(gt1, tmx)[1], name="mixer")

    def finish(yk, top_gate, row0, n_all):
        y = _final(x1, yk, top_gate[row0:row0 + n], per_tile(gt2, tm)[0], wts["ln_ffn_g"], wts["ln_ffn_b"], tm=tm,
                   tiles_per_mod=tpm, row0=row0, n_all=n_all, name="final_ln")
        return y.reshape(bsz, seq, d)

    return (u2, logits, finish), kv_rows, new_conv, new_ssm


def kernel(x_prompt, x_sample, c_prompt, c_sample, cache_kv_w128, cache_kv_w512, cache_kv_w2048, state_conv, state_ssm, ln_emb_g, ln_emb_b, w_ada, b_ada, w_in, conv_w, conv_b, dt_bias, a_log, d_skip, ssm_norm_g, w_attn_br, w_ssd_br, w_out, ln_mix_g, ln_mix_b, w_router, b_router, w_gate, b_gate, w_up, b_up, w_down, b_down, ln_ffn_g, ln_ffn_b):
    d = x_prompt.shape[-1]
    d_inner = ssm_norm_g.shape[-1]
    d_xbc = conv_w.shape[-1]
    n_heads = dt_bias.shape[-1]
    lyr = 0

    def rowv(v):
        return v.reshape(1, -1).astype(f32)

    def lane_pad(v):
        return jnp.pad(v.astype(f32), (0, LANES - n_heads)).reshape(1, LANES)

    wi = w_in[lyr]
    o0 = QKV_W
    o1 = o0 + d_inner
    o2 = o1 + d_xbc
    o3 = o2 + n_heads
    wr = jnp.pad(w_router[lyr], ((0, 0), (0, LANES - N_EXPERTS)))
    wr_hi = wr.astype(bf16)
    wts = {
        "ln_emb_g": rowv(ln_emb_g), "ln_emb_b": rowv(ln_emb_b),
        "w_ada": w_ada[lyr].astype(bf16), "b_ada": rowv(b_ada[lyr]),
        "w_qkv": wi[:, :o0].astype(bf16), "w_z": wi[:, o0:o1].astype(bf16), "w_xbc": wi[:, o1:o2].astype(bf16),
        "w_dt": jnp.pad(wi[:, o2:o3], ((0, 0), (0, LANES - n_heads))).astype(bf16),
        "w_gates": wi[:, o3:].astype(bf16),
        "zero_b": jnp.zeros((1, max(QKV_W, d_xbc, 2 * d)), f32),
        "w_attn_br": w_attn_br[lyr].astype(bf16), "w_ssd_br": w_ssd_br[lyr].astype(bf16),
        "w_out": w_out[lyr].astype(bf16),
        "ln_mix_g": rowv(ln_mix_g[lyr]), "ln_mix_b": rowv(ln_mix_b[lyr]),
        "w_router_hi": wr_hi, "w_router_lo": (wr - wr_hi.astype(f32)).astype(bf16),
        "b_router": jnp.pad(b_router[lyr], (0, LANES - N_EXPERTS)).reshape(1, LANES),
        "w_gate": w_gate[lyr], "b_gate": b_gate[lyr][:, None, :],
        "w_up": w_up[lyr], "b_up": b_up[lyr][:, None, :],
        "w_down": w_down[lyr], "b_down": b_down[lyr][:, None, :],
        "ln_ffn_g": rowv(ln_ffn_g[lyr]), "ln_ffn_b": rowv(ln_ffn_b[lyr]),
    }
    head_of_lane = jnp.arange(d_inner, dtype=jnp.int32) // SSM_HEAD_DIM
    ssd_prm = {
        "conv_w": conv_w[lyr], "conv_b": rowv(conv_b[lyr]),
        "dtb_row": lane_pad(dt_bias[lyr]), "alog_row": lane_pad(a_log[lyr]),
        "dtb_t": jnp.broadcast_to(dt_bias[lyr][:, None], (n_heads, LANES)),
        "alog_t": jnp.broadcast_to(a_log[lyr][:, None], (n_heads, LANES)),
        "dskip_x": jnp.repeat(d_skip[lyr], SSM_HEAD_DIM).reshape(1, d_inner),
        "norm_g": rowv(ssm_norm_g[lyr]),
        "ex": (jnp.arange(LANES, dtype=jnp.int32)[:, None] == head_of_lane[None, :]).astype(bf16),
        "rsel": (head_of_lane[:, None] == jnp.arange(LANES, dtype=jnp.int32)[None, :]).astype(bf16),
    }

    (u2_p, logits_p, finish_p), kv_p, conv_p, ssm_p = _stream(x_prompt, c_prompt, wts, ssd_prm)
    (u2_s, logits_s, finish_s), kv_s, conv_s, ssm_s = _stream(
        x_sample, c_sample, wts, ssd_prm,
        caches=(cache_kv_w128[lyr], cache_kv_w512[lyr], cache_kv_w2048[lyr]),
        conv_state=state_conv[lyr], ssm_state=state_ssm[lyr])
    n_p, n_s = logits_p.shape[0], logits_s.shape[0]
    yk, top_gate = _moe(jnp.concatenate([u2_p, u2_s], axis=0), jnp.concatenate([logits_p, logits_s], axis=0), wts,
                        bm=MOE_BLOCK_ROWS)
    yp = finish_p(yk, top_gate, 0, n_p + n_s)
    ys = finish_s(yk, top_gate, n_p, n_p + n_s)
    return (yp, ys, kv_p[0][None], kv_s[0][None], kv_p[1][None], kv_s[1][None], kv_p[2][None], kv_s[2][None],
            conv_p[None], conv_s[None], ssm_p[None], ssm_s[None])
```

```python
import functools
import math

import jax
import jax.numpy as jnp
from jax import lax
from jax.experimental import pallas as pl
from jax.experimental.pallas import tpu as pltpu

f32 = jnp.float32
bf16 = jnp.bfloat16

ATTN_GROUPS = ((128, 1), (512, 4), (2048, 16))
N_GROUPS = len(ATTN_GROUPS)
N_HEADS = 8
HEAD_DIM = 64
GROUP_W = N_HEADS * HEAD_DIM
QKV_W = N_GROUPS * 3 * GROUP_W
BAND = 128
ATTN_SCALE = HEAD_DIM ** -0.5
SSM_HEAD_DIM = 64
SSM_GROUPS = 4
D_STATE = 128
CONV_W = 4
N_EXPERTS = 32
TOP_K = 4
SWIGLU_ALPHA = 1.702
SWIGLU_LIMIT = 7.0
LN_EPS = 1e-5
RMS_EPS = 1e-5
DEPTH = 1
DN_ALPHA = (2 * DEPTH) ** 0.25
NEG = -1e30

LANES = 128
SUBLANES = 8
VMEM_LIMIT = 56 * 1024 * 1024
MOE_BLOCK_ROWS = 256
DMA_ISSUE_UNROLL = 32


def _params(*sem):
    return pltpu.CompilerParams(dimension_semantics=sem, vmem_limit_bytes=VMEM_LIMIT)


def _silu(x):
    return x * jax.nn.sigmoid(x)


def _softplus(x):
    return jnp.maximum(x, 0.0) + jnp.log(1.0 + jnp.exp(-jnp.abs(x)))


def _split3(x):
    hi = x.astype(bf16)
    r1 = x - hi.astype(f32)
    mid = r1.astype(bf16)
    lo = (r1 - mid.astype(f32)).astype(bf16)
    return hi, mid, lo


def _dot(a, b):
    return jnp.dot(a, b, preferred_element_type=f32)


def _dot_nt(a, b):
    return lax.dot_general(a, b, (((1,), (1,)), ((), ())), preferred_element_type=f32)


def _dot_tn(a, b):
    return lax.dot_general(a, b, (((0,), (0,)), ((), ())), preferred_element_type=f32)


def _dot3_lhs(x, w):
    hi, mid, lo = _split3(x)
    return _dot(hi, w) + _dot(mid, w) + _dot(lo, w)


def _dot3_rhs(w, x):
    hi, mid, lo = _split3(x)
    return _dot(w, hi) + _dot(w, mid) + _dot(w, lo)


def _layer_norm(x, g, b):
    mu = jnp.mean(x, axis=-1, keepdims=True)
    xc = x - mu
    var = jnp.mean(xc * xc, axis=-1, keepdims=True)
    return xc * lax.rsqrt(var + LN_EPS) * g + b


def _store_row_tiles(ref, x, lead=()):
    rows, d = x.shape
    nc = d // LANES
    for c in range(nc):
        ref[lead + (pl.ds(c, rows, stride=nc), slice(None))] = x[:, c * LANES:(c + 1) * LANES]


def _load_row_tiles(ref, rows, nc, lead=()):
    return jnp.concatenate([ref[lead + (pl.ds(c, rows, stride=nc), slice(None))] for c in range(nc)], axis=-1)


def _mm_body(a_ref, w_ref, b_ref, o_ref, *, act):
    a = a_ref[...]
    if act == "silu":
        a = _silu(a.astype(f32))
    o_ref[...] = (_dot(a.astype(bf16), w_ref[...]) + b_ref[...]).astype(o_ref.dtype)


def _mm(a, w, bias, *, tm, tn, act=None, out_dtype=f32, name):
    m, k = a.shape
    n = w.shape[1]
    return pl.pallas_call(
        functools.partial(_mm_body, act=act),
        grid=(m // tm, n // tn),
        in_specs=[pl.BlockSpec((tm, k), lambda i, j: (i, 0)),
                  pl.BlockSpec((k, tn), lambda i, j: (0, j)),
                  pl.BlockSpec((1, tn), lambda i, j: (0, j))],
        out_specs=pl.BlockSpec((tm, tn), lambda i, j: (i, j)),
        out_shape=jax.ShapeDtypeStruct((m, n), out_dtype),
        compiler_params=_params("parallel", "arbitrary"),
        name=name,
    )(a, w, bias)


def _ln_mod_body(x_ref, g_ref, b_ref, sc_ref, sh_ref, xn_ref, u_ref):
    xn = _layer_norm(x_ref[...], g_ref[...], b_ref[...])
    xn_ref[...] = xn
    u_ref[...] = (xn * (1.0 + sc_ref[0]) + sh_ref[0]).astype(bf16)


def _mod_spec(mod3, tm, tiles_per_mod):
    return pl.BlockSpec((1, mod3.shape[1], mod3.shape[2]), lambda i: (i // tiles_per_mod, 0, 0))


def _ln_mod(x, g, b, sc, sh, *, tm, tiles_per_mod, name):
    n, d = x.shape
    row = pl.BlockSpec((tm, d), lambda i: (i, 0))
    vec = pl.BlockSpec((1, d), lambda i: (0, 0))
    return pl.pallas_call(
        _ln_mod_body,
        grid=(n // tm,),
        in_specs=[row, vec, vec, _mod_spec(sc, tm, tiles_per_mod), _mod_spec(sh, tm, tiles_per_mod)],
        out_specs=(row, row),
        out_shape=(jax.ShapeDtypeStruct((n, d), f32), jax.ShapeDtypeStruct((n, d), bf16)),
        compiler_params=_params("parallel"),
        name=name,
    )(x, g, b, sc, sh)


ATTN_SPAN = BAND * max(dil for _, dil in ATTN_GROUPS)
ATTN_LANES = 2 * HEAD_DIM


def _attn_prompt_fused_body(*refs):
    q_refs = refs[0:N_GROUPS]
    k_refs = refs[N_GROUPS:2 * N_GROUPS]
    v_refs = refs[2 * N_GROUPS:3 * N_GROUPS]
    a_ref = refs[3 * N_GROUPS]
    o_scr, l_scr = refs[3 * N_GROUPS + 1:]
    span0 = pl.program_id(2) * ATTN_SPAN
    qi = lax.broadcasted_iota(jnp.int32, (BAND, 2 * BAND), 0)
    kj = lax.broadcasted_iota(jnp.int32, (BAND, 2 * BAND), 1)
    dist = qi + BAND - kj
    band = (dist >= 0) & (dist <= BAND)
    heads = ATTN_LANES // HEAD_DIM
    head_of_lane = lax.broadcasted_iota(jnp.int32, (BAND, ATTN_LANES), 1) // HEAD_DIM
    for g, (_, dil) in enumerate(ATTN_GROUPS):
        n_blocks = ATTN_SPAN // (BAND * dil)

        def block(idx, carry, g=g, dil=dil, n_blocks=n_blocks):
            r = idx % dil
            start = r + (idx // dil) * (BAND * dil)
            cur = span0 + start
            prev = cur - BAND * dil
            has_prev = prev >= 0
            prev = jnp.maximum(prev, 0)
            rows = pl.ds(start, BAND, stride=dil)
            q = (q_refs[g][0, rows, :] * ATTN_SCALE).astype(bf16)
            k = jnp.concatenate([k_refs[g][0, pl.ds(prev, BAND, stride=dil), :],
                                 k_refs[g][0, pl.ds(cur, BAND, stride=dil), :]], axis=0).astype(bf16)
            v = jnp.concatenate([v_refs[g][0, pl.ds(prev, BAND, stride=dil), :],
                                 v_refs[g][0, pl.ds(cur, BAND, stride=dil), :]], axis=0).astype(bf16)
            mask = band & ((kj >= BAND) | has_prev)
            v1 = jnp.concatenate([v, jnp.ones_like(v)], axis=-1)
            out = lse = None
            for h in range(heads):
                mine = head_of_lane == h
                s = jnp.where(mask, _dot_nt(jnp.where(mine, q, jnp.zeros_like(q)), k), NEG)
                m = jnp.max(s, axis=-1, keepdims=True)
                pv = _dot(jnp.exp(s - m).astype(bf16), v1)
                den = pv[:, ATTN_LANES:]
                o_h = pv[:, :ATTN_LANES] / den
                l_h = m + jnp.log(den)
                out = o_h if out is None else jnp.where(mine, o_h, out)
                lse = l_h if lse is None else jnp.where(mine, l_h, lse)
            o_scr[g, rows, :] = out
            l_scr[g, rows, :] = lse
            return carry

        lax.fori_loop(0, n_blocks * dil, block, 0, unroll=4)
    ls = [l_scr[g] for g in range(N_GROUPS)]
    m = functools.reduce(jnp.maximum, ls)
    ws = [jnp.exp(l - m) for l in ls]
    num = sum(ws[g] * o_scr[g] for g in range(N_GROUPS))
    a_ref[...] = num / sum(ws)


def _attn_prompt(qkv, bsz, seq):
    assert seq % ATTN_SPAN == 0
    qkv3 = qkv.reshape(bsz, seq, QKV_W)
    n_spans = seq // ATTN_SPAN
    lane_blocks = GROUP_W // ATTN_LANES

    def col(g, which):
        return lambda b, hp, sp: (g * 3 + which) * lane_blocks + hp

    q_specs = [pl.BlockSpec((1, ATTN_SPAN, ATTN_LANES),
                            functools.partial(lambda b, hp, sp, c: (b, sp, c(b, hp, sp)), c=col(g, 0)))
               for g in range(N_GROUPS)]
    kv_specs = [pl.BlockSpec((1, seq, ATTN_LANES),
                             functools.partial(lambda b, hp, sp, c: (b, 0, c(b, hp, sp)), c=col(g, which)))
                for which in (1, 2) for g in range(N_GROUPS)]
    a = pl.pallas_call(
        _attn_prompt_fused_body,
        grid=(bsz, lane_blocks, n_spans),
        in_specs=q_specs + kv_specs,
        out_specs=pl.BlockSpec((ATTN_SPAN, ATTN_LANES), lambda b, hp, sp: (b * n_spans + sp, hp)),
        out_shape=jax.ShapeDtypeStruct((bsz * seq, GROUP_W), f32),
        scratch_shapes=[pltpu.VMEM((N_GROUPS, ATTN_SPAN, ATTN_LANES), f32),
                        pltpu.VMEM((N_GROUPS, ATTN_SPAN, ATTN_LANES), f32)],
        compiler_params=_params("parallel", "parallel", "arbitrary"),
        name="attn_prompt",
    )(*([qkv3] * (3 * N_GROUPS)))
    return a


def _attn_sample_body(qkv_ref, c0_ref, c1_ref, c2_ref, a_ref, *, n_new):
    caches = (c0_ref, c1_ref, c2_ref)
    qkv = qkv_ref[...]
    masks = []
    for window, dil in ATTN_GROUPS:
        s_i = lax.broadcasted_iota(jnp.int32, (n_new, window), 0)
        p_i = lax.broadcasted_iota(jnp.int32, (n_new, window), 1)
        masks.append((p_i >= s_i) & (((s_i - p_i) & (dil - 1)) == 0))
    s_n = lax.broadcasted_iota(jnp.int32, (n_new, n_new), 0)
    k_n = lax.broadcasted_iota(jnp.int32, (n_new, n_new), 1)
    new_masks = [(k_n <= s_n) & (((s_n - k_n) & (dil - 1)) == 0) for _, dil in ATTN_GROUPS]
    for h in range(N_HEADS):
        hs = slice(h * HEAD_DIM, (h + 1) * HEAD_DIM)
        scores, new_scores, new_vals = [], [], []
        for g in range(N_GROUPS):
            base = g * 3 * GROUP_W
            q = (qkv[:, base + h * HEAD_DIM:base + (h + 1) * HEAD_DIM] * ATTN_SCALE).astype(bf16)
            kn = qkv[:, base + GROUP_W + h * HEAD_DIM:base + GROUP_W + (h + 1) * HEAD_DIM].astype(bf16)
            new_vals.append(qkv[:, base + 2 * GROUP_W + h * HEAD_DIM:base + 2 * GROUP_W + (h + 1) * HEAD_DIM]
                            .astype(bf16))
            scores.append(jnp.where(masks[g], _dot(q, caches[g][0, 0, h].astype(bf16)), NEG))
            new_scores.append(jnp.where(new_masks[g], _dot_nt(q, kn), NEG))
        m = scores[0].max(axis=-1, keepdims=True)
        for sc in scores[1:] + new_scores:
            m = jnp.maximum(m, sc.max(axis=-1, keepdims=True))
        den = jnp.zeros((n_new, 1), f32)
        o = jnp.zeros((n_new, HEAD_DIM), f32)
        for g in range(N_GROUPS):
            p = jnp.exp(scores[g] - m)
            pn = jnp.exp(new_scores[g] - m)
            den = den + jnp.sum(p, axis=-1, keepdims=True) + jnp.sum(pn, axis=-1, keepdims=True)
            o = o + _dot_nt(p.astype(bf16), caches[g][0, 1, h].astype(bf16)) + _dot(pn.astype(bf16), new_vals[g])
        a_ref[:, hs] = o / den


def _attn_sample(qkv, caches, bsz, n_new):
    views, specs = [], []
    for g, (window, dil) in enumerate(ATTN_GROUPS):
        assert caches[g].shape[1] == window == BAND * dil and dil & (dil - 1) == 0
        views.append(jnp.transpose(caches[g], (0, 2, 3, 4, 1)))
        specs.append(pl.BlockSpec((1, 2, N_HEADS, HEAD_DIM, window), lambda b: (b, 0, 0, 0, 0)))
    return pl.pallas_call(
        functools.partial(_attn_sample_body, n_new=n_new),
        grid=(bsz,),
        in_specs=[pl.BlockSpec((n_new, QKV_W), lambda b: (b, 0))] + specs,
        out_specs=pl.BlockSpec((n_new, GROUP_W), lambda b: (b, 0)),
        out_shape=jax.ShapeDtypeStruct((bsz * n_new, GROUP_W), f32),
        compiler_params=_params("parallel"),
        name="attn_sample",
    )(qkv, *views)


def _ssd_conv(ext_ref, n, cw_ref, cb_ref):
    y = cb_ref[...]
    for j in range(CONV_W):
        off = SUBLANES - (CONV_W - 1) + j
        y = y + ext_ref[off:off + n, :] * cw_ref[j:j + 1, :]
    return _silu(y)


def _gate_norm(y, z, g, d_inner):
    y = y * _silu(z.astype(f32))
    gw = d_inner // SSM_GROUPS
    parts = []
    for gi in range(SSM_GROUPS):
        yg = y[:, gi * gw:(gi + 1) * gw]
        parts.append(yg * lax.rsqrt(jnp.mean(yg * yg, axis=-1, keepdims=True) + RMS_EPS))
    return jnp.concatenate(parts, axis=-1) * g


def _ssd_prompt_body(xbc_ref, z_ref, dt_ref, cw_ref, cb_ref, dtb_ref, alog_ref, dtbt_ref, alogt_ref,
                     dskip_ref, ng_ref, ex_ref, s_ref, hl_ref, ext_ref, ht_ref, y_ref, *, d_inner, n_heads):
    c = pl.program_id(1)
    q = xbc_ref.shape[0]
    gw = d_inner // SSM_GROUPS
    hpg = n_heads // SSM_GROUPS

    @pl.when(c == 0)
    def _():
        ext_ref[0:SUBLANES, :] = jnp.zeros((SUBLANES, ext_ref.shape[1]), f32)
        ht_ref[...] = jnp.zeros_like(ht_ref)

    @pl.when(c > 0)
    def _():
        ext_ref[0:SUBLANES, :] = ext_ref[q:q + SUBLANES, :]

    ext_ref[SUBLANES:SUBLANES + q, :] = xbc_ref[...]
    xc = _ssd_conv(ext_ref, q, cw_ref, cb_ref)
    xs = xc[:, :d_inner]
    gn = SSM_GROUPS * D_STATE
    bm = xc[:, d_inner:d_inner + gn].astype(bf16)
    cm = xc[:, d_inner + gn:].astype(bf16)

    dtr = dt_ref[...]
    dt = _softplus(dtr + dtb_ref[...])
    da = dt * (-jnp.exp(alog_ref[...]))
    dt_t = _softplus(dtr.T[0:n_heads, :] + dtbt_ref[...])
    da_t = dt_t * (-jnp.exp(alogt_ref[...]))
    ii = lax.broadcasted_iota(jnp.int32, (q, q), 0)
    jj = lax.broadcasted_iota(jnp.int32, (q, q), 1)
    causal = ii >= jj
    lower = causal.astype(bf16)
    upper = (ii <= jj).astype(bf16)
    cum = _dot3_rhs(lower, da)
    cum_t = _dot3_lhs(da_t, upper)
    ex = ex_ref[...]
    cumx = _dot3_lhs(cum, ex)
    dtx = _dot3_lhs(dt, ex)
    clx = cumx[q - 1:q, :]
    ecum = jnp.exp(cumx)
    xd = (jnp.exp(clx - cumx) * dtx * xs).astype(bf16)
    xsb = xs.astype(bf16)

    for g in range(SSM_GROUPS):
        gl = slice(g * gw, (g + 1) * gw)
        cg = cm[:, g * D_STATE:(g + 1) * D_STATE]
        bg = bm[:, g * D_STATE:(g + 1) * D_STATE]
        cb = _dot_nt(cg, bg)
        h_old = ht_ref[g]
        y_ref[:, gl] = ecum[:, gl] * _dot(cg, h_old.astype(bf16))
        ht_ref[g] = jnp.exp(clx[:, gl]) * h_old + _dot_tn(bg, xd[:, gl])
        for e in range(hpg):
            hd = g * hpg + e
            hl = slice(hd * SSM_HEAD_DIM, (hd + 1) * SSM_HEAD_DIM)
            seg = jnp.broadcast_to(cum[:, hd:hd + 1], (q, q)) - jnp.broadcast_to(cum_t[hd:hd + 1, :], (q, q))
            w = cb * jnp.exp(jnp.where(causal, seg, NEG)) * jnp.broadcast_to(dt_t[hd:hd + 1, :], (q, q))
            y_ref[:, hl] += _dot(w.astype(bf16), xsb[:, hl])

    y = y_ref[...] + dskip_ref[...] * xs
    s_ref[...] = _gate_norm(y, z_ref[...], ng_ref[...], d_inner).astype(s_ref.dtype)

    @pl.when(c == pl.num_programs(1) - 1)
    def _():
        for g in range(SSM_GROUPS):
            hl_ref[0, g * gw:(g + 1) * gw, :] = ht_ref[g].T


def _ssd_prompt(xbc, z, dt, prm, bsz, seq, *, chunk):
    n, d_xbc = xbc.shape
    d_inner = z.shape[1]
    n_heads = d_inner // SSM_HEAD_DIM
    nc = seq // chunk
    gw = d_inner // SSM_GROUPS

    def row(w):
        return pl.BlockSpec((chunk, w), lambda b, c: (b * nc + c, 0))

    def full(a):
        return pl.BlockSpec(a.shape, lambda b, c: (0,) * a.ndim)

    consts = [prm["conv_w"], prm["conv_b"], prm["dtb_row"], prm["alog_row"], prm["dtb_t"], prm["alog_t"],
              prm["dskip_x"], prm["norm_g"], prm["ex"]]
    return pl.pallas_call(
        functools.partial(_ssd_prompt_body, d_inner=d_inner, n_heads=n_heads),
        grid=(bsz, nc),
        in_specs=[row(d_xbc), row(d_inner), row(LANES)] + [full(a) for a in consts],
        out_specs=(row(d_inner), pl.BlockSpec((1, d_inner, D_STATE), lambda b, c: (b, 0, 0))),
        out_shape=(jax.ShapeDtypeStruct((n, d_inner), bf16),
                   jax.ShapeDtypeStruct((bsz, d_inner, D_STATE), f32)),
        scratch_shapes=[pltpu.VMEM((SUBLANES + chunk + SUBLANES, d_xbc), f32),
                        pltpu.VMEM((SSM_GROUPS, D_STATE, gw), f32),
                        pltpu.VMEM((chunk, d_inner), f32)],
        compiler_params=_params("parallel", "arbitrary"),
        name="ssd_prompt",
    )(xbc, z, dt, *consts)


def _ssd_sample_body(xbc_ref, cs_ref, z_ref, dt_ref, h_ref, cw_ref, cb_ref, dtb_ref, alog_ref,
                     dskip_ref, ng_ref, ex_ref, rsel_ref, s_ref, hn_ref, ext_ref, *, d_inner, n_heads):
    q = xbc_ref.shape[0]
    gw = d_inner // SSM_GROUPS
    ext_ref[0:SUBLANES, :] = cs_ref[0]
    ext_ref[SUBLANES:SUBLANES + q, :] = xbc_ref[...]
    xc = _ssd_conv(ext_ref, q, cw_ref, cb_ref)
    xs = xc[:, :d_inner]
    gn = SSM_GROUPS * D_STATE
    bm = xc[:, d_inner:d_inner + gn]
    cm = xc[:, d_inner + gn:]

    dt = _softplus(dt_ref[...] + dtb_ref[...])
    da = dt * (-jnp.exp(alog_ref[...]))
    row = lax.broadcasted_iota(jnp.int32, (q, LANES), 0)
    cum = jnp.zeros((q, LANES), f32)
    for j in range(q):
        cum = cum + jnp.where(row >= j, da[j:j + 1, :], 0.0)
    ex = ex_ref[...]
    cumx = _dot3_lhs(cum, ex)
    dtx = _dot3_lhs(dt, ex)
    clx = cumx[q - 1:q, :]
    rowx = lax.broadcasted_iota(jnp.int32, (q, d_inner), 0)

    y = dskip_ref[...] * xs
    for j in range(q):
        prod = cm * bm[j:j + 1, :]
        cbx = jnp.concatenate(
            [jnp.broadcast_to(jnp.sum(prod[:, g * D_STATE:(g + 1) * D_STATE], axis=-1, keepdims=True), (q, gw))
             for g in range(SSM_GROUPS)], axis=-1)
        seg = jnp.where(rowx >= j, cumx - cumx[j:j + 1, :], NEG)
        y = y + cbx * jnp.exp(seg) * (dtx[j:j + 1, :] * xs[j:j + 1, :])

    xd = (jnp.exp(clx - cumx) * dtx * xs).astype(bf16)
    ones = jnp.ones((q, LANES), bf16)
    da_hi, da_mid, da_lo = _split3(da)
    cl_b = _dot_tn(da_hi, ones) + _dot_tn(da_mid, ones) + _dot_tn(da_lo, ones)
    decay = jnp.exp(_dot3_rhs(rsel_ref[...], cl_b))
    ecum = jnp.exp(cumx)
    cmb = cm.astype(bf16)
    bmb = bm.astype(bf16)
    ys = []
    for g in range(SSM_GROUPS):
        rows = slice(g * gw, (g + 1) * gw)
        hg = h_ref[0, rows, :]
        ys.append(_dot_nt(cmb[:, g * D_STATE:(g + 1) * D_STATE], hg.astype(bf16)))
        hn_ref[0, rows, :] = decay[rows, :] * hg + _dot_tn(xd[:, rows], bmb[:, g * D_STATE:(g + 1) * D_STATE])
    y = y + ecum * jnp.concatenate(ys, axis=-1)
    s_ref[...] = _gate_norm(y, z_ref[...], ng_ref[...], d_inner)


def _ssd_sample(xbc, conv_state8, z, dt, h0, prm, bsz, n_new):
    n, d_xbc = xbc.shape
    d_inner = z.shape[1]
    n_heads = d_inner // SSM_HEAD_DIM

    def row(w):
        return pl.BlockSpec((n_new, w), lambda b: (b, 0))

    def full(a):
        return pl.BlockSpec(a.shape, lambda b: (0,) * a.ndim)

    consts = [prm["conv_w"], prm["conv_b"], prm["dtb_row"], prm["alog_row"],
              prm["dskip_x"], prm["norm_g"], prm["ex"], prm["rsel"]]
    state = pl.BlockSpec((1, d_inner, D_STATE), lambda b: (b, 0, 0))
    return pl.pallas_call(
        functools.partial(_ssd_sample_body, d_inner=d_inner, n_heads=n_heads),
        grid=(bsz,),
        in_specs=[row(d_xbc), pl.BlockSpec((1, SUBLANES, d_xbc), lambda b: (b, 0, 0)), row(d_inner), row(LANES),
                  state] + [full(a) for a in consts],
        out_specs=(row(d_inner), state),
        out_shape=(jax.ShapeDtypeStruct((n, d_inner), f32),
                   jax.ShapeDtypeStruct((bsz, d_inner, D_STATE), f32)),
        scratch_shapes=[pltpu.VMEM((2 * SUBLANES, d_xbc), f32)],
        compiler_params=_params("parallel"),
        name="ssd_sample",
    )(xbc, conv_state8, z, dt, h0, *consts)


def _mixer_body(a_ref, s_ref, ga_ref, gs_ref, xn_ref, gt_ref, sc_ref, sh_ref, wa_ref, ws_ref, wo_ref,
                lg_ref, lb_ref, wrh_ref, wrl_ref, br_ref, x1_ref, u2_ref, lo_ref):
    m = (jax.nn.sigmoid(ga_ref[...].astype(f32)) * _dot(a_ref[...].astype(bf16), wa_ref[...])
         + jax.nn.sigmoid(gs_ref[...].astype(f32)) * _dot(s_ref[...].astype(bf16), ws_ref[...]))
    o = _dot(m.astype(bf16), wo_ref[...])
    x1 = _layer_norm(DN_ALPHA * xn_ref[...] + gt_ref[0] * o, lg_ref[...], lb_ref[...])
    x1_ref[...] = x1
    u2 = x1 * (1.0 + sc_ref[0]) + sh_ref[0]
    _store_row_tiles(u2_ref, u2)
    hi, mid, lo = _split3(u2)
    wrh = wrh_ref[...]
    wrl = wrl_ref[...]
    lo_ref[...] = (_dot(hi, wrh) + (_dot(hi, wrl) + _dot(mid, wrh)) + (_dot(mid, wrl) + _dot(lo, wrh))
                   + br_ref[...])


def _mixer(a, s, gates, xn, gt, sc, sh, wts, *, tm, tiles_per_mod, name):
    n, d = xn.shape

    def row(w, col=0):
        return pl.BlockSpec((tm, w), lambda i: (i, col))

    def full(arr):
        return pl.BlockSpec(arr.shape, lambda i: (0,) * arr.ndim)

    consts = [wts["w_attn_br"], wts["w_ssd_br"], wts["w_out"], wts["ln_mix_g"], wts["ln_mix_b"],
              wts["w_router_hi"], wts["w_router_lo"], wts["b_router"]]
    return pl.pallas_call(
        _mixer_body,
        grid=(n // tm,),
        in_specs=[row(a.shape[1]), row(s.shape[1]), row(d, 0), row(d, 1), row(d),
                  _mod_spec(gt, tm, tiles_per_mod), _mod_spec(sc, tm, tiles_per_mod),
                  _mod_spec(sh, tm, tiles_per_mod)] + [full(c) for c in consts],
        out_specs=(row(d), pl.BlockSpec((tm * (d // LANES), LANES), lambda i: (i, 0)), row(LANES)),
        out_shape=(jax.ShapeDtypeStruct((n, d), f32), jax.ShapeDtypeStruct((n * (d // LANES), LANES), f32),
                   jax.ShapeDtypeStruct((n, LANES), f32)),
        compiler_params=_params("parallel"),
        name=name,
    )(a, s, gates, gates, xn, gt, sc, sh, *consts)


def _moe_body(be_ref, nv_ref, nr_ref, tok_ref, tok_next_ref, slot_ref, x_hbm, wg_ref, bg_ref, wu_ref, bu_ref, wd_ref,
              bd_ref, y_hbm, xbuf, ybuf, wgb, wub, wdb, sem_in, sem_out, *, bm):
    i = pl.program_id(0)
    nv = nv_ref[0]
    nc = xbuf.shape[1] // bm
    cur = i % 2
    nxt = 1 - cur

    def gather_row(idx_ref, buf, r, priority):
        src = pl.multiple_of(idx_ref[0, 0, r], nc)
        pltpu.make_async_copy(x_hbm.at[pl.ds(src, nc), :], xbuf.at[buf, pl.ds(pl.multiple_of(r * nc, nc), nc), :],
                              sem_in.at[buf]).start(priority=priority)

    def start_gather(idx_ref, buf):
        def body(r2, carry):
            gather_row(idx_ref, buf, 2 * r2, 0)
            gather_row(idx_ref, buf, 2 * r2 + 1, 1)
            return carry
        lax.fori_loop(0, bm // 2, body, 0, unroll=DMA_ISSUE_UNROLL // 2)

    def wait_gather(buf):
        pltpu.make_async_copy(x_hbm.at[pl.ds(0, bm * nc), :], xbuf.at[buf], sem_in.at[buf]).wait()

    def scatter_row(buf, r, slot):
        dst = pl.multiple_of(slot, nc)
        return pltpu.make_async_copy(ybuf.at[buf, pl.ds(pl.multiple_of(r * nc, nc), nc), :],
                                     y_hbm.at[pl.ds(dst, nc), :], sem_out.at[buf])

    def start_scatter(buf, n_real):
        @pl.when(n_real == bm)
        def _():
            def body(r2, carry):
                scatter_row(buf, 2 * r2, slot_ref[0, 0, 2 * r2]).start(priority=0)
                scatter_row(buf, 2 * r2 + 1, slot_ref[0, 0, 2 * r2 + 1]).start(priority=1)
                return carry
            lax.fori_loop(0, bm // 2, body, 0, unroll=DMA_ISSUE_UNROLL // 2)

        @pl.when(n_real < bm)
        def _():
            def body(r, carry):
                scatter_row(buf, r, slot_ref[0, 0, r]).start()
                return carry
            lax.fori_loop(0, n_real, body, 0)

    def wait_scatter(buf, n_real):
        @pl.when(n_real == bm)
        def _():
            pltpu.make_async_copy(ybuf.at[buf], y_hbm.at[pl.ds(0, bm * nc), :], sem_out.at[buf]).wait()

        @pl.when(n_real < bm)
        def _():
            def body(r, carry):
                scatter_row(buf, 0, 0).wait()
                return carry
            lax.fori_loop(0, n_real, body, 0)

    @pl.when((i == 0) & (nv > 0))
    def _():
        start_gather(tok_ref, cur)

    @pl.when(i + 1 < nv)
    def _():
        start_gather(tok_next_ref, nxt)

    @pl.when(i < nv)
    def _():
        @pl.when((i == 0) | (be_ref[i] != be_ref[jnp.maximum(i - 1, 0)]))
        def _():
            wgb[...] = wg_ref[0].astype(bf16)
            wub[...] = wu_ref[0].astype(bf16)
            wdb[...] = wd_ref[0].astype(bf16)

        wait_gather(cur)
        xb = _load_row_tiles(xbuf, bm, nc, (cur,)).astype(bf16)
        gate = jnp.minimum(_dot(xb, wgb[...]) + bg_ref[0], SWIGLU_LIMIT)
        up = jnp.clip(_dot(xb, wub[...]) + bu_ref[0], -SWIGLU_LIMIT, SWIGLU_LIMIT)
        h = gate * jax.nn.sigmoid(SWIGLU_ALPHA * gate) * (up + 1.0)
        y = _dot(h.astype(bf16), wdb[...]) + bd_ref[0]

        @pl.when(i >= 1)
        def _():
            wait_scatter(nxt, nr_ref[jnp.maximum(i - 1, 0)])

        _store_row_tiles(ybuf, y, (cur,))
        start_scatter(cur, nr_ref[i])

        @pl.when(i == nv - 1)
        def _():
            wait_scatter(cur, nr_ref[i])


def _moe(u2, logits, wts, *, bm):
    d_ff, d = wts["w_down"].shape[1:]
    nc = d // LANES
    n_tok = u2.shape[0] // nc
    n_assign = n_tok * TOP_K
    i32 = jnp.int32
    top_logit, top_idx = lax.top_k(logits[:, :N_EXPERTS], TOP_K)
    top_gate = jax.nn.softmax(top_logit, axis=-1)
    e_flat = top_idx.reshape(-1)
    a_bits = (n_assign - 1).bit_length()
    order = jnp.sort(e_flat.astype(i32) * (1 << a_bits) + jnp.arange(n_assign, dtype=i32)) & ((1 << a_bits) - 1)
    experts = jnp.arange(N_EXPERTS, dtype=i32)
    counts = jnp.sum((e_flat[:, None] == experts[None, :]).astype(i32), axis=0)
    starts = jnp.cumsum(counts) - counts
    padded = (counts + bm - 1) // bm * bm
    pad_ends = jnp.cumsum(padded)
    pad_starts = pad_ends - padded
    n_blocks = -(-n_assign // bm) + N_EXPERTS
    blk_row = jnp.arange(n_blocks, dtype=i32) * bm
    block_expert = jnp.minimum(jnp.sum((pad_ends[None, :] <= blk_row[:, None]).astype(i32), axis=1), N_EXPERTS - 1)
    off = blk_row - pad_starts[block_expert]
    n_real = jnp.clip(counts[block_expert] - off, 0, bm)
    j = jnp.arange(bm, dtype=i32)[None, :]
    src = jnp.clip(starts[block_expert][:, None] + off[:, None] + j, 0, n_assign - 1)
    a_id = jnp.take(order, src.reshape(-1), axis=0).reshape(n_blocks, bm)
    real = j < n_real[:, None]
    row_tok = jnp.where(real, a_id // TOP_K, 0) * nc
    row_slot = jnp.where(real, (a_id % TOP_K) * n_tok + a_id // TOP_K, 0) * nc
    n_valid = (pad_ends[-1] // bm).astype(i32).reshape(1)

    def wspec(shape):
        return pl.BlockSpec((1,) + shape, lambda i, be, nv, nr: (be[i], 0, 0))

    idx_spec = pl.BlockSpec((1, 1, bm), lambda i, be, nv, nr: (i, 0, 0), memory_space=pltpu.SMEM)
    idx_next = pl.BlockSpec((1, 1, bm), lambda i, be, nv, nr: (jnp.minimum(i + 1, n_blocks - 1), 0, 0),
                            memory_space=pltpu.SMEM)
    grid_spec = pltpu.PrefetchScalarGridSpec(
        num_scalar_prefetch=3,
        grid=(n_blocks,),
        in_specs=[idx_spec, idx_next, idx_spec,
                  pl.BlockSpec(memory_space=pl.ANY),
                  wspec((d, d_ff)), wspec((1, d_ff)), wspec((d, d_ff)), wspec((1, d_ff)),
                  wspec((d_ff, d)), wspec((1, d))],
        out_specs=pl.BlockSpec(memory_space=pl.ANY),
        scratch_shapes=[pltpu.VMEM((2, bm * nc, LANES), f32), pltpu.VMEM((2, bm * nc, LANES), f32),
                        pltpu.VMEM((d, d_ff), bf16), pltpu.VMEM((d, d_ff), bf16), pltpu.VMEM((d_ff, d), bf16),
                        pltpu.SemaphoreType.DMA((2,)), pltpu.SemaphoreType.DMA((2,))],
    )
    row_tok3 = row_tok.reshape(n_blocks, 1, bm)
    y = pl.pallas_call(
        functools.partial(_moe_body, bm=bm),
        grid_spec=grid_spec,
        out_shape=jax.ShapeDtypeStruct((n_assign * nc, LANES), f32),
        compiler_params=_params("arbitrary"),
        name=f"moe_bm{bm}",
    )(block_expert, n_valid, n_real, row_tok3, row_tok3, row_slot.reshape(n_blocks, 1, bm), u2,
      wts["w_gate"], wts["b_gate"], wts["w_up"], wts["b_up"], wts["w_down"], wts["b_down"])
    return y, top_gate


def _final_body(x1_ref, *rest):
    yk_refs, (pg_ref, gt_ref, g_ref, b_ref, o_ref) = rest[:TOP_K], rest[TOP_K:]
    pg = pg_ref[...]
    tm, d = x1_ref.shape
    y = pg[:, 0:1] * _load_row_tiles(yk_refs[0], tm, d // LANES)
    for k in range(1, TOP_K):
        y = y + pg[:, k:k + 1] * _load_row_tiles(yk_refs[k], tm, d // LANES)
    o_ref[...] = _layer_norm(DN_ALPHA * x1_ref[...] + gt_ref[0] * y, g_ref[...], b_ref[...])


def _final(x1, yk, top_gate, gt, g, b, *, tm, tiles_per_mod, row0, n_all, name):
    n, d = x1.shape
    nt = n // tm
    assert row0 % tm == 0 and n_all % tm == 0
    nt_all, blk0 = n_all // tm, row0 // tm
    row = pl.BlockSpec((tm, d), lambda i: (i, 0))
    vec = pl.BlockSpec((1, d), lambda i: (0, 0))
    planes = [pl.BlockSpec((tm * (d // LANES), LANES),
                           functools.partial(lambda i, k: (k * nt_all + blk0 + i, 0), k=k)) for k in range(TOP_K)]
    return pl.pallas_call(
        _final_body,
        grid=(nt,),
        in_specs=[row] + planes + [pl.BlockSpec((tm, TOP_K), lambda i: (i, 0)),
                                   _mod_spec(gt, tm, tiles_per_mod), vec, vec],
        out_specs=row,
        out_shape=jax.ShapeDtypeStruct((n, d), f32),
        compiler_params=_params("parallel"),
        name=name,
    )(x1, *([yk] * TOP_K), top_gate, gt, g, b)


def _stream(x, c, wts, ssd_prm, *, caches=None, conv_state=None, ssm_state=None):
    bsz, seq, d = x.shape
    n = bsz * seq
    is_prompt = caches is None
    d_inner = wts["w_z"].shape[1]
    d_xbc = wts["w_xbc"].shape[1]

    c_pad = jnp.pad(c, ((0, (-bsz) % SUBLANES), (0, 0)))
    mod = _mm(c_pad, wts["w_ada"], wts["b_ada"], tm=c_pad.shape[0], tn=2 * d, act="silu", name="ada_mod")[:bsz]
    sh1, sc1, gt1, sh2, sc2, gt2 = jnp.split(mod, 6, axis=-1)

    def per_tile(p, tile):
        if seq % tile == 0:
            return p.reshape(bsz, 1, d), seq // tile
        return jnp.broadcast_to(p[:, None, :], (bsz, seq, d)).reshape(n // tile, tile, d), 1

    tm = 512
    tpm = per_tile(sc1, tm)[1]
    xn, u = _ln_mod(x.reshape(n, d), wts["ln_emb_g"], wts["ln_emb_b"], per_tile(sc1, tm)[0], per_tile(sh1, tm)[0],
                    tm=tm, tiles_per_mod=tpm, name="ln_mod")
    tmm = min(n, 2048)
    act_dtype = bf16 if seq % 16 == 0 else f32
    qkv = _mm(u, wts["w_qkv"], wts["zero_b"][:, :QKV_W], tm=tmm, tn=3 * GROUP_W, name="proj_qkv")
    z = _mm(u, wts["w_z"], wts["zero_b"][:, :d_inner], tm=tmm, tn=d_inner // 2, out_dtype=act_dtype, name="proj_z")
    xbc = _mm(u, wts["w_xbc"], wts["zero_b"][:, :d_xbc], tm=tmm, tn=d_xbc // 2, name="proj_xbc")
    dt = _mm(u, wts["w_dt"], wts["zero_b"][:, :LANES], tm=tmm, tn=LANES, name="proj_dt")
    gates = _mm(u, wts["w_gates"], wts["zero_b"][:, :2 * d], tm=tmm, tn=d, out_dtype=bf16, name="proj_gates")

    qkv3 = qkv.reshape(bsz, seq, QKV_W)
    xbc3 = xbc.reshape(bsz, seq, d_xbc)

    def kv_rows_of(g, keep):
        cols = qkv3[:, seq - keep:, (g * 3 + 1) * GROUP_W:(g * 3 + 3) * GROUP_W]
        return cols.reshape(bsz, keep, 2, N_HEADS, HEAD_DIM)

    if is_prompt:
        a = _attn_prompt(qkv, bsz, seq)
        s, h_last = _ssd_prompt(xbc, z, dt, ssd_prm, bsz, seq, chunk=128)
        kv_rows = [kv_rows_of(g, min(w, seq)) for g, (w, _) in enumerate(ATTN_GROUPS)]
        new_conv = xbc3[:, -(CONV_W - 1):]
    else:
        a = _attn_sample(qkv, caches, bsz, seq)
        cs8 = jnp.pad(conv_state, ((0, 0), (SUBLANES - (CONV_W - 1), 0), (0, 0)))
        s, h_last = _ssd_sample(xbc, cs8, z, dt, ssm_state.reshape(bsz, d_inner, D_STATE), ssd_prm, bsz, seq)
        kv_rows = [kv_rows_of(g, seq) for g in range(N_GROUPS)]
        new_conv = jnp.concatenate([conv_state, xbc3], axis=1)[:, -(CONV_W - 1):]
    new_ssm = h_last.reshape(bsz, d_inner // SSM_HEAD_DIM, SSM_HEAD_DIM, D_STATE)

    tmx = 512
    x1, u2, logits = _mixer(a, s, gates, xn, per_tile(gt1, tmx)[0], per_tile(sc2, tmx)[0], per_tile(sh2, tmx)[0],
                            wts, tm=tmx, tiles_per_mod=per_tile(gt1, tmx)[1], name="mixer")

    def finish(yk, top_gate, row0, n_all):
        y = _final(x1, yk, top_gate[row0:row0 + n], per_tile(gt2, tm)[0], wts["ln_ffn_g"], wts["ln_ffn_b"], tm=tm,
                   tiles_per_mod=tpm, row0=row0, n_all=n_all, name="final_ln")
        return y.reshape(bsz, seq, d)

    return (u2, logits, finish), kv_rows, new_conv, new_ssm


def kernel(x_prompt, x_sample, c_prompt, c_sample, cache_kv_w128, cache_kv_w512, cache_kv_w2048, state_conv, state_ssm, ln_emb_g, ln_emb_b, w_ada, b_ada, w_in, conv_w, conv_b, dt_bias, a_log, d_skip, ssm_norm_g, w_attn_br, w_ssd_br, w_out, ln_mix_g, ln_mix_b, w_router, b_router, w_gate, b_gate, w_up, b_up, w_down, b_down, ln_ffn_g, ln_ffn_b):
    d = x_prompt.shape[-1]
    d_inner = ssm_norm_g.shape[-1]
    d_xbc = conv_w.shape[-1]
    n_heads = dt_bias.shape[-1]
    lyr = 0

    def rowv(v):
        return v.reshape(1, -1).astype(f32)

    def lane_pad(v):
        return jnp.pad(v.astype(f32), (0, LANES - n_heads)).reshape(1, LANES)

    wi = w_in[lyr]
    o0 = QKV_W
    o1 = o0 + d_inner
    o2 = o1 + d_xbc
    o3 = o2 + n_heads
    wr = jnp.pad(w_router[lyr], ((0, 0), (0, LANES - N_EXPERTS)))
    wr_hi = wr.astype(bf16)
    wts = {
        "ln_emb_g": rowv(ln_emb_g), "ln_emb_b": rowv(ln_emb_b),
        "w_ada": w_ada[lyr].astype(bf16), "b_ada": rowv(b_ada[lyr]),
        "w_qkv": wi[:, :o0].astype(bf16), "w_z": wi[:, o0:o1].astype(bf16), "w_xbc": wi[:, o1:o2].astype(bf16),
        "w_dt": jnp.pad(wi[:, o2:o3], ((0, 0), (0, LANES - n_heads))).astype(bf16),
        "w_gates": wi[:, o3:].astype(bf16),
        "zero_b": jnp.zeros((1, max(QKV_W, d_xbc, 2 * d)), f32),
        "w_attn_br": w_attn_br[lyr].astype(bf16), "w_ssd_br": w_ssd_br[lyr].astype(bf16),
        "w_out": w_out[lyr].astype(bf16),
        "ln_mix_g": rowv(ln_mix_g[lyr]), "ln_mix_b": rowv(ln_mix_b[lyr]),
        "w_router_hi": wr_hi, "w_router_lo": (wr - wr_hi.astype(f32)).astype(bf16),
        "b_router": jnp.pad(b_router[lyr], (0, LANES - N_EXPERTS)).reshape(1, LANES),
        "w_gate": w_gate[lyr], "b_gate": b_gate[lyr][:, None, :],
        "w_up": w_up[lyr], "b_up": b_up[lyr][:, None, :],
        "w_down": w_down[lyr], "b_down": b_down[lyr][:, None, :],
        "ln_ffn_g": rowv(ln_ffn_g[lyr]), "ln_ffn_b": rowv(ln_ffn_b[lyr]),
    }
    head_of_lane = jnp.arange(d_inner, dtype=jnp.int32) // SSM_HEAD_DIM
    ssd_prm = {
        "conv_w": conv_w[lyr], "conv_b": rowv(conv_b[lyr]),
        "dtb_row": lane_pad(dt_bias[lyr]), "alog_row": lane_pad(a_log[lyr]),
        "dtb_t": jnp.broadcast_to(dt_bias[lyr][:, None], (n_heads, LANES)),
        "alog_t": jnp.broadcast_to(a_log[lyr][:, None], (n_heads, LANES)),
        "dskip_x": jnp.repeat(d_skip[lyr], SSM_HEAD_DIM).reshape(1, d_inner),
        "norm_g": rowv(ssm_norm_g[lyr]),
        "ex": (jnp.arange(LANES, dtype=jnp.int32)[:, None] == head_of_lane[None, :]).astype(bf16),
        "rsel": (head_of_lane[:, None] == jnp.arange(LANES, dtype=jnp.int32)[None, :]).astype(bf16),
    }

    (u2_p, logits_p, finish_p), kv_p, conv_p, ssm_p = _stream(x_prompt, c_prompt, wts, ssd_prm)
    (u2_s, logits_s, finish_s), kv_s, conv_s, ssm_s = _stream(
        x_sample, c_sample, wts, ssd_prm,
        caches=(cache_kv_w128[lyr], cache_kv_w512[lyr], cache_kv_w2048[lyr]),
        conv_state=state_conv[lyr], ssm_state=state_ssm[lyr])
    n_p, n_s = logits_p.shape[0], logits_s.shape[0]
    yk, top_gate = _moe(jnp.concatenate([u2_p, u2_s], axis=0), jnp.concatenate([logits_p, logits_s], axis=0), wts,
                        bm=MOE_BLOCK_ROWS)
    yp = finish_p(yk, top_gate, 0, n_p + n_s)
    ys = finish_s(yk, top_gate, n_p, n_p + n_s)
    return (yp, ys, kv_p[0][None], kv_s[0][None], kv_p[1][None], kv_s[1][None], kv_p[2][None], kv_s[2][None],
            conv_p[None], conv_s[None], ssm_p[None], ssm_s[None])
```
